```python
import jax, jax.numpy as jnp
from jax import lax
import numpy as np

D_MODEL = 1024
BATCH = 4
SEQ = 4096
DEPTH = 1
DEC_BATCH = 128
DEC_SEQ = 4
PAST_LEN = 8192
PAGE_SIZE = 128

N_META = 16
RET_HEADS = 4
RET_DK = 128
RET_DV = 128
RET_CHUNK = 128
MLA_HEADS = 4
Q_LORA = 384
KV_LORA = 256
QK_NOPE = 128
QK_ROPE = 64
V_DIM = 128
D_FF = 2816
CONV_W = 3
ATTN_BLOCK = 128
ROPE_THETA = 10000.0
EPS = 1e-6
RET_W = RET_HEADS * RET_DV
MLA_W = MLA_HEADS * V_DIM
MIX_W = RET_W + MLA_W
SOFTMAX_SCALE = (QK_NOPE + QK_ROPE) ** -0.5
IN_WIDTHS = (RET_HEADS * RET_DK, RET_HEADS * RET_DK, RET_W, RET_W, Q_LORA, KV_LORA, QK_ROPE)
IN_W = sum(IN_WIDTHS)
IN_SPLITS = tuple(int(s) for s in np.cumsum(IN_WIDTHS)[:-1])

kernel_name = 'hybrid_retention_mla_convffn_step'


def rmsnorm(x, g):
    xf = x.astype(jnp.float32)
    y = xf * lax.rsqrt(jnp.mean(xf * xf, axis=-1, keepdims=True) + EPS)
    return (y * g.astype(jnp.float32)).astype(x.dtype)


def rope(x, pos):
    d = x.shape[-1]
    inv = ROPE_THETA ** (-jnp.arange(0, d, 2, dtype=jnp.float32) / d)
    ang = pos.astype(jnp.float32)[:, None] * inv[None, :]
    cos = jnp.cos(ang)[:, None, :]
    sin = jnp.sin(ang)[:, None, :]
    xf = x.astype(jnp.float32)
    x1, x2 = xf[..., : d // 2], xf[..., d // 2:]
    return jnp.concatenate([x1 * cos - x2 * sin, x1 * sin + x2 * cos], axis=-1).astype(x.dtype)


def retention_log_decay():
    return jnp.log1p(-jnp.exp2(-5.0 - jnp.arange(RET_HEADS, dtype=jnp.float32)))


def mixer_inputs(h, pos, w_in, g_q, w_uq, g_kv, w_uk):
    B, T, _ = h.shape
    z = h @ w_in
    q_r, k_r, v_r, g_r, c_q, c_kv, k_pe = jnp.split(z, IN_SPLITS, axis=-1)
    q_r = rope(q_r.reshape(B, T, RET_HEADS, RET_DK), pos)
    k_r = rope(k_r.reshape(B, T, RET_HEADS, RET_DK), pos) * (RET_DK ** -0.5)
    v_r = v_r.reshape(B, T, RET_HEADS, RET_DV)
    q = (rmsnorm(c_q, g_q) @ w_uq).reshape(B, T, MLA_HEADS, QK_NOPE + QK_ROPE)
    q_pe = rope(q[..., QK_NOPE:], pos)
    q_lat = jnp.einsum('bthn,chn->bthc', q[..., :QK_NOPE], w_uk)
    c_kv = rmsnorm(c_kv, g_kv)
    k_pe = rope(k_pe[:, :, None, :], pos)[:, :, 0, :]
    return q_r, k_r, v_r, g_r, q_lat, q_pe, c_kv, k_pe


def retention_chunk(state, q, k, v, log_gamma):
    L = q.shape[1]
    n = jnp.arange(L, dtype=jnp.float32)
    diff = n[:, None] - n[None, :]
    decay = jnp.where(diff >= 0, jnp.exp(log_gamma[:, None, None] * jnp.maximum(diff, 0.0)), 0.0)
    qf, kf, vf = q.astype(jnp.float32), k.astype(jnp.float32), v.astype(jnp.float32)
    scores = jnp.einsum('blhd,bmhd->bhlm', qf, kf) * decay[None]
    inner = jnp.einsum('bhlm,bmhe->blhe', scores, vf)
    q_dec = qf * jnp.exp((n + 1.0)[:, None] * log_gamma[None, :])[None, :, :, None]
    cross = jnp.einsum('blhd,bhde->blhe', q_dec, state)
    k_dec = kf * jnp.exp((L - 1.0 - n)[:, None] * log_gamma[None, :])[None, :, :, None]
    new_state = jnp.exp(L * log_gamma)[None, :, None, None] * state + jnp.einsum('blhd,blhe->bhde', k_dec, vf)
    return new_state, inner + cross


def retention_prompt(q, k, v, log_gamma):
    B, T = q.shape[0], q.shape[1]
    state0 = jnp.zeros((B, RET_HEADS, RET_DK, RET_DV), jnp.float32)
    state, o_meta = retention_chunk(state0, q[:, :N_META], k[:, :N_META], v[:, :N_META], log_gamma)
    nc = (T - N_META) // RET_CHUNK

    def to_chunks(a):
        return a[:, N_META:].reshape(B, nc, RET_CHUNK, a.shape[2], a.shape[3]).swapaxes(0, 1)

    def step(s, xs):
        return retention_chunk(s, xs[0], xs[1], xs[2], log_gamma)

    state, o = lax.scan(step, state, (to_chunks(q), to_chunks(k), to_chunks(v)))
    o = o.swapaxes(0, 1).reshape(B, nc * RET_CHUNK, RET_HEADS, RET_DV)
    return jnp.concatenate([o_meta, o], axis=1), state


def mla_attend(q_lat, q_pe, segments):
    qf = q_lat.astype(jnp.float32)
    pf = q_pe.astype(jnp.float32)
    scores = []
    for c_kv, k_pe, mask in segments:
        s = (jnp.einsum('bqhc,bkc->bhqk', qf, c_kv.astype(jnp.float32))
             + jnp.einsum('bqhr,bkr->bhqk', pf, k_pe.astype(jnp.float32))) * SOFTMAX_SCALE
        scores.append(jnp.where(mask, s, -jnp.inf))
    p = jax.nn.softmax(jnp.concatenate(scores, axis=-1), axis=-1)
    out = 0.0
    start = 0
    for c_kv, _, _ in segments:
        n = c_kv.shape[1]
        out = out + jnp.einsum('bhqk,bkc->bqhc', p[..., start:start + n], c_kv.astype(jnp.float32))
        start += n
    return out


def mla_prompt(q_lat, q_pe, c_kv, k_pe):
    B, T = q_lat.shape[0], q_lat.shape[1]
    mpos = jnp.arange(N_META)
    out_meta = mla_attend(q_lat[:, :N_META], q_pe[:, :N_META],
                          ((c_kv[:, :N_META], k_pe[:, :N_META], mpos[None, :] <= mpos[:, None]),))
    nb = (T - N_META) // ATTN_BLOCK
    kpos = jnp.arange(T)
    qb = q_lat[:, N_META:].reshape(B, nb, ATTN_BLOCK, MLA_HEADS, KV_LORA).swapaxes(0, 1)
    pb = q_pe[:, N_META:].reshape(B, nb, ATTN_BLOCK, MLA_HEADS, QK_ROPE).swapaxes(0, 1)
    starts = N_META + jnp.arange(nb) * ATTN_BLOCK

    def block(args):
        ql, qp, s0 = args
        qpos = s0 + jnp.arange(ATTN_BLOCK)
        return mla_attend(ql, qp, ((c_kv, k_pe, kpos[None, :] <= qpos[:, None]),))

    out = lax.map(block, (qb, pb, starts))
    out = out.swapaxes(0, 1).reshape(B, nb * ATTN_BLOCK, MLA_HEADS, KV_LORA)
    return jnp.concatenate([out_meta, out], axis=1)


def mixer_output(o_r, g_r, o_lat, w_uv, w_out):
    B, T = g_r.shape[0], g_r.shape[1]
    of = o_r.astype(jnp.float32)
    of = of * lax.rsqrt(jnp.mean(of * of, axis=-1, keepdims=True) + EPS)
    ret = of.reshape(B, T, RET_W) * jax.nn.silu(g_r.astype(jnp.float32))
    mla = jnp.einsum('bthc,chv->bthv', o_lat, w_uv.astype(jnp.float32)).reshape(B, T, MLA_W)
    return jnp.concatenate([ret, mla], axis=-1).astype(w_out.dtype) @ w_out


def conv_ffn(h, buf, w_up, conv_w, conv_b, w_down):
    u = h @ w_up
    T = u.shape[1]
    up = jnp.concatenate([buf.astype(u.dtype), u], axis=1)
    c = conv_b
    for j in range(CONV_W):
        c = c + conv_w[j] * up[:, j:j + T]
    a, g = jnp.split(c, 2, axis=-1)
    y = (jax.nn.silu(g) * a) @ w_down
    return y, up[:, up.shape[1] - (CONV_W - 1):]


def setup_inputs(seed: int = 0) -> dict:
    key = jax.random.key(seed)
    ks = jax.random.split(key, 24)
    n_pages = PAST_LEN // PAGE_SIZE
    n_used = DEC_BATCH * n_pages
    n_pool = n_used + n_used // 4

    def nrm(k, shape, scale=1.0):
        return jax.random.normal(k, shape, jnp.float32) * scale

    page_table = jax.random.permutation(ks[6], n_pool)[:n_used].reshape(DEC_BATCH, n_pages).astype(jnp.int32)
    return {
        'x_prompt': nrm(ks[0], (BATCH, SEQ, D_MODEL)),
        'x_sample': nrm(ks[1], (DEC_BATCH, DEC_SEQ, D_MODEL)),
        'cache_kv_latent': nrm(ks[2], (DEPTH, n_pool, PAGE_SIZE, KV_LORA)),
        'cache_k_rope': nrm(ks[3], (DEPTH, n_pool, PAGE_SIZE, QK_ROPE)),
        'state_retention': nrm(ks[4], (DEPTH, DEC_BATCH, RET_HEADS, RET_DK, RET_DV), 0.3),
        'state_ffn_conv': nrm(ks[5], (DEPTH, DEC_BATCH, CONV_W - 1, 2 * D_FF)),
        'page_table': page_table,
        'meta_tokens': nrm(ks[7], (N_META, D_MODEL)),
        'g_mix': 1.0 + nrm(ks[8], (DEPTH, D_MODEL), 0.01),
        'w_in': nrm(ks[9], (DEPTH, D_MODEL, IN_W), D_MODEL ** -0.5),
        'g_q': 1.0 + nrm(ks[10], (DEPTH, Q_LORA), 0.01),
        'w_uq': nrm(ks[11], (DEPTH, Q_LORA, MLA_HEADS * (QK_NOPE + QK_ROPE)), Q_LORA ** -0.5),
        'g_kv': 1.0 + nrm(ks[12], (DEPTH, KV_LORA), 0.01),
        'w_uk': nrm(ks[13], (DEPTH, KV_LORA, MLA_HEADS, QK_NOPE), KV_LORA ** -0.5),
        'w_uv': nrm(ks[14], (DEPTH, KV_LORA, MLA_HEADS, V_DIM), KV_LORA ** -0.5),
        'w_out': nrm(ks[15], (DEPTH, MIX_W, D_MODEL), MIX_W ** -0.5),
        'g_ffn': 1.0 + nrm(ks[16], (DEPTH, D_MODEL), 0.01),
        'w_up': nrm(ks[17], (DEPTH, D_MODEL, 2 * D_FF), D_MODEL ** -0.5),
        'conv_w': nrm(ks[18], (DEPTH, CONV_W, 2 * D_FF), CONV_W ** -0.5),
        'conv_b': nrm(ks[19], (DEPTH, 2 * D_FF), 0.01),
        'w_down': nrm(ks[20], (DEPTH, D_FF, D_MODEL), D_FF ** -0.5),
        'g_final': 1.0 + nrm(ks[21], (D_MODEL,), 0.01),
    }


def reference(x_prompt, x_sample, cache_kv_latent, cache_k_rope, state_retention, state_ffn_conv,
              page_table, meta_tokens, g_mix, w_in, g_q, w_uq, g_kv, w_uk, w_uv, w_out,
              g_ffn, w_up, conv_w, conv_b, w_down, g_final):
    B = x_prompt.shape[0]
    hp = jnp.concatenate([jnp.broadcast_to(meta_tokens.astype(x_prompt.dtype)[None], (B, N_META, D_MODEL)),
                          x_prompt], axis=1)
    Tp = hp.shape[1]
    pos_p = jnp.arange(Tp)
    hs = x_sample
    DB, Ls = hs.shape[0], hs.shape[1]
    past_len = page_table.shape[1] * PAGE_SIZE
    pos_s = past_len + jnp.arange(Ls)
    log_gamma = retention_log_decay()
    past_mask = jnp.ones((Ls, past_len), dtype=bool)
    self_mask = jnp.tril(jnp.ones((Ls, Ls), dtype=bool))

    kv_p, pe_p, ret_p, conv_p = [], [], [], []
    kv_s, pe_s, ret_s, conv_s = [], [], [], []
    for l in range(DEPTH):
        a = rmsnorm(hp, g_mix[l])
        q_r, k_r, v_r, g_r, q_lat, q_pe, c_kv, k_pe = mixer_inputs(a, pos_p, w_in[l], g_q[l], w_uq[l], g_kv[l], w_uk[l])
        o_r, st = retention_prompt(q_r, k_r, v_r, log_gamma)
        o_lat = mla_prompt(q_lat, q_pe, c_kv, k_pe)
        hp = hp + mixer_output(o_r, g_r, o_lat, w_uv[l], w_out[l])
        f, buf = conv_ffn(rmsnorm(hp, g_ffn[l]), jnp.zeros((B, CONV_W - 1, 2 * D_FF), hp.dtype),
                          w_up[l], conv_w[l], conv_b[l], w_down[l])
        hp = hp + f
        kv_p.append(c_kv)
        pe_p.append(k_pe)
        ret_p.append(st.astype(state_retention.dtype))
        conv_p.append(buf.astype(state_ffn_conv.dtype))

        a = rmsnorm(hs, g_mix[l])
        q_r, k_r, v_r, g_r, q_lat, q_pe, c_kv, k_pe = mixer_inputs(a, pos_s, w_in[l], g_q[l], w_uq[l], g_kv[l], w_uk[l])
        st, o_r = retention_chunk(state_retention[l].astype(jnp.float32), q_r, k_r, v_r, log_gamma)
        past_kv = cache_kv_latent[l][page_table].reshape(DB, past_len, KV_LORA)
        past_pe = cache_k_rope[l][page_table].reshape(DB, past_len, QK_ROPE)
        o_lat = mla_attend(q_lat, q_pe, ((past_kv, past_pe, past_mask), (c_kv, k_pe, self_mask)))
        hs = hs + mixer_output(o_r, g_r, o_lat, w_uv[l], w_out[l])
        f, buf = conv_ffn(rmsnorm(hs, g_ffn[l]), state_ffn_conv[l], w_up[l], conv_w[l], conv_b[l], w_down[l])
        hs = hs + f
        kv_s.append(c_kv)
        pe_s.append(k_pe)
        ret_s.append(st.astype(state_retention.dtype))
        conv_s.append(buf.astype(state_ffn_conv.dtype))

    y_prompt = rmsnorm(hp[:, N_META:], g_final)
    y_sample = rmsnorm(hs, g_final)
    return (y_prompt, y_sample,
            jnp.stack(kv_p), jnp.stack(pe_p), jnp.stack(ret_p), jnp.stack(conv_p),
            jnp.stack(kv_s), jnp.stack(pe_s), jnp.stack(ret_s), jnp.stack(conv_s))
```

```python
import functools
import math

import jax
import jax.numpy as jnp
from jax import lax
from jax.experimental import pallas as pl
from jax.experimental.pallas import tpu as pltpu

F32 = jnp.float32
BF16 = jnp.bfloat16

N_META = 16
RET_HEADS = 4
RET_DK = 128
RET_DV = 128
MLA_HEADS = 4
Q_LORA = 384
KV_LORA = 256
QK_NOPE = 128
QK_ROPE = 64
V_DIM = 128
CONV_W = 3
ROPE_THETA = 10000.0
EPS = 1e-6
SOFTMAX_SCALE = (QK_NOPE + QK_ROPE) ** -0.5
RET_W = RET_HEADS * RET_DV
LANES = 128
ROW_TILE = 16
QCAT_W = KV_LORA + LANES
LOG_GAMMA = tuple(math.log1p(-(2.0 ** (-5.0 - h))) for h in range(RET_HEADS))
VMEM_LIMIT = 56 * 1024 * 1024
NEG_INF = float("-inf")


def _const_spec(shape):
    nd = len(shape)
    return pl.BlockSpec(shape, lambda *_: (0,) * nd, pipeline_mode=pl.Buffered(1))


def _rms(x, g):
    return x * lax.rsqrt(jnp.mean(x * x, axis=-1, keepdims=True) + EPS) * g


def _proj_kernel(x_ref, c128_ref, s128_ref, c64_ref, s64a_ref, s64b_ref,
                 gmix_ref, win_ref, gq_ref, wuq_ref, gkv_ref, wuk_ref,
                 qr_ref, kr_ref, vr_ref, gr_ref, qcat_ref, kvcat_ref, ckv_ref, kpe_ref):
    a = _rms(x_ref[...], gmix_ref[...])
    z = jnp.dot(a.astype(BF16), win_ref[...], preferred_element_type=F32)
    c128 = c128_ref[...]
    s128 = s128_ref[...]
    c64 = c64_ref[...]
    s64a = s64a_ref[...]
    s64b = s64b_ref[...]

    def rope128(v):
        return v * c128 + pltpu.roll(v, 64, 1) * s128

    def rope64(v):
        return v * c64 + pltpu.roll(v, 96, 1) * s64a + pltpu.roll(v, 32, 1) * s64b

    for h in range(RET_HEADS):
        sl = slice(h * RET_DK, (h + 1) * RET_DK)
        qr_ref[:, sl] = rope128(z[:, sl]).astype(BF16)
        ksl = slice(RET_W + h * RET_DK, RET_W + (h + 1) * RET_DK)
        kr_ref[:, sl] = (rope128(z[:, ksl]) * (RET_DK ** -0.5)).astype(BF16)
    vr_ref[...] = z[:, 2 * RET_W:3 * RET_W].astype(BF16)
    gr_ref[...] = z[:, 3 * RET_W:4 * RET_W].astype(BF16)

    o = 4 * RET_W
    cqn = _rms(z[:, o:o + Q_LORA], gq_ref[...])
    q2 = jnp.dot(cqn.astype(BF16), wuq_ref[...], preferred_element_type=F32)
    for h in range(MLA_HEADS):
        b0 = h * 2 * LANES
        nope = q2[:, b0:b0 + QK_NOPE]
        pe = rope64(q2[:, b0 + LANES:b0 + 2 * LANES])
        qlat = jnp.dot(nope.astype(BF16), wuk_ref[h], preferred_element_type=F32)
        qcat_ref[h, :, 0:KV_LORA] = (qlat * SOFTMAX_SCALE).astype(BF16)
        qcat_ref[h, :, KV_LORA:QCAT_W] = (pe * SOFTMAX_SCALE).astype(BF16)

    o += Q_LORA
    ckvn = _rms(z[:, o:o + KV_LORA], gkv_ref[...])
    ckv_ref[...] = ckvn
    o += KV_LORA
    kp = rope64(z[:, o:o + LANES])
    kpe_ref[...] = kp[:, 0:QK_ROPE]
    kvcat_ref[:, 0:KV_LORA] = ckvn.astype(BF16)
    kvcat_ref[:, KV_LORA:QCAT_W] = kp.astype(BF16)


def _project(x, tabs, wts, tm):
    n, d = x.shape
    nt = tabs[0].shape[0] // tm
    gmix, win, gq, wuq, gkv, wuk = wts
    row = lambda w: pl.BlockSpec((tm, w), lambda i: (i, 0))
    tab = pl.BlockSpec((tm, LANES), lambda i: (i % nt, 0))
    out_shapes = (
        jax.ShapeDtypeStruct((n, RET_W), BF16), jax.ShapeDtypeStruct((n, RET_W), BF16),
        jax.ShapeDtypeStruct((n, RET_W), BF16), jax.ShapeDtypeStruct((n, RET_W), BF16),
        jax.ShapeDtypeStruct((MLA_HEADS, n, QCAT_W), BF16),
        jax.ShapeDtypeStruct((n, QCAT_W), BF16),
        jax.ShapeDtypeStruct((n, KV_LORA), F32),
        jax.ShapeDtypeStruct((n, QK_ROPE), F32),
    )
    out_specs = (row(RET_W), row(RET_W), row(RET_W), row(RET_W),
                 pl.BlockSpec((MLA_HEADS, tm, QCAT_W), lambda i: (0, i, 0)),
                 row(QCAT_W), row(KV_LORA), row(QK_ROPE))
    return pl.pallas_call(
        _proj_kernel,
        grid=(n // tm,),
        in_specs=[row(d), tab, tab, tab, tab, tab,
                  _const_spec(gmix.shape), _const_spec(win.shape), _const_spec(gq.shape),
                  _const_spec(wuq.shape), _const_spec(gkv.shape), _const_spec(wuk.shape)],
        out_specs=out_specs,
        out_shape=out_shapes,
        compiler_params=pltpu.CompilerParams(dimension_semantics=("arbitrary",),
                                             vmem_limit_bytes=VMEM_LIMIT),
        name="proj",
    )(x, *tabs, gmix, win, gq, wuq, gkv, wuk)


def _ret_chunk_kernel(q_ref, k_ref, v_ref, g_ref, s0_ref, o_ref, sf_ref, st_ref, *, lv):
    c = pl.program_id(1)
    L = q_ref.shape[0]

    @pl.when(c == 0)
    def _():
        st_ref[...] = s0_ref[0]

    li = lax.broadcasted_iota(jnp.int32, (L, L), 0)
    mi = lax.broadcasted_iota(jnp.int32, (L, L), 1)
    diff = (li - mi).astype(F32)
    n = lax.broadcasted_iota(jnp.int32, (L, 1), 0).astype(F32)
    for h in range(RET_HEADS):
        lg = LOG_GAMMA[h]
        sl = slice(h * RET_DK, (h + 1) * RET_DK)
        q = q_ref[:, sl]
        k = k_ref[:, sl]
        v = v_ref[:, sl]
        decay = jnp.where(diff >= 0, jnp.exp(lg * jnp.maximum(diff, 0.0)), 0.0)
        s = lax.dot_general(q, k, (((1,), (1,)), ((), ())), preferred_element_type=F32) * decay
        inner = jnp.dot(s.astype(BF16), v, preferred_element_type=F32)
        state = st_ref[h]
        qd = (q.astype(F32) * jnp.exp(lg * (n + 1.0))).astype(BF16)
        cross = jnp.dot(qd, state.astype(BF16), preferred_element_type=F32)
        o = inner + cross
        kdec = jnp.where(n < lv, jnp.exp(lg * jnp.maximum(lv - 1.0 - n, 0.0)), 0.0)
        kd = (k.astype(F32) * kdec).astype(BF16)
        st_ref[h] = math.exp(lg * lv) * state + lax.dot_general(
            kd, v, (((0,), (0,)), ((), ())), preferred_element_type=F32)
        of = o * lax.rsqrt(jnp.mean(o * o, axis=-1, keepdims=True) + EPS)
        g = g_ref[:, sl].astype(F32)
        o_ref[:, sl] = (of * (g * jax.nn.sigmoid(g))).astype(BF16)

    @pl.when(c == pl.num_programs(1) - 1)
    def _():
        sf_ref[0] = st_ref[...]


def _retention_chunks(qr, kr, vr, gr, s0, nb, L, lv, shared_s0):
    n = qr.shape[0]
    nc = n // (nb * L)
    row = pl.BlockSpec((L, RET_W), lambda b, c: (b * nc + c, 0))
    st_block = (1, RET_HEADS, RET_DK, RET_DV)
    s0_map = (lambda b, c: (0, 0, 0, 0)) if shared_s0 else (lambda b, c: (b, 0, 0, 0))
    return pl.pallas_call(
        functools.partial(_ret_chunk_kernel, lv=float(lv)),
        grid=(nb, nc),
        in_specs=[row, row, row, row, pl.BlockSpec(st_block, s0_map)],
        out_specs=(row, pl.BlockSpec(st_block, lambda b, c: (b, 0, 0, 0))),
        out_shape=(jax.ShapeDtypeStruct((n, RET_W), BF16),
                   jax.ShapeDtypeStruct((nb, RET_HEADS, RET_DK, RET_DV), F32)),
        scratch_shapes=[pltpu.VMEM((RET_HEADS, RET_DK, RET_DV), F32)],
        compiler_params=pltpu.CompilerParams(dimension_semantics=("arbitrary", "arbitrary"),
                                             vmem_limit_bytes=VMEM_LIMIT),
        name="ret_chunks",
    )(qr, kr, vr, gr, s0)


def _ret_decode_kernel(q_ref, k_ref, v_ref, g_ref, s_ref, o_ref, sn_ref, *, ls):
    R = q_ref.shape[0]
    per_tile = ROW_TILE // ls
    li = lax.broadcasted_iota(jnp.int32, (R, R), 0)
    mi = lax.broadcasted_iota(jnp.int32, (R, R), 1)
    same = (li // ls) == (mi // ls)
    diff = ((li % ls) - (mi % ls)).astype(F32)
    t_col = (lax.broadcasted_iota(jnp.int32, (R, 1), 0) % ls).astype(F32)
    seq_in_tile = lax.broadcasted_iota(jnp.int32, (ROW_TILE, 1), 0) // ls
    for h in range(RET_HEADS):
        lg = LOG_GAMMA[h]
        sl = slice(h * RET_DK, (h + 1) * RET_DK)
        q = q_ref[:, sl]
        k = k_ref[:, sl]
        v = v_ref[:, sl]
        decay = jnp.where(same & (diff >= 0), jnp.exp(lg * jnp.maximum(diff, 0.0)), 0.0)
        s = lax.dot_general(q, k, (((1,), (1,)), ((), ())), preferred_element_type=F32) * decay
        inner = jnp.dot(s.astype(BF16), v, preferred_element_type=F32)
        qd = q.astype(F32) * jnp.exp(lg * (t_col + 1.0))
        kd = k.astype(F32) * jnp.exp(lg * (ls - 1.0 - t_col))
        sdec = math.exp(lg * ls)
        cross_tiles = []
        for tt in range(R // ROW_TILE):
            rows = slice(tt * ROW_TILE, (tt + 1) * ROW_TILE)
            qd_t = qd[rows].astype(BF16)
            kd_t = kd[rows]
            v_t = v[rows]
            acc = jnp.zeros((ROW_TILE, RET_DV), F32)
            for j in range(per_tile):
                b = tt * per_tile + j
                state = s_ref[b, h]
                mine = seq_in_tile == j
                cr = jnp.dot(qd_t, state.astype(BF16), preferred_element_type=F32)
                acc = jnp.where(mine, cr, acc)
                kdm = jnp.where(mine, kd_t, 0.0).astype(BF16)
                sn_ref[b, h] = sdec * state + lax.dot_general(
                    kdm, v_t, (((0,), (0,)), ((), ())), preferred_element_type=F32)
            cross_tiles.append(acc)
        o = inner + jnp.concatenate(cross_tiles, axis=0)
        of = o * lax.rsqrt(jnp.mean(o * o, axis=-1, keepdims=True) + EPS)
        g = g_ref[:, sl].astype(F32)
        o_ref[:, sl] = (of * (g * jax.nn.sigmoid(g))).astype(BF16)


def _retention_decode(qr, kr, vr, gr, state, ls, rows):
    n = qr.shape[0]
    nseq = rows // ls
    row = pl.BlockSpec((rows, RET_W), lambda i: (i, 0))
    st = pl.BlockSpec((nseq, RET_HEADS, RET_DK, RET_DV), lambda i: (i, 0, 0, 0))
    return pl.pallas_call(
        functools.partial(_ret_decode_kernel, ls=ls),
        grid=(n // rows,),
        in_specs=[row, row, row, row, st],
        out_specs=(row, st),
        out_shape=(jax.ShapeDtypeStruct((n, RET_W), BF16),
                   jax.ShapeDtypeStruct(state.shape, F32)),
        compiler_params=pltpu.CompilerParams(dimension_semantics=("arbitrary",),
                                             vmem_limit_bytes=VMEM_LIMIT),
        name="ret_decode",
    )(qr, kr, vr, gr, state)


def _softmax_step(q, kc, mask, m_prev, l_prev, acc_prev):
    s = lax.dot_general(q, kc, (((1,), (1,)), ((), ())), preferred_element_type=F32)
    if mask is not None:
        s = jnp.where(mask, s, NEG_INF)
    m_new = jnp.maximum(m_prev, jnp.max(s, axis=-1, keepdims=True))
    alpha = jnp.exp(m_prev - m_new)
    p = jnp.exp(s - m_new)
    l_new = alpha * l_prev + jnp.sum(p, axis=-1, keepdims=True)
    acc_new = alpha * acc_prev + jnp.dot(p.astype(BF16), kc[:, 0:KV_LORA],
                                         preferred_element_type=F32)
    return m_new, l_new, acc_new


def _mla_prompt_kernel(q_ref, k_ref, km_ref, o_ref, m_ref, l_ref, acc_ref, *, tq):
    i = pl.program_id(1)
    km = km_ref[...]
    colm = lax.broadcasted_iota(jnp.int32, (tq, km.shape[0]), 1)
    for h in range(MLA_HEADS):
        q = q_ref[h]
        s = lax.dot_general(q, km, (((1,), (1,)), ((), ())), preferred_element_type=F32)
        s = jnp.where(colm < N_META, s, NEG_INF)
        m = jnp.max(s, axis=-1, keepdims=True)
        p = jnp.exp(s - m)
        m_ref[h] = m
        l_ref[h] = jnp.sum(p, axis=-1, keepdims=True)
        acc_ref[h] = jnp.dot(p.astype(BF16), km[:, 0:KV_LORA], preferred_element_type=F32)

    def chunk(j, mask):
        kc = k_ref[pl.ds(pl.multiple_of(j * tq, tq), tq), :]
        for h in range(MLA_HEADS):
            m, l, acc = _softmax_step(q_ref[h], kc, mask, m_ref[h], l_ref[h], acc_ref[h])
            m_ref[h] = m
            l_ref[h] = l
            acc_ref[h] = acc

    def body(j, carry):
        chunk(j, None)
        return carry

    lax.fori_loop(0, i, body, 0)
    row = lax.broadcasted_iota(jnp.int32, (tq, tq), 0)
    col = lax.broadcasted_iota(jnp.int32, (tq, tq), 1)
    chunk(i, col <= row)
    for h in range(MLA_HEADS):
        o_ref[:, h * KV_LORA:(h + 1) * KV_LORA] = (acc_ref[h] / l_ref[h]).astype(BF16)


def _mla_prompt(qcat, kvcat, kmeta, nb, tq):
    n = kvcat.shape[0]
    t = n // nb
    nq = t // tq
    return pl.pallas_call(
        functools.partial(_mla_prompt_kernel, tq=tq),
        grid=(nb, nq),
        in_specs=[pl.BlockSpec((MLA_HEADS, tq, QCAT_W), lambda b, i: (0, b * nq + i, 0)),
                  pl.BlockSpec((t, QCAT_W), lambda b, i: (b, 0)),
                  _const_spec(kmeta.shape)],
        out_specs=pl.BlockSpec((tq, MLA_HEADS * KV_LORA), lambda b, i: (b * nq + i, 0)),
        out_shape=jax.ShapeDtypeStruct((n, MLA_HEADS * KV_LORA), BF16),
        scratch_shapes=[pltpu.VMEM((MLA_HEADS, tq, 1), F32), pltpu.VMEM((MLA_HEADS, tq, 1), F32),
                        pltpu.VMEM((MLA_HEADS, tq, KV_LORA), F32)],
        compiler_params=pltpu.CompilerParams(dimension_semantics=("arbitrary", "arbitrary"),
                                             vmem_limit_bytes=VMEM_LIMIT),
        name="mla_prompt",
    )(qcat, kvcat, kmeta)


def _mla_meta_kernel(q_ref, km_ref, o_ref):
    km = km_ref[...]
    r = q_ref.shape[1]
    row = lax.broadcasted_iota(jnp.int32, (r, km.shape[0]), 0)
    col = lax.broadcasted_iota(jnp.int32, (r, km.shape[0]), 1)
    mask = (col <= row) & (col < N_META)
    for h in range(MLA_HEADS):
        s = lax.dot_general(q_ref[h], km, (((1,), (1,)), ((), ())), preferred_element_type=F32)
        s = jnp.where(mask, s, NEG_INF)
        p = jnp.exp(s - jnp.max(s, axis=-1, keepdims=True))
        acc = jnp.dot(p.astype(BF16), km[:, 0:KV_LORA], preferred_element_type=F32)
        o_ref[:, h * KV_LORA:(h + 1) * KV_LORA] = (
            acc / jnp.sum(p, axis=-1, keepdims=True)).astype(BF16)


def _mla_meta(qcat, kmeta):
    r = qcat.shape[1]
    return pl.pallas_call(
        _mla_meta_kernel,
        out_shape=jax.ShapeDtypeStruct((r, MLA_HEADS * KV_LORA), BF16),
        compiler_params=pltpu.CompilerParams(vmem_limit_bytes=VMEM_LIMIT),
        name="mla_meta",
    )(qcat, kmeta)


def _mla_decode_kernel(pt_ref, q_ref, kself_ref, peself_ref, *rest, pages, page, ls):
    kv_refs = rest[:pages]
    pe_refs = rest[pages:2 * pages]
    o_ref, kbuf_ref, m_ref, l_ref, acc_ref = rest[2 * pages:]
    del pt_ref
    j = pl.program_id(1)
    nj = pl.num_programs(1)
    q = q_ref[0]
    rows = q.shape[0]

    @pl.when((pl.program_id(0) == 0) & (j == 0))
    def _():
        kbuf_ref[...] = jnp.zeros(kbuf_ref.shape, BF16)

    @pl.when(j == 0)
    def _():
        m_ref[...] = jnp.full(m_ref.shape, NEG_INF, F32)
        l_ref[...] = jnp.zeros(l_ref.shape, F32)
        acc_ref[...] = jnp.zeros(acc_ref.shape, F32)

    for p in range(pages):
        rsl = slice(p * page, (p + 1) * page)
        kbuf_ref[rsl, 0:KV_LORA] = kv_refs[p][0].astype(BF16)
        kbuf_ref[rsl, KV_LORA:KV_LORA + QK_ROPE] = pe_refs[p][0].astype(BF16)
    m, l, acc = _softmax_step(q, kbuf_ref[...], None, m_ref[...], l_ref[...], acc_ref[...])
    m_ref[...] = m
    l_ref[...] = l
    acc_ref[...] = acc

    @pl.when(j == nj - 1)
    def _():
        qf = q.astype(F32)
        ql = qf[:, 0:KV_LORA]
        qp = qf[:, KV_LORA:KV_LORA + QK_ROPE]
        kvs = kself_ref[0]
        pes = peself_ref[0]
        t_row = lax.broadcasted_iota(jnp.int32, (rows, 1), 0) // MLA_HEADS
        ss = []
        for t2 in range(ls):
            s_t = (jnp.sum(ql * kvs[t2:t2 + 1, :], axis=-1, keepdims=True)
                   + jnp.sum(qp * pes[t2:t2 + 1, :], axis=-1, keepdims=True))
            ss.append(jnp.where(t_row >= t2, s_t, NEG_INF))
        m0 = m_ref[...]
        m1 = m0
        for s_t in ss:
            m1 = jnp.maximum(m1, s_t)
        alpha = jnp.exp(m0 - m1)
        l1 = alpha * l_ref[...]
        acc1 = alpha * acc_ref[...]
        for t2 in range(ls):
            p_t = jnp.exp(ss[t2] - m1)
            l1 = l1 + p_t
            acc1 = acc1 + p_t * kvs[t2:t2 + 1, :]
        o_ref[0] = (acc1 / l1).astype(BF16)


def _mla_decode(page_table, q, kself, peself, cache_kv, cache_pe, pages):
    db, rows, _ = q.shape
    ls = kself.shape[1]
    n_pages = page_table.shape[1]
    page = cache_kv.shape[1]
    kv_specs = [pl.BlockSpec((1, page, KV_LORA),
                             functools.partial(lambda b, j, pt, p: (pt[b, j * pages + p], 0, 0), p=p))
                for p in range(pages)]
    pe_specs = [pl.BlockSpec((1, page, QK_ROPE),
                             functools.partial(lambda b, j, pt, p: (pt[b, j * pages + p], 0, 0), p=p))
                for p in range(pages)]
    grid_spec = pltpu.PrefetchScalarGridSpec(
        num_scalar_prefetch=1,
        grid=(db, n_pages // pages),
        in_specs=[pl.BlockSpec((1, rows, QCAT_W), lambda b, j, pt: (b, 0, 0)),
                  pl.BlockSpec((1, ls, KV_LORA), lambda b, j, pt: (b, 0, 0)),
                  pl.BlockSpec((1, ls, QK_ROPE), lambda b, j, pt: (b, 0, 0))] + kv_specs + pe_specs,
        out_specs=pl.BlockSpec((1, rows, KV_LORA), lambda b, j, pt: (b, 0, 0)),
        scratch_shapes=[pltpu.VMEM((pages * page, QCAT_W), BF16),
                        pltpu.VMEM((rows, 1), F32), pltpu.VMEM((rows, 1), F32),
                        pltpu.VMEM((rows, KV_LORA), F32)],
    )
    return pl.pallas_call(
        functools.partial(_mla_decode_kernel, pages=pages, page=page, ls=ls),
        grid_spec=grid_spec,
        out_shape=jax.ShapeDtypeStruct((db, rows, KV_LORA), BF16),
        compiler_params=pltpu.CompilerParams(dimension_semantics=("arbitrary", "arbitrary"),
                                             vmem_limit_bytes=VMEM_LIMIT),
        name="mla_decode",
    )(page_table, q, kself, peself, *([cache_kv] * pages), *([cache_pe] * pages))


def _mix_ffn_kernel(x_ref, ret_ref, olat_ref, cin_ref, wuv_ref, wout_ref, gffn_ref, wup_ref,
                    cw_ref, cb_ref, wdown_ref, gfin_ref, y_ref, cout_ref,
                    carry_ref, stage_ref, acc_ref, *, stride, carry_end, cw):
    t = pl.program_id(1)
    tm = x_ref.shape[0]
    hal = carry_ref.shape[0]
    d_ff = wdown_ref.shape[0]

    @pl.when(t == 0)
    def _():
        carry_ref[...] = cin_ref[...]

    mix = jnp.dot(ret_ref[...], wout_ref[0:RET_W, :], preferred_element_type=F32)
    for h in range(MLA_HEADS):
        mla_h = jnp.dot(olat_ref[:, h * KV_LORA:(h + 1) * KV_LORA], wuv_ref[h],
                        preferred_element_type=F32)
        mix = mix + jnp.dot(mla_h.astype(BF16),
                            wout_ref[RET_W + h * V_DIM:RET_W + (h + 1) * V_DIM, :],
                            preferred_element_type=F32)
    h1 = x_ref[...] + mix
    a2 = _rms(h1, gffn_ref[...]).astype(BF16)

    def conv_half(c0):
        u = jnp.dot(a2, wup_ref[:, c0:c0 + cw], preferred_element_type=F32)
        stage_ref[0:hal, :] = carry_ref[:, c0:c0 + cw]
        stage_ref[hal:hal + tm, :] = u
        carry_ref[:, c0:c0 + cw] = u[carry_end - hal:carry_end, :]
        um1 = stage_ref[hal - stride:hal - stride + tm, :]
        um2 = stage_ref[hal - 2 * stride:hal - 2 * stride + tm, :]
        return (cb_ref[:, c0:c0 + cw] + cw_ref[0:1, c0:c0 + cw] * um2
                + cw_ref[1:2, c0:c0 + cw] * um1 + cw_ref[2:3, c0:c0 + cw] * u)

    for jc in range(d_ff // cw):
        ca = conv_half(jc * cw)
        cg = conv_half(d_ff + jc * cw)
        hmid = (cg * jax.nn.sigmoid(cg) * ca).astype(BF16)
        part = jnp.dot(hmid, wdown_ref[jc * cw:(jc + 1) * cw, :], preferred_element_type=F32)
        if jc == 0:
            acc_ref[...] = part
        else:
            acc_ref[...] += part
    h2 = h1 + acc_ref[...]
    y_ref[...] = _rms(h2, gfin_ref[...])

    @pl.when(t == pl.num_programs(1) - 1)
    def _():
        cout_ref[0] = carry_ref[...]


def _mix_ffn(x, ret, olat, carry_in, wts, nseq, tm, stride, carry_end, cw):
    n, d = x.shape
    hal = carry_in.shape[0]
    nt = n // (nseq * tm)
    wuv, wout, gffn, wup, convw, convb, wdown, gfin = wts
    row = lambda w: pl.BlockSpec((tm, w), lambda b, t: (b * nt + t, 0))
    return pl.pallas_call(
        functools.partial(_mix_ffn_kernel, stride=stride, carry_end=carry_end, cw=cw),
        grid=(nseq, nt),
        in_specs=[row(d), row(RET_W), row(MLA_HEADS * KV_LORA), _const_spec(carry_in.shape),
                  _const_spec(wuv.shape), _const_spec(wout.shape), _const_spec(gffn.shape),
                  _const_spec(wup.shape), _const_spec(convw.shape), _const_spec(convb.shape),
                  _const_spec(wdown.shape), _const_spec(gfin.shape)],
        out_specs=(row(d), pl.BlockSpec((1, hal, wup.shape[1]), lambda b, t: (b, 0, 0))),
        out_shape=(jax.ShapeDtypeStruct((n, d), F32),
                   jax.ShapeDtypeStruct((nseq, hal, wup.shape[1]), F32)),
        scratch_shapes=[pltpu.VMEM((hal, wup.shape[1]), F32),
                        pltpu.VMEM((hal + tm, cw), F32),
                        pltpu.VMEM((tm, d), F32)],
        compiler_params=pltpu.CompilerParams(dimension_semantics=("arbitrary", "arbitrary"),
                                             vmem_limit_bytes=VMEM_LIMIT),
        name="mix_ffn",
    )(x, ret, olat, carry_in, wuv, wout, gffn, wup, convw, convb, wdown, gfin)


def _rope_tables(pos):
    pos = pos.astype(F32)[:, None]

    def cs(dim):
        inv = ROPE_THETA ** (-jnp.arange(0, dim, 2, dtype=F32) / dim)
        ang = pos * inv[None, :]
        return jnp.cos(ang), jnp.sin(ang)

    c, s = cs(RET_DK)
    c128 = jnp.concatenate([c, c], axis=-1)
    s128 = jnp.concatenate([-s, s], axis=-1)
    c, s = cs(QK_ROPE)
    z32 = jnp.zeros_like(s)
    z64 = jnp.zeros((pos.shape[0], LANES - QK_ROPE), F32)
    c64 = jnp.concatenate([c, c, z64], axis=-1)
    s64a = jnp.concatenate([-s, z32, z64], axis=-1)
    s64b = jnp.concatenate([z32, s, z64], axis=-1)
    return c128, s128, c64, s64a, s64b


def kernel(x_prompt, x_sample, cache_kv_latent, cache_k_rope, state_retention, state_ffn_conv,
           page_table, meta_tokens, g_mix, w_in, g_q, w_uq, g_kv, w_uk, w_uv, w_out,
           g_ffn, w_up, conv_w, conv_b, w_down, g_final):
    nb, seq, d = x_prompt.shape
    db, ls, _ = x_sample.shape
    depth = w_in.shape[0]
    assert depth == 1, "single-layer step"
    n_pages = page_table.shape[1]
    page = cache_kv_latent.shape[2]
    past_len = n_pages * page
    d_ff = w_down.shape[1]
    l = 0

    w = w_in[l]
    win = jnp.concatenate([w, jnp.zeros((d, LANES - QK_ROPE), w.dtype)], axis=1).astype(BF16)
    wq = w_uq[l].reshape(Q_LORA, MLA_HEADS, QK_NOPE + QK_ROPE)
    wuq = jnp.concatenate([wq, jnp.zeros((Q_LORA, MLA_HEADS, LANES - QK_ROPE), wq.dtype)],
                          axis=-1).reshape(Q_LORA, MLA_HEADS * 2 * LANES).astype(BF16)
    wuk = jnp.transpose(w_uk[l], (1, 2, 0)).astype(BF16)
    wuv = jnp.transpose(w_uv[l], (1, 0, 2)).astype(BF16)
    proj_w = (g_mix[l][None, :], win, g_q[l][None, :], wuq, g_kv[l][None, :], wuk)
    ffn_w = (wuv, w_out[l].astype(BF16), g_ffn[l][None, :], w_up[l].astype(BF16),
             conv_w[l], conv_b[l][None, :], w_down[l].astype(BF16), g_final[None, :])

    tile = 128
    xm = jnp.concatenate([meta_tokens.astype(F32), jnp.zeros((tile - N_META, d), F32)], axis=0)
    tabs_m = _rope_tables(jnp.arange(tile))
    qr, kr, vr, gr, qcat_m, kvcat_m, ckv_m, kpe_m = _project(xm, tabs_m, proj_w, tile)
    zero_state = jnp.zeros((1, RET_HEADS, RET_DK, RET_DV), F32)
    ret_m, state_m = _retention_chunks(qr, kr, vr, gr, zero_state, 1, tile, N_META, True)
    row_valid = (jnp.arange(tile) < N_META)[:, None]
    kmeta = jnp.where(row_valid, kvcat_m, jnp.zeros_like(kvcat_m))
    olat_m = _mla_meta(qcat_m, kmeta)
    _, carry_m = _mix_ffn(xm, ret_m, olat_m, jnp.zeros((8, 2 * d_ff), F32), ffn_w,
                          1, tile, 1, N_META, 256)

    tm = 256
    xp = x_prompt.reshape(nb * seq, d)
    tabs_p = _rope_tables(N_META + jnp.arange(seq))
    qr, kr, vr, gr, qcat, kvcat, ckv_p, kpe_p = _project(xp, tabs_p, proj_w, tm)
    ret_p, state_p = _retention_chunks(qr, kr, vr, gr, state_m, nb, 128, 128, True)
    olat_p = _mla_prompt(qcat, kvcat, kmeta, nb, 256)
    y_p, carry_p = _mix_ffn(xp, ret_p, olat_p, carry_m[0], ffn_w, nb, tm, 1, tm, 256)

    ns = db * ls
    xs = x_sample.reshape(ns, d)
    pos_s = past_len + jnp.arange(ls)
    tabs_s = tuple(jnp.tile(t, (db, 1)) for t in _rope_tables(pos_s))
    qr, kr, vr, gr, qcat_s, _, ckv_s, kpe_s = _project(xs, tabs_s, proj_w, min(ns, 256))
    ret_s, state_s = _retention_decode(qr, kr, vr, gr, state_retention[l], ls, min(ns, 64))
    q_s = jnp.transpose(qcat_s.reshape(MLA_HEADS, db, ls, QCAT_W), (1, 2, 0, 3)).reshape(
        db, ls * MLA_HEADS, QCAT_W)
    olat_s = _mla_decode(page_table, q_s, ckv_s.reshape(db, ls, KV_LORA),
                         kpe_s.reshape(db, ls, QK_ROPE), cache_kv_latent[l], cache_k_rope[l], 8)
    olat_s = olat_s.reshape(ns, MLA_HEADS * KV_LORA)
    tmaj = lambda a: jnp.transpose(a.reshape(db, ls, a.shape[-1]), (1, 0, 2)).reshape(ns, a.shape[-1])
    carry_s_in = jnp.transpose(state_ffn_conv[l], (1, 0, 2)).reshape((CONV_W - 1) * db, 2 * d_ff)
    y_s, carry_s = _mix_ffn(tmaj(xs), tmaj(ret_s), tmaj(olat_s), carry_s_in, ffn_w,
                            1, ns, db, ns, 256)
    y_s = jnp.transpose(y_s.reshape(ls, db, d), (1, 0, 2))
    conv_s = jnp.transpose(carry_s.reshape(CONV_W - 1, db, 2 * d_ff), (1, 0, 2))

    bc = lambda a: jnp.broadcast_to(a[:N_META][None], (nb, N_META, a.shape[-1]))
    kv_p = jnp.concatenate([bc(ckv_m), ckv_p.reshape(nb, seq, KV_LORA)], axis=1)
    pe_p = jnp.concatenate([bc(kpe_m), kpe_p.reshape(nb, seq, QK_ROPE)], axis=1)
    return (y_p.reshape(nb, seq, d), y_s,
            kv_p[None], pe_p[None],
            state_p[None],
            carry_p[:, 8 - (CONV_W - 1):, :][None],
            ckv_s.reshape(db, ls, KV_LORA)[None], kpe_s.reshape(db, ls, QK_ROPE)[None],
            state_s[None], conv_s[None])
```

```python
import functools
import math

import jax
import jax.numpy as jnp
from jax import lax
from jax.experimental import pallas as pl
from jax.experimental.pallas import tpu as pltpu

F32 = jnp.float32
BF16 = jnp.bfloat16

N_META = 16
RET_HEADS = 4
RET_DK = 128
RET_DV = 128
MLA_HEADS = 4
Q_LORA = 384
KV_LORA = 256
QK_NOPE = 128
QK_ROPE = 64
V_DIM = 128
CONV_W = 3
ROPE_THETA = 10000.0
EPS = 1e-6
SOFTMAX_SCALE = (QK_NOPE + QK_ROPE) ** -0.5
RET_W = RET_HEADS * RET_DV
LANES = 128
ROW_TILE = 16
QCAT_W = KV_LORA + LANES
LOG_GAMMA = tuple(math.log1p(-(2.0 ** (-5.0 - h))) for h in range(RET_HEADS))
VMEM_LIMIT = 56 * 1024 * 1024
NEG_INF = float("-inf")


def _const_spec(shape):
    nd = len(shape)
    return pl.BlockSpec(shape, lambda *_: (0,) * nd, pipeline_mode=pl.Buffered(1))


def _rms(x, g):
    return x * lax.rsqrt(jnp.mean(x * x, axis=-1, keepdims=True) + EPS) * g


def _proj_kernel(x_ref, c128_ref, s128_ref, c64_ref, s64a_ref, s64b_ref,
                 gmix_ref, win_ref, gq_ref, wuq_ref, gkv_ref, wuk_ref,
                 qr_ref, kr_ref, vr_ref, gr_ref, qcat_ref, kvcat_ref, ckv_ref, kpe_ref):
    a = _rms(x_ref[...], gmix_ref[...])
    z = jnp.dot(a.astype(BF16), win_ref[...], preferred_element_type=F32)
    c128 = c128_ref[...]
    s128 = s128_ref[...]
    c64 = c64_ref[...]
    s64a = s64a_ref[...]
    s64b = s64b_ref[...]

    def rope128(v):
        return v * c128 + pltpu.roll(v, 64, 1) * s128

    def rope64(v):
        return v * c64 + pltpu.roll(v, 96, 1) * s64a + pltpu.roll(v, 32, 1) * s64b

    for h in range(RET_HEADS):
        sl = slice(h * RET_DK, (h + 1) * RET_DK)
        qr_ref[:, sl] = rope128(z[:, sl]).astype(BF16)
        ksl = slice(RET_W + h * RET_DK, RET_W + (h + 1) * RET_DK)
        kr_ref[:, sl] = (rope128(z[:, ksl]) * (RET_DK ** -0.5)).astype(BF16)
    vr_ref[...] = z[:, 2 * RET_W:3 * RET_W].astype(BF16)
    gr_ref[...] = z[:, 3 * RET_W:4 * RET_W].astype(BF16)

    o = 4 * RET_W
    cqn = _rms(z[:, o:o + Q_LORA], gq_ref[...])
    q2 = jnp.dot(cqn.astype(BF16), wuq_ref[...], preferred_element_type=F32)
    for h in range(MLA_HEADS):
        b0 = h * 2 * LANES
        nope = q2[:, b0:b0 + QK_NOPE]
        pe = rope64(q2[:, b0 + LANES:b0 + 2 * LANES])
        qlat = jnp.dot(nope.astype(BF16), wuk_ref[h], preferred_element_type=F32)
        qcat_ref[h, :, 0:KV_LORA] = (qlat * SOFTMAX_SCALE).astype(BF16)
        qcat_ref[h, :, KV_LORA:QCAT_W] = (pe * SOFTMAX_SCALE).astype(BF16)

    o += Q_LORA
    ckvn = _rms(z[:, o:o + KV_LORA], gkv_ref[...])
    ckv_ref[...] = ckvn
    o += KV_LORA
    kp = rope64(z[:, o:o + LANES])
    kpe_ref[...] = kp[:, 0:QK_ROPE]
    kvcat_ref[:, 0:KV_LORA] = ckvn.astype(BF16)
    kvcat_ref[:, KV_LORA:QCAT_W] = kp.astype(BF16)


def _project(x, tabs, wts, tm):
    n, d = x.shape
    nt = tabs[0].shape[0] // tm
    gmix, win, gq, wuq, gkv, wuk = wts
    row = lambda w: pl.BlockSpec((tm, w), lambda i: (i, 0))
    tab = pl.BlockSpec((tm, LANES), lambda i: (i % nt, 0))
    out_shapes = (
        jax.ShapeDtypeStruct((n, RET_W), BF16), jax.ShapeDtypeStruct((n, RET_W), BF16),
        jax.ShapeDtypeStruct((n, RET_W), BF16), jax.ShapeDtypeStruct((n, RET_W), BF16),
        jax.ShapeDtypeStruct((MLA_HEADS, n, QCAT_W), BF16),
        jax.ShapeDtypeStruct((n, QCAT_W), BF16),
        jax.ShapeDtypeStruct((n, KV_LORA), F32),
        jax.ShapeDtypeStruct((n, QK_ROPE), F32),
    )
    out_specs = (row(RET_W), row(RET_W), row(RET_W), row(RET_W),
                 pl.BlockSpec((MLA_HEADS, tm, QCAT_W), lambda i: (0, i, 0)),
                 row(QCAT_W), row(KV_LORA), row(QK_ROPE))
    return pl.pallas_call(
        _proj_kernel,
        grid=(n // tm,),
        in_specs=[row(d), tab, tab, tab, tab, tab,
                  _const_spec(gmix.shape), _const_spec(win.shape), _const_spec(gq.shape),
                  _const_spec(wuq.shape), _const_spec(gkv.shape), _const_spec(wuk.shape)],
        out_specs=out_specs,
        out_shape=out_shapes,
        compiler_params=pltpu.CompilerParams(dimension_semantics=("arbitrary",),
                                             vmem_limit_bytes=VMEM_LIMIT),
        name="proj",
    )(x, *tabs, gmix, win, gq, wuq, gkv, wuk)


def _ret_chunk_kernel(q_ref, k_ref, v_ref, g_ref, s0_ref, o_ref, sf_ref, st_ref, *, lv):
    c = pl.program_id(1)
    L = q_ref.shape[0]

    @pl.when(c == 0)
    def _():
        st_ref[...] = s0_ref[0]

    li = lax.broadcasted_iota(jnp.int32, (L, L), 0)
    mi = lax.broadcasted_iota(jnp.int32, (L, L), 1)
    diff = (li - mi).astype(F32)
    n = lax.broadcasted_iota(jnp.int32, (L, 1), 0).astype(F32)
    for h in range(RET_HEADS):
        lg = LOG_GAMMA[h]
        sl = slice(h * RET_DK, (h + 1) * RET_DK)
        q = q_ref[:, sl]
        k = k_ref[:, sl]
        v = v_ref[:, sl]
        decay = jnp.where(diff >= 0, jnp.exp(lg * jnp.maximum(diff, 0.0)), 0.0)
        s = lax.dot_general(q, k, (((1,), (1,)), ((), ())), preferred_element_type=F32) * decay
        inner = jnp.dot(s.astype(BF16), v, preferred_element_type=F32)
        state = st_ref[h]
        qd = (q.astype(F32) * jnp.exp(lg * (n + 1.0))).astype(BF16)
        cross = jnp.dot(qd, state.astype(BF16), preferred_element_type=F32)
        o = inner + cross
        kdec = jnp.where(n < lv, jnp.exp(lg * jnp.maximum(lv - 1.0 - n, 0.0)), 0.0)
        kd = (k.astype(F32) * kdec).astype(BF16)
        st_ref[h] = math.exp(lg * lv) * state + lax.dot_general(
            kd, v, (((0,), (0,)), ((), ())), preferred_element_type=F32)
        of = o * lax.rsqrt(jnp.mean(o * o, axis=-1, keepdims=True) + EPS)
        g = g_ref[:, sl].astype(F32)
        o_ref[:, sl] = (of * (g * jax.nn.sigmoid(g))).astype(BF16)

    @pl.when(c == pl.num_programs(1) - 1)
    def _():
        sf_ref[0] = st_ref[...]


def _retention_chunks(qr, kr, vr, gr, s0, nb, L, lv, shared_s0):
    n = qr.shape[0]
    nc = n // (nb * L)
    row = pl.BlockSpec((L, RET_W), lambda b, c: (b * nc + c, 0))
    st_block = (1, RET_HEADS, RET_DK, RET_DV)
    s0_map = (lambda b, c: (0, 0, 0, 0)) if shared_s0 else (lambda b, c: (b, 0, 0, 0))
    return pl.pallas_call(
        functools.partial(_ret_chunk_kernel, lv=float(lv)),
        grid=(nb, nc),
        in_specs=[row, row, row, row, pl.BlockSpec(st_block, s0_map)],
        out_specs=(row, pl.BlockSpec(st_block, lambda b, c: (b, 0, 0, 0))),
        out_shape=(jax.ShapeDtypeStruct((n, RET_W), BF16),
                   jax.ShapeDtypeStruct((nb, RET_HEADS, RET_DK, RET_DV), F32)),
        scratch_shapes=[pltpu.VMEM((RET_HEADS, RET_DK, RET_DV), F32)],
        compiler_params=pltpu.CompilerParams(dimension_semantics=("arbitrary", "arbitrary"),
                                             vmem_limit_bytes=VMEM_LIMIT),
        name="ret_chunks",
    )(qr, kr, vr, gr, s0)


def _ret_decode_kernel(q_ref, k_ref, v_ref, g_ref, s_ref, o_ref, sn_ref, *, ls):
    R = q_ref.shape[0]
    per_tile = ROW_TILE // ls
    li = lax.broadcasted_iota(jnp.int32, (R, R), 0)
    mi = lax.broadcasted_iota(jnp.int32, (R, R), 1)
    same = (li // ls) == (mi // ls)
    diff = ((li % ls) - (mi % ls)).astype(F32)
    t_col = (lax.broadcasted_iota(jnp.int32, (R, 1), 0) % ls).astype(F32)
    seq_in_tile = lax.broadcasted_iota(jnp.int32, (ROW_TILE, 1), 0) // ls
    for h in range(RET_HEADS):
        lg = LOG_GAMMA[h]
        sl = slice(h * RET_DK, (h + 1) * RET_DK)
        q = q_ref[:, sl]
        k = k_ref[:, sl]
        v = v_ref[:, sl]
        decay = jnp.where(same & (diff >= 0), jnp.exp(lg * jnp.maximum(diff, 0.0)), 0.0)
        s = lax.dot_general(q, k, (((1,), (1,)), ((), ())), preferred_element_type=F32) * decay
        inner = jnp.dot(s.astype(BF16), v, preferred_element_type=F32)
        qd = q.astype(F32) * jnp.exp(lg * (t_col + 1.0))
        kd = k.astype(F32) * jnp.exp(lg * (ls - 1.0 - t_col))
        sdec = math.exp(lg * ls)
        cross_tiles = []
        for tt in range(R // ROW_TILE):
            rows = slice(tt * ROW_TILE, (tt + 1) * ROW_TILE)
            qd_t = qd[rows].astype(BF16)
            kd_t = kd[rows]
            v_t = v[rows]
            acc = jnp.zeros((ROW_TILE, RET_DV), F32)
            for j in range(per_tile):
                b = tt * per_tile + j
                state = s_ref[b, h]
                mine = seq_in_tile == j
                cr = jnp.dot(qd_t, state.astype(BF16), preferred_element_type=F32)
                acc = jnp.where(mine, cr, acc)
                kdm = jnp.where(mine, kd_t, 0.0).astype(BF16)
                sn_ref[b, h] = sdec * state + lax.dot_general(
                    kdm, v_t, (((0,), (0,)), ((), ())), preferred_element_type=F32)
            cross_tiles.append(acc)
        o = inner + jnp.concatenate(cross_tiles, axis=0)
        of = o * lax.rsqrt(jnp.mean(o * o, axis=-1, keepdims=True) + EPS)
        g = g_ref[:, sl].astype(F32)
        o_ref[:, sl] = (of * (g * jax.nn.sigmoid(g))).astype(BF16)


def _retention_decode(qr, kr, vr, gr, state, ls, rows):
    n = qr.shape[0]
    nseq = rows // ls
    row = pl.BlockSpec((rows, RET_W), lambda i: (i, 0))
    st = pl.BlockSpec((nseq, RET_HEADS, RET_DK, RET_DV), lambda i: (i, 0, 0, 0))
    return pl.pallas_call(
        functools.partial(_ret_decode_kernel, ls=ls),
        grid=(n // rows,),
        in_specs=[row, row, row, row, st],
        out_specs=(row, st),
        out_shape=(jax.ShapeDtypeStruct((n, RET_W), BF16),
                   jax.ShapeDtypeStruct(state.shape, F32)),
        compiler_params=pltpu.CompilerParams(dimension_semantics=("arbitrary",),
                                             vmem_limit_bytes=VMEM_LIMIT),
        name="ret_decode",
    )(qr, kr, vr, gr, state)


def _mla_prompt_kernel(q_ref, k_ref, km_ref, o_ref, vt_ref, vmt_ref, m_ref, l_ref, acc_ref, *, tq, tk):
    i = pl.program_id(1)
    nkm = km_ref.shape[0]
    nq = MLA_HEADS * tq

    @pl.when(i == 0)
    def _():
        vmt_ref[...] = km_ref[:, 0:KV_LORA].T
        for c in range(k_ref.shape[0] // tq):
            vt_ref[:, c * tq:(c + 1) * tq] = k_ref[c * tq:(c + 1) * tq, 0:KV_LORA].T

    q_all = q_ref[...].reshape(nq, QCAT_W)

    def scores(kc):
        return lax.dot_general(kc, q_all, (((1,), (1,)), ((), ())), preferred_element_type=F32)

    s = scores(km_ref[...])
    s = jnp.where(lax.broadcasted_iota(jnp.int32, (nkm, nq), 0) < N_META, s, NEG_INF)
    m = jnp.max(s, axis=0, keepdims=True)
    p = jnp.exp(s - m)
    m_ref[...] = m
    l_ref[...] = jnp.sum(p, axis=0, keepdims=True)
    acc_ref[...] = jnp.dot(vmt_ref[...], p.astype(BF16), preferred_element_type=F32)

    def update(start, size, mask):
        s = scores(k_ref[pl.ds(start, size), :])
        if mask is not None:
            s = jnp.where(mask, s, NEG_INF)
        m_prev = m_ref[...]
        m_new = jnp.maximum(m_prev, jnp.max(s, axis=0, keepdims=True))
        alpha = jnp.exp(m_prev - m_new)
        p = jnp.exp(s - m_new)
        m_ref[...] = m_new
        l_ref[...] = alpha * l_ref[...] + jnp.sum(p, axis=0, keepdims=True)
        acc_ref[...] = alpha * acc_ref[...] + jnp.dot(
            vt_ref[:, pl.ds(start, size)], p.astype(BF16), preferred_element_type=F32)

    n_vis = i * tq

    def body(j, carry):
        update(pl.multiple_of(j * tk, tk), tk, None)
        return carry

    lax.fori_loop(0, n_vis // tk, body, 0)
    if tk > tq:
        @pl.when(n_vis % tk != 0)
        def _():
            update(pl.multiple_of(n_vis - tq, tq), tq, None)

    key = lax.broadcasted_iota(jnp.int32, (tq, nq), 0)
    qry = lax.broadcasted_iota(jnp.int32, (tq, nq), 1) % tq
    update(pl.multiple_of(n_vis, tq), tq, key <= qry)
    out = acc_ref[...] / l_ref[...]
    for h in range(MLA_HEADS):
        o_ref[:, h * KV_LORA:(h + 1) * KV_LORA] = out[:, h * tq:(h + 1) * tq].T.astype(BF16)


def _mla_prompt(qcat, kvcat, kmeta, nb, tq, tk):
    n = kvcat.shape[0]
    t = n // nb
    nq = t // tq
    return pl.pallas_call(
        functools.partial(_mla_prompt_kernel, tq=tq, tk=tk),
        grid=(nb, nq),
        in_specs=[pl.BlockSpec((MLA_HEADS, tq, QCAT_W), lambda b, i: (0, b * nq + i, 0)),
                  pl.BlockSpec((t, QCAT_W), lambda b, i: (b, 0)),
                  _const_spec(kmeta.shape)],
        out_specs=pl.BlockSpec((tq, MLA_HEADS * KV_LORA), lambda b, i: (b * nq + i, 0)),
        out_shape=jax.ShapeDtypeStruct((n, MLA_HEADS * KV_LORA), BF16),
        scratch_shapes=[pltpu.VMEM((KV_LORA, t), BF16), pltpu.VMEM((KV_LORA, kmeta.shape[0]), BF16),
                        pltpu.VMEM((1, MLA_HEADS * tq), F32), pltpu.VMEM((1, MLA_HEADS * tq), F32),
                        pltpu.VMEM((KV_LORA, MLA_HEADS * tq), F32)],
        compiler_params=pltpu.CompilerParams(dimension_semantics=("arbitrary", "arbitrary"),
                                             vmem_limit_bytes=VMEM_LIMIT),
        name="mla_prompt",
    )(qcat, kvcat, kmeta)


def _mla_meta_kernel(q_ref, km_ref, o_ref):
    km = km_ref[...]
    r = q_ref.shape[1]
    row = lax.broadcasted_iota(jnp.int32, (r, km.shape[0]), 0)
    col = lax.broadcasted_iota(jnp.int32, (r, km.shape[0]), 1)
    mask = (col <= row) & (col < N_META)
    for h in range(MLA_HEADS):
        s = lax.dot_general(q_ref[h], km, (((1,), (1,)), ((), ())), preferred_element_type=F32)
        s = jnp.where(mask, s, NEG_INF)
        p = jnp.exp(s - jnp.max(s, axis=-1, keepdims=True))
        acc = jnp.dot(p.astype(BF16), km[:, 0:KV_LORA], preferred_element_type=F32)
        o_ref[:, h * KV_LORA:(h + 1) * KV_LORA] = (
            acc / jnp.sum(p, axis=-1, keepdims=True)).astype(BF16)


def _mla_meta(qcat, kmeta):
    r = qcat.shape[1]
    return pl.pallas_call(
        _mla_meta_kernel,
        out_shape=jax.ShapeDtypeStruct((r, MLA_HEADS * KV_LORA), BF16),
        compiler_params=pltpu.CompilerParams(vmem_limit_bytes=VMEM_LIMIT),
        name="mla_meta",
    )(qcat, kmeta)


def _mla_decode_kernel(pt_ref, q_ref, kself_ref, peself_ref, kv_hbm, pe_hbm, o_ref,
                       kvbuf, pebuf, kb16, pb16, s_ref, sem, *, n_pages, page, ls, chunk):
    b = pl.program_id(0)
    nb = pl.num_programs(0)
    slot = b % 2
    n_keys = n_pages * page
    rope_rows = pb16.shape[0]

    def page_copies(seq, sl):
        out = []
        for j in range(n_pages):
            pid = pt_ref[seq, j]
            out.append(pltpu.make_async_copy(kv_hbm.at[pid], kvbuf.at[sl, pl.ds(j * page, page)],
                                             sem.at[0, sl]))
            out.append(pltpu.make_async_copy(pe_hbm.at[pid], pebuf.at[sl, j], sem.at[1, sl]))
        return out

    @pl.when(b == 0)
    def _():
        pb16[QK_ROPE:rope_rows, :] = jnp.zeros((rope_rows - QK_ROPE, n_keys), BF16)
        for c in page_copies(0, 0):
            c.start()

    @pl.when(b + 1 < nb)
    def _():
        for c in page_copies(b + 1, 1 - slot):
            c.start()

    for c in page_copies(b, slot):
        c.wait()

    q = q_ref[0]
    rows = q.shape[0]
    ql = q[:, 0:KV_LORA]
    qp = q[:, KV_LORA:QCAT_W]
    ppc = chunk // page
    for c in range(n_keys // chunk):
        ksl = slice(c * chunk, (c + 1) * chunk)
        kb16[ksl, :] = kvbuf[slot, ksl, :].astype(BF16)
        for j in range(ppc):
            jj = c * ppc + j
            pb16[0:QK_ROPE, jj * page:(jj + 1) * page] = pebuf[slot, jj].astype(BF16)
        s_ref[:, ksl] = (
            lax.dot_general(ql, kb16[ksl, :], (((1,), (1,)), ((), ())), preferred_element_type=F32)
            + jnp.dot(qp, pb16[:, ksl], preferred_element_type=F32))

    qf = q.astype(F32)
    kvs = kself_ref[0]
    pes = peself_ref[0]
    t_row = lax.broadcasted_iota(jnp.int32, (rows, 1), 0) // MLA_HEADS
    ss = []
    for t2 in range(ls):
        s_t = (jnp.sum(qf[:, 0:KV_LORA] * kvs[t2:t2 + 1, :], axis=-1, keepdims=True)
               + jnp.sum(qf[:, KV_LORA:KV_LORA + QK_ROPE] * pes[t2:t2 + 1, :], axis=-1, keepdims=True))
        ss.append(jnp.where(t_row >= t2, s_t, NEG_INF))
    m = jnp.max(s_ref[...], axis=-1, keepdims=True)
    for s_t in ss:
        m = jnp.maximum(m, s_t)
    l = jnp.zeros((rows, 1), F32)
    acc = jnp.zeros((rows, KV_LORA), F32)
    for t2 in range(ls):
        p_t = jnp.exp(ss[t2] - m)
        l = l + p_t
        acc = acc + p_t * kvs[t2:t2 + 1, :]
    for c in range(n_keys // chunk):
        ksl = slice(c * chunk, (c + 1) * chunk)
        p = jnp.exp(s_ref[:, ksl] - m)
        l = l + jnp.sum(p, axis=-1, keepdims=True)
        acc = acc + jnp.dot(p.astype(BF16), kb16[ksl, :], preferred_element_type=F32)
    o_ref[0] = (acc / l).astype(BF16)


def _mla_decode(page_table, q, kself, peself, cache_kv, cache_pe_t, chunk):
    db, rows, _ = q.shape
    ls = kself.shape[1]
    n_pages = page_table.shape[1]
    page = cache_kv.shape[1]
    n_keys = n_pages * page
    grid_spec = pltpu.PrefetchScalarGridSpec(
        num_scalar_prefetch=1,
        grid=(db,),
        in_specs=[pl.BlockSpec((1, rows, QCAT_W), lambda b, pt: (b, 0, 0)),
                  pl.BlockSpec((1, ls, KV_LORA), lambda b, pt: (b, 0, 0)),
                  pl.BlockSpec((1, ls, QK_ROPE), lambda b, pt: (b, 0, 0)),
                  pl.BlockSpec(memory_space=pl.ANY),
                  pl.BlockSpec(memory_space=pl.ANY)],
        out_specs=pl.BlockSpec((1, rows, KV_LORA), lambda b, pt: (b, 0, 0)),
        scratch_shapes=[pltpu.VMEM((2, n_keys, KV_LORA), F32),
                        pltpu.VMEM((2, n_pages, QK_ROPE, page), F32),
                        pltpu.VMEM((n_keys, KV_LORA), BF16),
                        pltpu.VMEM((QCAT_W - KV_LORA, n_keys), BF16),
                        pltpu.VMEM((rows, n_keys), F32),
                        pltpu.SemaphoreType.DMA((2, 2))],
    )
    return pl.pallas_call(
        functools.partial(_mla_decode_kernel, n_pages=n_pages, page=page, ls=ls, chunk=chunk),
        grid_spec=grid_spec,
        out_shape=jax.ShapeDtypeStruct((db, rows, KV_LORA), BF16),
        compiler_params=pltpu.CompilerParams(dimension_semantics=("arbitrary",),
                                             vmem_limit_bytes=VMEM_LIMIT),
        name="mla_decode",
    )(page_table, q, kself, peself, cache_kv, cache_pe_t)


def _mix_ffn_kernel(x_ref, ret_ref, olat_ref, cin_ref, wuv_ref, wout_ref, gffn_ref, wup_ref,
                    cw_ref, cb_ref, wdown_ref, gfin_ref, y_ref, cout_ref,
                    carry_ref, stage_ref, acc_ref, *, stride, carry_end, cw):
    t = pl.program_id(1)
    tm = x_ref.shape[0]
    hal = carry_ref.shape[0]
    d_ff = wdown_ref.shape[0]

    @pl.when(t == 0)
    def _():
        carry_ref[...] = cin_ref[...]

    mla = jnp.concatenate(
        [jnp.dot(olat_ref[:, h * KV_LORA:(h + 1) * KV_LORA], wuv_ref[h],
                 preferred_element_type=F32).astype(BF16) for h in range(MLA_HEADS)], axis=1)
    mixed = jnp.concatenate([ret_ref[...], mla], axis=1)
    h1 = x_ref[...] + jnp.dot(mixed, wout_ref[...], preferred_element_type=F32)
    a2 = _rms(h1, gffn_ref[...]).astype(BF16)

    def conv_half(c0):
        u = jnp.dot(a2, wup_ref[:, c0:c0 + cw], preferred_element_type=F32)
        stage_ref[0:hal, :] = carry_ref[:, c0:c0 + cw]
        stage_ref[hal:hal + tm, :] = u
        carry_ref[:, c0:c0 + cw] = u[carry_end - hal:carry_end, :]
        um1 = stage_ref[hal - stride:hal - stride + tm, :]
        um2 = stage_ref[hal - 2 * stride:hal - 2 * stride + tm, :]
        return (cb_ref[:, c0:c0 + cw] + cw_ref[0:1, c0:c0 + cw] * um2
                + cw_ref[1:2, c0:c0 + cw] * um1 + cw_ref[2:3, c0:c0 + cw] * u)

    for jc in range(d_ff // cw):
        ca = conv_half(jc * cw)
        cg = conv_half(d_ff + jc * cw)
        hmid = (cg * jax.nn.sigmoid(cg) * ca).astype(BF16)
        part = jnp.dot(hmid, wdown_ref[jc * cw:(jc + 1) * cw, :], preferred_element_type=F32)
        if jc == 0:
            acc_ref[...] = part
        else:
            acc_ref[...] += part
    h2 = h1 + acc_ref[...]
    y_ref[...] = _rms(h2, gfin_ref[...])

    @pl.when(t == pl.num_programs(1) - 1)
    def _():
        cout_ref[0] = carry_ref[...]


def _mix_ffn(x, ret, olat, carry_in, wts, nseq, tm, stride, carry_end, cw):
    n, d = x.shape
    hal = carry_in.shape[0]
    nt = n // (nseq * tm)
    wuv, wout, gffn, wup, convw, convb, wdown, gfin = wts
    row = lambda w: pl.BlockSpec((tm, w), lambda b, t: (b * nt + t, 0))
    return pl.pallas_call(
        functools.partial(_mix_ffn_kernel, stride=stride, carry_end=carry_end, cw=cw),
        grid=(nseq, nt),
        in_specs=[row(d), row(RET_W), row(MLA_HEADS * KV_LORA), _const_spec(carry_in.shape),
                  _const_spec(wuv.shape), _const_spec(wout.shape), _const_spec(gffn.shape),
                  _const_spec(wup.shape), _const_spec(convw.shape), _const_spec(convb.shape),
                  _const_spec(wdown.shape), _const_spec(gfin.shape)],
        out_specs=(row(d), pl.BlockSpec((1, hal, wup.shape[1]), lambda b, t: (b, 0, 0))),
        out_shape=(jax.ShapeDtypeStruct((n, d), F32),
                   jax.ShapeDtypeStruct((nseq, hal, wup.shape[1]), F32)),
        scratch_shapes=[pltpu.VMEM((hal, wup.shape[1]), F32),
                        pltpu.VMEM((hal + tm, cw), F32),
                        pltpu.VMEM((tm, d), F32)],
        compiler_params=pltpu.CompilerParams(dimension_semantics=("arbitrary", "arbitrary"),
                                             vmem_limit_bytes=VMEM_LIMIT),
        name="mix_ffn",
    )(x, ret, olat, carry_in, wuv, wout, gffn, wup, convw, convb, wdown, gfin)


def _rope_tables(pos):
    pos = pos.astype(F32)[:, None]

    def cs(dim):
        inv = ROPE_THETA ** (-jnp.arange(0, dim, 2, dtype=F32) / dim)
        ang = pos * inv[None, :]
        return jnp.cos(ang), jnp.sin(ang)

    c, s = cs(RET_DK)
    c128 = jnp.concatenate([c, c], axis=-1)
    s128 = jnp.concatenate([-s, s], axis=-1)
    c, s = cs(QK_ROPE)
    z32 = jnp.zeros_like(s)
    z64 = jnp.zeros((pos.shape[0], LANES - QK_ROPE), F32)
    c64 = jnp.concatenate([c, c, z64], axis=-1)
    s64a = jnp.concatenate([-s, z32, z64], axis=-1)
    s64b = jnp.concatenate([z32, s, z64], axis=-1)
    return c128, s128, c64, s64a, s64b


def kernel(x_prompt, x_sample, cache_kv_latent, cache_k_rope, state_retention, state_ffn_conv,
           page_table, meta_tokens, g_mix, w_in, g_q, w_uq, g_kv, w_uk, w_uv, w_out,
           g_ffn, w_up, conv_w, conv_b, w_down, g_final):
    nb, seq, d = x_prompt.shape
    db, ls, _ = x_sample.shape
    depth = w_in.shape[0]
    assert depth == 1, "single-layer step"
    n_pages = page_table.shape[1]
    page = cache_kv_latent.shape[2]
    past_len = n_pages * page
    d_ff = w_down.shape[1]
    l = 0

    w = w_in[l]
    win = jnp.concatenate([w, jnp.zeros((d, LANES - QK_ROPE), w.dtype)], axis=1).astype(BF16)
    wq = w_uq[l].reshape(Q_LORA, MLA_HEADS, QK_NOPE + QK_ROPE)
    wuq = jnp.concatenate([wq, jnp.zeros((Q_LORA, MLA_HEADS, LANES - QK_ROPE), wq.dtype)],
                          axis=-1).reshape(Q_LORA, MLA_HEADS * 2 * LANES).astype(BF16)
    wuk = jnp.transpose(w_uk[l], (1, 2, 0)).astype(BF16)
    wuv = jnp.transpose(w_uv[l], (1, 0, 2)).astype(BF16)
    proj_w = (g_mix[l][None, :], win, g_q[l][None, :], wuq, g_kv[l][None, :], wuk)
    ffn_w = (wuv, w_out[l].astype(BF16), g_ffn[l][None, :], w_up[l].astype(BF16),
             conv_w[l], conv_b[l][None, :], w_down[l].astype(BF16), g_final[None, :])

    tile = 128
    xm = jnp.concatenate([meta_tokens.astype(F32), jnp.zeros((tile - N_META, d), F32)], axis=0)
    tabs_m = _rope_tables(jnp.arange(tile))
    qr, kr, vr, gr, qcat_m, kvcat_m, ckv_m, kpe_m = _project(xm, tabs_m, proj_w, tile)
    zero_state = jnp.zeros((1, RET_HEADS, RET_DK, RET_DV), F32)
    ret_m, state_m = _retention_chunks(qr, kr, vr, gr, zero_state, 1, tile, N_META, True)
    row_valid = (jnp.arange(tile) < N_META)[:, None]
    kmeta = jnp.where(row_valid, kvcat_m, jnp.zeros_like(kvcat_m))
    olat_m = _mla_meta(qcat_m, kmeta)
    _, carry_m = _mix_ffn(xm, ret_m, olat_m, jnp.zeros((8, 2 * d_ff), F32), ffn_w,
                          1, tile, 1, N_META, 256)

    tm = 256
    xp = x_prompt.reshape(nb * seq, d)
    tabs_p = _rope_tables(N_META + jnp.arange(seq))
    qr, kr, vr, gr, qcat, kvcat, ckv_p, kpe_p = _project(xp, tabs_p, proj_w, tm)
    ret_p, state_p = _retention_chunks(qr, kr, vr, gr, state_m, nb, 128, 128, True)
    olat_p = _mla_prompt(qcat, kvcat, kmeta, nb, 256, 512)
    y_p, carry_p = _mix_ffn(xp, ret_p, olat_p, carry_m[0], ffn_w, nb, 512, 1, 512, 256)

    ns = db * ls
    xs = x_sample.reshape(ns, d)
    pos_s = past_len + jnp.arange(ls)
    tabs_s = tuple(jnp.tile(t, (db, 1)) for t in _rope_tables(pos_s))
    qr, kr, vr, gr, qcat_s, _, ckv_s, kpe_s = _project(xs, tabs_s, proj_w, min(ns, 256))
    ret_s, state_s = _retention_decode(qr, kr, vr, gr, state_retention[l], ls, min(ns, 64))
    q_s = jnp.transpose(qcat_s.reshape(MLA_HEADS, db, ls, QCAT_W), (1, 2, 0, 3)).reshape(
        db, ls * MLA_HEADS, QCAT_W)
    cache_pe_t = jnp.swapaxes(cache_k_rope[l], 1, 2)
    olat_s = _mla_decode(page_table, q_s, ckv_s.reshape(db, ls, KV_LORA),
                         kpe_s.reshape(db, ls, QK_ROPE), cache_kv_latent[l], cache_pe_t,
                         min(1024, past_len))
    olat_s = olat_s.reshape(ns, MLA_HEADS * KV_LORA)
    tmaj = lambda a: jnp.transpose(a.reshape(db, ls, a.shape[-1]), (1, 0, 2)).reshape(ns, a.shape[-1])
    carry_s_in = jnp.transpose(state_ffn_conv[l], (1, 0, 2)).reshape((CONV_W - 1) * db, 2 * d_ff)
    y_s, carry_s = _mix_ffn(tmaj(xs), tmaj(ret_s), tmaj(olat_s), carry_s_in, ffn_w,
                            1, ns, db, ns, 256)
    y_s = jnp.transpose(y_s.reshape(ls, db, d), (1, 0, 2))
    conv_s = jnp.transpose(carry_s.reshape(CONV_W - 1, db, 2 * d_ff), (1, 0, 2))

    bc = lambda a: jnp.broadcast_to(a[:N_META][None], (nb, N_META, a.shape[-1]))
    kv_p = jnp.concatenate([bc(ckv_m), ckv_p.reshape(nb, seq, KV_LORA)], axis=1)
    pe_p = jnp.concatenate([bc(kpe_m), kpe_p.reshape(nb, seq, QK_ROPE)], axis=1)
    return (y_p.reshape(nb, seq, d), y_s,
            kv_p[None], pe_p[None],
            state_p[None],
            carry_p[:, 8 - (CONV_W - 1):, :][None],
            ckv_s.reshape(db, ls, KV_LORA)[None], kpe_s.reshape(db, ls, QK_ROPE)[None],
            state_s[None], conv_s[None])
```

```python
import functools
import math

import jax
import jax.numpy as jnp
from jax import lax
from jax.experimental import pallas as pl
from jax.experimental.pallas import tpu as pltpu

F32 = jnp.float32
BF16 = jnp.bfloat16

N_META = 16
RET_HEADS = 4
RET_DK = 128
RET_DV = 128
MLA_HEADS = 4
Q_LORA = 384
KV_LORA = 256
QK_NOPE = 128
QK_ROPE = 64
V_DIM = 128
CONV_W = 3
ROPE_THETA = 10000.0
EPS = 1e-6
SOFTMAX_SCALE = (QK_NOPE + QK_ROPE) ** -0.5
RET_W = RET_HEADS * RET_DV
LANES = 128
ROW_TILE = 16
QCAT_W = KV_LORA + LANES
LOG_GAMMA = tuple(math.log1p(-(2.0 ** (-5.0 - h))) for h in range(RET_HEADS))
VMEM_LIMIT = 56 * 1024 * 1024
NEG_INF = float("-inf")


def _const_spec(shape):
    nd = len(shape)
    return pl.BlockSpec(shape, lambda *_: (0,) * nd, pipeline_mode=pl.Buffered(1))


def _rms(x, g):
    return x * lax.rsqrt(jnp.mean(x * x, axis=-1, keepdims=True) + EPS) * g


def _proj_kernel(x_ref, c128_ref, s128_ref, c64_ref, s64a_ref, s64b_ref,
                 gmix_ref, win_ref, gq_ref, wuq_ref, gkv_ref, wuk_ref,
                 qr_ref, kr_ref, vr_ref, gr_ref, qcat_ref, kvcat_ref, ckv_ref, kpe_ref):
    a = _rms(x_ref[...], gmix_ref[...])
    z = jnp.dot(a.astype(BF16), win_ref[...], preferred_element_type=F32)
    c128 = c128_ref[...]
    s128 = s128_ref[...]
    c64 = c64_ref[...]
    s64a = s64a_ref[...]
    s64b = s64b_ref[...]

    def rope128(v):
        return v * c128 + pltpu.roll(v, 64, 1) * s128

    def rope64(v):
        return v * c64 + pltpu.roll(v, 96, 1) * s64a + pltpu.roll(v, 32, 1) * s64b

    for h in range(RET_HEADS):
        sl = slice(h * RET_DK, (h + 1) * RET_DK)
        qr_ref[:, sl] = rope128(z[:, sl]).astype(BF16)
        ksl = slice(RET_W + h * RET_DK, RET_W + (h + 1) * RET_DK)
        kr_ref[:, sl] = (rope128(z[:, ksl]) * (RET_DK ** -0.5)).astype(BF16)
    vr_ref[...] = z[:, 2 * RET_W:3 * RET_W].astype(BF16)
    gr_ref[...] = z[:, 3 * RET_W:4 * RET_W].astype(BF16)

    o = 4 * RET_W
    cqn = _rms(z[:, o:o + Q_LORA], gq_ref[...])
    q2 = jnp.dot(cqn.astype(BF16), wuq_ref[...], preferred_element_type=F32)
    for h in range(MLA_HEADS):
        b0 = h * 2 * LANES
        nope = q2[:, b0:b0 + QK_NOPE]
        pe = rope64(q2[:, b0 + LANES:b0 + 2 * LANES])
        qlat = jnp.dot(nope.astype(BF16), wuk_ref[h], preferred_element_type=F32)
        qcat_ref[h, :, 0:KV_LORA] = (qlat * SOFTMAX_SCALE).astype(BF16)
        qcat_ref[h, :, KV_LORA:QCAT_W] = (pe * SOFTMAX_SCALE).astype(BF16)

    o += Q_LORA
    ckvn = _rms(z[:, o:o + KV_LORA], gkv_ref[...])
    ckv_ref[...] = ckvn
    o += KV_LORA
    kp = rope64(z[:, o:o + LANES])
    kpe_ref[...] = kp[:, 0:QK_ROPE]
    kvcat_ref[:, 0:KV_LORA] = ckvn.astype(BF16)
    kvcat_ref[:, KV_LORA:QCAT_W] = kp.astype(BF16)


def _project(x, tabs, wts, tm):
    n, d = x.shape
    nt = tabs[0].shape[0] // tm
    gmix, win, gq, wuq, gkv, wuk = wts
    row = lambda w: pl.BlockSpec((tm, w), lambda i: (i, 0))
    tab = pl.BlockSpec((tm, LANES), lambda i: (i % nt, 0))
    out_shapes = (
        jax.ShapeDtypeStruct((n, RET_W), BF16), jax.ShapeDtypeStruct((n, RET_W), BF16),
        jax.ShapeDtypeStruct((n, RET_W), BF16), jax.ShapeDtypeStruct((n, RET_W), BF16),
        jax.ShapeDtypeStruct((MLA_HEADS, n, QCAT_W), BF16),
        jax.ShapeDtypeStruct((n, QCAT_W), BF16),
        jax.ShapeDtypeStruct((n, KV_LORA), F32),
        jax.ShapeDtypeStruct((n, QK_ROPE), F32),
    )
    out_specs = (row(RET_W), row(RET_W), row(RET_W), row(RET_W),
                 pl.BlockSpec((MLA_HEADS, tm, QCAT_W), lambda i: (0, i, 0)),
                 row(QCAT_W), row(KV_LORA), row(QK_ROPE))
    return pl.pallas_call(
        _proj_kernel,
        grid=(n // tm,),
        in_specs=[row(d), tab, tab, tab, tab, tab,
                  _const_spec(gmix.shape), _const_spec(win.shape), _const_spec(gq.shape),
                  _const_spec(wuq.shape), _const_spec(gkv.shape), _const_spec(wuk.shape)],
        out_specs=out_specs,
        out_shape=out_shapes,
        compiler_params=pltpu.CompilerParams(dimension_semantics=("arbitrary",),
                                             vmem_limit_bytes=VMEM_LIMIT),
        name="proj",
    )(x, *tabs, gmix, win, gq, wuq, gkv, wuk)


def _ret_chunk_kernel(q_ref, k_ref, v_ref, g_ref, s0_ref, o_ref, sf_ref, st_ref, *, lv):
    c = pl.program_id(0)
    nb, L, _ = q_ref.shape

    @pl.when(c == 0)
    def _():
        for b in range(nb):
            st_ref[b] = s0_ref[0]

    li = lax.broadcasted_iota(jnp.int32, (L, L), 0)
    mi = lax.broadcasted_iota(jnp.int32, (L, L), 1)
    diff = (li - mi).astype(F32)
    n = lax.broadcasted_iota(jnp.int32, (L, 1), 0).astype(F32)
    for h in range(RET_HEADS):
        lg = LOG_GAMMA[h]
        sl = slice(h * RET_DK, (h + 1) * RET_DK)
        decay = jnp.where(diff >= 0, jnp.exp(lg * jnp.maximum(diff, 0.0)), 0.0)
        qdec = jnp.exp(lg * (n + 1.0))
        kdec = jnp.where(n < lv, jnp.exp(lg * jnp.maximum(lv - 1.0 - n, 0.0)), 0.0)
        for b in range(nb):
            q = q_ref[b, :, sl]
            k = k_ref[b, :, sl]
            v = v_ref[b, :, sl]
            s = lax.dot_general(q, k, (((1,), (1,)), ((), ())), preferred_element_type=F32) * decay
            inner = jnp.dot(s.astype(BF16), v, preferred_element_type=F32)
            state = st_ref[b, h]
            qd = (q.astype(F32) * qdec).astype(BF16)
            cross = jnp.dot(qd, state.astype(BF16), preferred_element_type=F32)
            o = inner + cross
            kd = (k.astype(F32) * kdec).astype(BF16)
            st_ref[b, h] = math.exp(lg * lv) * state + lax.dot_general(
                kd, v, (((0,), (0,)), ((), ())), preferred_element_type=F32)
            of = o * lax.rsqrt(jnp.mean(o * o, axis=-1, keepdims=True) + EPS)
            g = g_ref[b, :, sl].astype(F32)
            o_ref[b, :, sl] = (of * (g * jax.nn.sigmoid(g))).astype(BF16)

    @pl.when(c == pl.num_programs(0) - 1)
    def _():
        sf_ref[...] = st_ref[...]


def _retention_chunks(qr, kr, vr, gr, s0, nb, L, lv):
    n = qr.shape[0]
    t = n // nb
    as3 = lambda a: a.reshape(nb, t, RET_W)
    blk = pl.BlockSpec((nb, L, RET_W), lambda c: (0, c, 0))
    return pl.pallas_call(
        functools.partial(_ret_chunk_kernel, lv=float(lv)),
        grid=(t // L,),
        in_specs=[blk, blk, blk, blk, _const_spec(s0.shape)],
        out_specs=(blk, pl.BlockSpec((nb, RET_HEADS, RET_DK, RET_DV), lambda c: (0, 0, 0, 0))),
        out_shape=(jax.ShapeDtypeStruct((nb, t, RET_W), BF16),
                   jax.ShapeDtypeStruct((nb, RET_HEADS, RET_DK, RET_DV), F32)),
        scratch_shapes=[pltpu.VMEM((nb, RET_HEADS, RET_DK, RET_DV), F32)],
        compiler_params=pltpu.CompilerParams(dimension_semantics=("arbitrary",),
                                             vmem_limit_bytes=VMEM_LIMIT),
        name="ret_chunks",
    )(as3(qr), as3(kr), as3(vr), as3(gr), s0)


def _ret_decode_kernel(q_ref, k_ref, v_ref, g_ref, s_ref, o_ref, sn_ref, *, ls):
    R = q_ref.shape[0]
    per_tile = ROW_TILE // ls
    li = lax.broadcasted_iota(jnp.int32, (R, R), 0)
    mi = lax.broadcasted_iota(jnp.int32, (R, R), 1)
    same = (li // ls) == (mi // ls)
    diff = ((li % ls) - (mi % ls)).astype(F32)
    t_col = (lax.broadcasted_iota(jnp.int32, (R, 1), 0) % ls).astype(F32)
    seq_in_tile = lax.broadcasted_iota(jnp.int32, (ROW_TILE, 1), 0) // ls
    for h in range(RET_HEADS):
        lg = LOG_GAMMA[h]
        sl = slice(h * RET_DK, (h + 1) * RET_DK)
        q = q_ref[:, sl]
        k = k_ref[:, sl]
        v = v_ref[:, sl]
        decay = jnp.where(same & (diff >= 0), jnp.exp(lg * jnp.maximum(diff, 0.0)), 0.0)
        s = lax.dot_general(q, k, (((1,), (1,)), ((), ())), preferred_element_type=F32) * decay
        inner = jnp.dot(s.astype(BF16), v, preferred_element_type=F32)
        qd = q.astype(F32) * jnp.exp(lg * (t_col + 1.0))
        kd = k.astype(F32) * jnp.exp(lg * (ls - 1.0 - t_col))
        sdec = math.exp(lg * ls)
        cross_tiles = []
        for tt in range(R // ROW_TILE):
            rows = slice(tt * ROW_TILE, (tt + 1) * ROW_TILE)
            qd_t = qd[rows].astype(BF16)
            kd_t = kd[rows]
            v_t = v[rows]
            acc = jnp.zeros((ROW_TILE, RET_DV), F32)
            for j in range(per_tile):
                b = tt * per_tile + j
                state = s_ref[b, h]
                mine = seq_in_tile == j
                cr = jnp.dot(qd_t, state.astype(BF16), preferred_element_type=F32)
                acc = jnp.where(mine, cr, acc)
                kdm = jnp.where(mine, kd_t, 0.0).astype(BF16)
                sn_ref[b, h] = sdec * state + lax.dot_general(
                    kdm, v_t, (((0,), (0,)), ((), ())), preferred_element_type=F32)
            cross_tiles.append(acc)
        o = inner + jnp.concatenate(cross_tiles, axis=0)
        of = o * lax.rsqrt(jnp.mean(o * o, axis=-1, keepdims=True) + EPS)
        g = g_ref[:, sl].astype(F32)
        o_ref[:, sl] = (of * (g * jax.nn.sigmoid(g))).astype(BF16)


def _retention_decode(qr, kr, vr, gr, state, ls, rows):
    n = qr.shape[0]
    nseq = rows // ls
    row = pl.BlockSpec((rows, RET_W), lambda i: (i, 0))
    st = pl.BlockSpec((nseq, RET_HEADS, RET_DK, RET_DV), lambda i: (i, 0, 0, 0))
    return pl.pallas_call(
        functools.partial(_ret_decode_kernel, ls=ls),
        grid=(n // rows,),
        in_specs=[row, row, row, row, st],
        out_specs=(row, st),
        out_shape=(jax.ShapeDtypeStruct((n, RET_W), BF16),
                   jax.ShapeDtypeStruct(state.shape, F32)),
        compiler_params=pltpu.CompilerParams(dimension_semantics=("arbitrary",),
                                             vmem_limit_bytes=VMEM_LIMIT),
        name="ret_decode",
    )(qr, kr, vr, gr, state)


def _mla_prompt_kernel(q_ref, k_ref, km_ref, o_ref, vt_ref, vmt_ref, m_ref, l_ref, acc_ref, *, tq, tk):
    assert tk % tq == 0 and tk <= 2 * tq
    i = pl.program_id(1)
    nkm = km_ref.shape[0]
    nq = MLA_HEADS * tq

    @pl.when(i == 0)
    def _():
        vmt_ref[...] = km_ref[:, 0:KV_LORA].T
        for c in range(k_ref.shape[0] // tq):
            vt_ref[:, c * tq:(c + 1) * tq] = k_ref[c * tq:(c + 1) * tq, 0:KV_LORA].T

    q_all = q_ref[...].reshape(nq, QCAT_W)
    groups = [slice(g * nq // 2, (g + 1) * nq // 2) for g in range(2)]

    def scores(kc, grp):
        return lax.dot_general(kc, q_all[grp], (((1,), (1,)), ((), ())),
                               preferred_element_type=F32)

    s = scores(km_ref[...], slice(0, nq))
    s = jnp.where(lax.broadcasted_iota(jnp.int32, (nkm, nq), 0) < N_META, s, NEG_INF)
    m = jnp.max(s, axis=0, keepdims=True)
    p = jnp.exp(s - m)
    m_ref[...] = m
    l_ref[...] = jnp.sum(p, axis=0, keepdims=True)
    acc_ref[...] = jnp.dot(vmt_ref[...], p.astype(BF16), preferred_element_type=F32)

    def update(start, size, mask):
        kc = k_ref[pl.ds(start, size), :]
        vt = vt_ref[:, pl.ds(start, size)]
        ss = [scores(kc, grp) for grp in groups]
        for grp, s in zip(groups, ss):
            if mask is not None:
                s = jnp.where(mask[:, grp], s, NEG_INF)
            m_prev = m_ref[:, grp]
            m_new = jnp.maximum(m_prev, jnp.max(s, axis=0, keepdims=True))
            alpha = jnp.exp(m_prev - m_new)
            p = jnp.exp(s - m_new)
            m_ref[:, grp] = m_new
            l_ref[:, grp] = alpha * l_ref[:, grp] + jnp.sum(p, axis=0, keepdims=True)
            acc_ref[:, grp] = alpha * acc_ref[:, grp] + jnp.dot(
                vt, p.astype(BF16), preferred_element_type=F32)

    n_vis = i * tq

    def body(j, carry):
        update(pl.multiple_of(j * tk, tk), tk, None)
        return carry

    lax.fori_loop(0, n_vis // tk, body, 0)
    if tk > tq:
        @pl.when(n_vis % tk != 0)
        def _():
            update(pl.multiple_of(n_vis - tq, tq), tq, None)

    key = lax.broadcasted_iota(jnp.int32, (tq, nq), 0)
    qry = lax.broadcasted_iota(jnp.int32, (tq, nq), 1) % tq
    update(pl.multiple_of(n_vis, tq), tq, key <= qry)
    out = acc_ref[...] / l_ref[...]
    for h in range(MLA_HEADS):
        o_ref[:, h * KV_LORA:(h + 1) * KV_LORA] = out[:, h * tq:(h + 1) * tq].T.astype(BF16)


def _mla_prompt(qcat, kvcat, kmeta, nb, tq, tk):
    n = kvcat.shape[0]
    t = n // nb
    nq = t // tq
    return pl.pallas_call(
        functools.partial(_mla_prompt_kernel, tq=tq, tk=tk),
        grid=(nb, nq),
        in_specs=[pl.BlockSpec((MLA_HEADS, tq, QCAT_W), lambda b, i: (0, b * nq + i, 0)),
                  pl.BlockSpec((t, QCAT_W), lambda b, i: (b, 0)),
                  _const_spec(kmeta.shape)],
        out_specs=pl.BlockSpec((tq, MLA_HEADS * KV_LORA), lambda b, i: (b * nq + i, 0)),
        out_shape=jax.ShapeDtypeStruct((n, MLA_HEADS * KV_LORA), BF16),
        scratch_shapes=[pltpu.VMEM((KV_LORA, t), BF16), pltpu.VMEM((KV_LORA, kmeta.shape[0]), BF16),
                        pltpu.VMEM((1, MLA_HEADS * tq), F32), pltpu.VMEM((1, MLA_HEADS * tq), F32),
                        pltpu.VMEM((KV_LORA, MLA_HEADS * tq), F32)],
        compiler_params=pltpu.CompilerParams(dimension_semantics=("arbitrary", "arbitrary"),
                                             vmem_limit_bytes=VMEM_LIMIT),
        name="mla_prompt",
    )(qcat, kvcat, kmeta)


def _mla_meta_kernel(q_ref, km_ref, o_ref):
    km = km_ref[...]
    r = q_ref.shape[1]
    row = lax.broadcasted_iota(jnp.int32, (r, km.shape[0]), 0)
    col = lax.broadcasted_iota(jnp.int32, (r, km.shape[0]), 1)
    mask = (col <= row) & (col < N_META)
    for h in range(MLA_HEADS):
        s = lax.dot_general(q_ref[h], km, (((1,), (1,)), ((), ())), preferred_element_type=F32)
        s = jnp.where(mask, s, NEG_INF)
        p = jnp.exp(s - jnp.max(s, axis=-1, keepdims=True))
        acc = jnp.dot(p.astype(BF16), km[:, 0:KV_LORA], preferred_element_type=F32)
        o_ref[:, h * KV_LORA:(h + 1) * KV_LORA] = (
            acc / jnp.sum(p, axis=-1, keepdims=True)).astype(BF16)


def _mla_meta(qcat, kmeta):
    r = qcat.shape[1]
    return pl.pallas_call(
        _mla_meta_kernel,
        out_shape=jax.ShapeDtypeStruct((r, MLA_HEADS * KV_LORA), BF16),
        compiler_params=pltpu.CompilerParams(vmem_limit_bytes=VMEM_LIMIT),
        name="mla_meta",
    )(qcat, kmeta)


def _mla_decode_kernel(pt_ref, q_ref, kself_ref, peself_ref, kv_hbm, pe_hbm, o_ref,
                       kvbuf, pebuf, kb16, pb16, s_ref, sem, *, n_pages, page, ls, chunk):
    b = pl.program_id(0)
    nb = pl.num_programs(0)
    slot = b % 2
    n_keys = n_pages * page
    rope_rows = pb16.shape[0]

    def page_copies(seq, sl):
        out = []
        for j in range(n_pages):
            pid = pt_ref[seq, j]
            out.append(pltpu.make_async_copy(kv_hbm.at[pid], kvbuf.at[sl, pl.ds(j * page, page)],
                                             sem.at[0, sl]))
            out.append(pltpu.make_async_copy(pe_hbm.at[pid], pebuf.at[sl, j], sem.at[1, sl]))
        return out

    @pl.when(b == 0)
    def _():
        pb16[QK_ROPE:rope_rows, :] = jnp.zeros((rope_rows - QK_ROPE, n_keys), BF16)
        for c in page_copies(0, 0):
            c.start()

    @pl.when(b + 1 < nb)
    def _():
        for c in page_copies(b + 1, 1 - slot):
            c.start()

    for c in page_copies(b, slot):
        c.wait()

    q = q_ref[0]
    rows = q.shape[0]
    ql = q[:, 0:KV_LORA]
    qp = q[:, KV_LORA:QCAT_W]
    ppc = chunk // page
    for c in range(n_keys // chunk):
        ksl = slice(c * chunk, (c + 1) * chunk)
        kb16[ksl, :] = kvbuf[slot, ksl, :].astype(BF16)
        for j in range(ppc):
            jj = c * ppc + j
            pb16[0:QK_ROPE, jj * page:(jj + 1) * page] = pebuf[slot, jj].astype(BF16)
        s_ref[:, ksl] = (
            lax.dot_general(ql, kb16[ksl, :], (((1,), (1,)), ((), ())), preferred_element_type=F32)
            + jnp.dot(qp, pb16[:, ksl], preferred_element_type=F32))

    qf = q.astype(F32)
    kvs = kself_ref[0]
    pes = peself_ref[0]
    t_row = lax.broadcasted_iota(jnp.int32, (rows, 1), 0) // MLA_HEADS
    ss = []
    for t2 in range(ls):
        s_t = (jnp.sum(qf[:, 0:KV_LORA] * kvs[t2:t2 + 1, :], axis=-1, keepdims=True)
               + jnp.sum(qf[:, KV_LORA:KV_LORA + QK_ROPE] * pes[t2:t2 + 1, :], axis=-1, keepdims=True))
        ss.append(jnp.where(t_row >= t2, s_t, NEG_INF))
    m = jnp.max(s_ref[...], axis=-1, keepdims=True)
    for s_t in ss:
        m = jnp.maximum(m, s_t)
    l = jnp.zeros((rows, 1), F32)
    acc = jnp.zeros((rows, KV_LORA), F32)
    for t2 in range(ls):
        p_t = jnp.exp(ss[t2] - m)
        l = l + p_t
        acc = acc + p_t * kvs[t2:t2 + 1, :]
    for c in range(n_keys // chunk):
        ksl = slice(c * chunk, (c + 1) * chunk)
        p = jnp.exp(s_ref[:, ksl] - m)
        l = l + jnp.sum(p, axis=-1, keepdims=True)
        acc = acc + jnp.dot(p.astype(BF16), kb16[ksl, :], preferred_element_type=F32)
    o_ref[0] = (acc / l).astype(BF16)


def _mla_decode(page_table, q, kself, peself, cache_kv, cache_pe_t, chunk):
    db, rows, _ = q.shape
    ls = kself.shape[1]
    n_pages = page_table.shape[1]
    page = cache_kv.shape[1]
    n_keys = n_pages * page
    grid_spec = pltpu.PrefetchScalarGridSpec(
        num_scalar_prefetch=1,
        grid=(db,),
        in_specs=[pl.BlockSpec((1, rows, QCAT_W), lambda b, pt: (b, 0, 0)),
                  pl.BlockSpec((1, ls, KV_LORA), lambda b, pt: (b, 0, 0)),
                  pl.BlockSpec((1, ls, QK_ROPE), lambda b, pt: (b, 0, 0)),
                  pl.BlockSpec(memory_space=pl.ANY),
                  pl.BlockSpec(memory_space=pl.ANY)],
        out_specs=pl.BlockSpec((1, rows, KV_LORA), lambda b, pt: (b, 0, 0)),
        scratch_shapes=[pltpu.VMEM((2, n_keys, KV_LORA), F32),
                        pltpu.VMEM((2, n_pages, QK_ROPE, page), F32),
                        pltpu.VMEM((n_keys, KV_LORA), BF16),
                        pltpu.VMEM((QCAT_W - KV_LORA, n_keys), BF16),
                        pltpu.VMEM((rows, n_keys), F32),
                        pltpu.SemaphoreType.DMA((2, 2))],
    )
    return pl.pallas_call(
        functools.partial(_mla_decode_kernel, n_pages=n_pages, page=page, ls=ls, chunk=chunk),
        grid_spec=grid_spec,
        out_shape=jax.ShapeDtypeStruct((db, rows, KV_LORA), BF16),
        compiler_params=pltpu.CompilerParams(dimension_semantics=("arbitrary",),
                                             vmem_limit_bytes=VMEM_LIMIT),
        name="mla_decode",
    )(page_table, q, kself, peself, cache_kv, cache_pe_t)


def _mix_ffn_kernel(x_ref, ret_ref, olat_ref, cin_ref, wuv_ref, wout_ref, gffn_ref, wup_ref,
                    cw_ref, cb_ref, wdown_ref, gfin_ref, y_ref, cout_ref,
                    carry_ref, stage_ref, *, stride, carry_end, cw):
    t = pl.program_id(1)
    tm = x_ref.shape[0]
    hal = carry_ref.shape[0]
    d_ff = wdown_ref.shape[0]

    @pl.when(t == 0)
    def _():
        carry_ref[...] = cin_ref[...]

    mla = jnp.concatenate(
        [jnp.dot(olat_ref[:, h * KV_LORA:(h + 1) * KV_LORA], wuv_ref[h],
                 preferred_element_type=F32).astype(BF16) for h in range(MLA_HEADS)], axis=1)
    mixed = jnp.concatenate([ret_ref[...], mla], axis=1)
    h1 = x_ref[...] + jnp.dot(mixed, wout_ref[...], preferred_element_type=F32)
    a2 = _rms(h1, gffn_ref[...]).astype(BF16)

    def conv_half(c0, slot):
        u = jnp.dot(a2, wup_ref[:, c0:c0 + cw], preferred_element_type=F32)
        stage_ref[slot, 0:hal, :] = carry_ref[:, c0:c0 + cw]
        stage_ref[slot, hal:hal + tm, :] = u
        carry_ref[:, c0:c0 + cw] = u[carry_end - hal:carry_end, :]
        um1 = stage_ref[slot, hal - stride:hal - stride + tm, :]
        um2 = stage_ref[slot, hal - 2 * stride:hal - 2 * stride + tm, :]
        return (cb_ref[:, c0:c0 + cw] + cw_ref[0:1, c0:c0 + cw] * um2
                + cw_ref[1:2, c0:c0 + cw] * um1 + cw_ref[2:3, c0:c0 + cw] * u)

    n_slots = stage_ref.shape[0]
    y = None
    for jc in range(d_ff // cw):
        ca = conv_half(jc * cw, (2 * jc) % n_slots)
        cg = conv_half(d_ff + jc * cw, (2 * jc + 1) % n_slots)
        hmid = (cg * jax.nn.sigmoid(cg) * ca).astype(BF16)
        part = jnp.dot(hmid, wdown_ref[jc * cw:(jc + 1) * cw, :], preferred_element_type=F32)
        y = part if y is None else y + part
    h2 = h1 + y
    y_ref[...] = _rms(h2, gfin_ref[...])

    @pl.when(t == pl.num_programs(1) - 1)
    def _():
        cout_ref[0] = carry_ref[...]


def _mix_ffn(x, ret, olat, carry_in, wts, nseq, tm, stride, carry_end, cw):
    n, d = x.shape
    hal = carry_in.shape[0]
    nt = n // (nseq * tm)
    wuv, wout, gffn, wup, convw, convb, wdown, gfin = wts
    row = lambda w: pl.BlockSpec((tm, w), lambda b, t: (b * nt + t, 0))
    return pl.pallas_call(
        functools.partial(_mix_ffn_kernel, stride=stride, carry_end=carry_end, cw=cw),
        grid=(nseq, nt),
        in_specs=[row(d), row(RET_W), row(MLA_HEADS * KV_LORA), _const_spec(carry_in.shape),
                  _const_spec(wuv.shape), _const_spec(wout.shape), _const_spec(gffn.shape),
                  _const_spec(wup.shape), _const_spec(convw.shape), _const_spec(convb.shape),
                  _const_spec(wdown.shape), _const_spec(gfin.shape)],
        out_specs=(row(d), pl.BlockSpec((1, hal, wup.shape[1]), lambda b, t: (b, 0, 0))),
        out_shape=(jax.ShapeDtypeStruct((n, d), F32),
                   jax.ShapeDtypeStruct((nseq, hal, wup.shape[1]), F32)),
        scratch_shapes=[pltpu.VMEM((hal, wup.shape[1]), F32),
                        pltpu.VMEM((4, hal + tm, cw), F32)],
        compiler_params=pltpu.CompilerParams(dimension_semantics=("arbitrary", "arbitrary"),
                                             vmem_limit_bytes=VMEM_LIMIT),
        name="mix_ffn",
    )(x, ret, olat, carry_in, wuv, wout, gffn, wup, convw, convb, wdown, gfin)


def _rope_tables(pos):
    pos = pos.astype(F32)[:, None]

    def cs(dim):
        inv = ROPE_THETA ** (-jnp.arange(0, dim, 2, dtype=F32) / dim)
        ang = pos * inv[None, :]
        return jnp.cos(ang), jnp.sin(ang)

    c, s = cs(RET_DK)
    c128 = jnp.concatenate([c, c], axis=-1)
    s128 = jnp.concatenate([-s, s], axis=-1)
    c, s = cs(QK_ROPE)
    z32 = jnp.zeros_like(s)
    z64 = jnp.zeros((pos.shape[0], LANES - QK_ROPE), F32)
    c64 = jnp.concatenate([c, c, z64], axis=-1)
    s64a = jnp.concatenate([-s, z32, z64], axis=-1)
    s64b = jnp.concatenate([z32, s, z64], axis=-1)
    return c128, s128, c64, s64a, s64b


def kernel(x_prompt, x_sample, cache_kv_latent, cache_k_rope, state_retention, state_ffn_conv,
           page_table, meta_tokens, g_mix, w_in, g_q, w_uq, g_kv, w_uk, w_uv, w_out,
           g_ffn, w_up, conv_w, conv_b, w_down, g_final):
    nb, seq, d = x_prompt.shape
    db, ls, _ = x_sample.shape
    depth = w_in.shape[0]
    assert depth == 1, "single-layer step"
    n_pages = page_table.shape[1]
    page = cache_kv_latent.shape[2]
    past_len = n_pages * page
    d_ff = w_down.shape[1]
    l = 0

    w = w_in[l]
    win = jnp.concatenate([w, jnp.zeros((d, LANES - QK_ROPE), w.dtype)], axis=1).astype(BF16)
    wq = w_uq[l].reshape(Q_LORA, MLA_HEADS, QK_NOPE + QK_ROPE)
    wuq = jnp.concatenate([wq, jnp.zeros((Q_LORA, MLA_HEADS, LANES - QK_ROPE), wq.dtype)],
                          axis=-1).reshape(Q_LORA, MLA_HEADS * 2 * LANES).astype(BF16)
    wuk = jnp.transpose(w_uk[l], (1, 2, 0)).astype(BF16)
    wuv = jnp.transpose(w_uv[l], (1, 0, 2)).astype(BF16)
    proj_w = (g_mix[l][None, :], win, g_q[l][None, :], wuq, g_kv[l][None, :], wuk)
    ffn_w = (wuv, w_out[l].astype(BF16), g_ffn[l][None, :], w_up[l].astype(BF16),
             conv_w[l], conv_b[l][None, :], w_down[l].astype(BF16), g_final[None, :])

    tile = 128
    xm = jnp.concatenate([meta_tokens.astype(F32), jnp.zeros((tile - N_META, d), F32)], axis=0)
    tabs_m = _rope_tables(jnp.arange(tile))
    qr, kr, vr, gr, qcat_m, kvcat_m, ckv_m, kpe_m = _project(xm, tabs_m, proj_w, tile)
    zero_state = jnp.zeros((1, RET_HEADS, RET_DK, RET_DV), F32)
    ret_m, state_m = _retention_chunks(qr, kr, vr, gr, zero_state, 1, tile, N_META)
    ret_m = ret_m.reshape(tile, RET_W)
    row_valid = (jnp.arange(tile) < N_META)[:, None]
    kmeta = jnp.where(row_valid, kvcat_m, jnp.zeros_like(kvcat_m))
    olat_m = _mla_meta(qcat_m, kmeta)
    _, carry_m = _mix_ffn(xm, ret_m, olat_m, jnp.zeros((8, 2 * d_ff), F32), ffn_w,
                          1, tile, 1, N_META, 256)

    tm = 512
    xp = x_prompt.reshape(nb * seq, d)
    tabs_p = _rope_tables(N_META + jnp.arange(seq))
    qr, kr, vr, gr, qcat, kvcat, ckv_p, kpe_p = _project(xp, tabs_p, proj_w, tm)
    ret_p, state_p = _retention_chunks(qr, kr, vr, gr, state_m, nb, 128, 128)
    ret_p = ret_p.reshape(nb * seq, RET_W)
    olat_p = _mla_prompt(qcat, kvcat, kmeta, nb, 256, 512)
    y_p, carry_p = _mix_ffn(xp, ret_p, olat_p, carry_m[0], ffn_w, nb, 512, 1, 512, 256)

    ns = db * ls
    xs = x_sample.reshape(ns, d)
    pos_s = past_len + jnp.arange(ls)
    tabs_s = tuple(jnp.tile(t, (db, 1)) for t in _rope_tables(pos_s))
    qr, kr, vr, gr, qcat_s, _, ckv_s, kpe_s = _project(xs, tabs_s, proj_w, min(ns, 256))
    ret_s, state_s = _retention_decode(qr, kr, vr, gr, state_retention[l], ls, min(ns, 64))
    q_s = jnp.transpose(qcat_s.reshape(MLA_HEADS, db, ls, QCAT_W), (1, 2, 0, 3)).reshape(
        db, ls * MLA_HEADS, QCAT_W)
    cache_pe_t = jnp.swapaxes(cache_k_rope[l], 1, 2)
    olat_s = _mla_decode(page_table, q_s, ckv_s.reshape(db, ls, KV_LORA),
                         kpe_s.reshape(db, ls, QK_ROPE), cache_kv_latent[l], cache_pe_t,
                         min(1024, past_len))
    olat_s = olat_s.reshape(ns, MLA_HEADS * KV_LORA)
    tmaj = lambda a: jnp.transpose(a.reshape(db, ls, a.shape[-1]), (1, 0, 2)).reshape(ns, a.shape[-1])
    carry_s_in = jnp.transpose(state_ffn_conv[l], (1, 0, 2)).reshape((CONV_W - 1) * db, 2 * d_ff)
    y_s, carry_s = _mix_ffn(tmaj(xs), tmaj(ret_s), tmaj(olat_s), carry_s_in, ffn_w,
                            1, ns, db, ns, 256)
    y_s = jnp.transpose(y_s.reshape(ls, db, d), (1, 0, 2))
    conv_s = jnp.transpose(carry_s.reshape(CONV_W - 1, db, 2 * d_ff), (1, 0, 2))

    bc = lambda a: jnp.broadcast_to(a[:N_META][None], (nb, N_META, a.shape[-1]))
    kv_p = jnp.concatenate([bc(ckv_m), ckv_p.reshape(nb, seq, KV_LORA)], axis=1)
    pe_p = jnp.concatenate([bc(kpe_m), kpe_p.reshape(nb, seq, QK_ROPE)], axis=1)
    return (y_p.reshape(nb, seq, d), y_s,
            kv_p[None], pe_p[None],
            state_p[None],
            carry_p[:, 8 - (CONV_W - 1):, :][None],
            ckv_s.reshape(db, ls, KV_LORA)[None], kpe_s.reshape(db, ls, QK_ROPE)[None],
            state_s[None], conv_s[None])
```

```python
import functools
import math

import jax
import jax.numpy as jnp
from jax import lax
from jax.experimental import pallas as pl
from jax.experimental.pallas import tpu as pltpu

F32 = jnp.float32
BF16 = jnp.bfloat16

N_META = 16
RET_HEADS = 4
RET_DK = 128
RET_DV = 128
MLA_HEADS = 4
Q_LORA = 384
KV_LORA = 256
QK_NOPE = 128
QK_ROPE = 64
V_DIM = 128
CONV_W = 3
ROPE_THETA = 10000.0
EPS = 1e-6
SOFTMAX_SCALE = (QK_NOPE + QK_ROPE) ** -0.5
RET_W = RET_HEADS * RET_DV
LANES = 128
ROW_TILE = 16
QCAT_W = KV_LORA + LANES
LOG_GAMMA = tuple(math.log1p(-(2.0 ** (-5.0 - h))) for h in range(RET_HEADS))
VMEM_LIMIT = 56 * 1024 * 1024
NEG_INF = float("-inf")


def _const_spec(shape):
    nd = len(shape)
    return pl.BlockSpec(shape, lambda *_: (0,) * nd, pipeline_mode=pl.Buffered(1))


def _rms(x, g):
    return x * lax.rsqrt(jnp.mean(x * x, axis=-1, keepdims=True) + EPS) * g


def _proj_kernel(x_ref, c128_ref, s128_ref, c64_ref, s64a_ref, s64b_ref,
                 gmix_ref, win_ref, gq_ref, wuq_ref, gkv_ref, wuk_ref,
                 qr_ref, kr_ref, vr_ref, gr_ref, qcat_ref, kvcat_ref, ckv_ref, kpe_ref):
    a = _rms(x_ref[...], gmix_ref[...])
    z = jnp.dot(a.astype(BF16), win_ref[...], preferred_element_type=F32)
    c128 = c128_ref[...]
    s128 = s128_ref[...]
    c64 = c64_ref[...]
    s64a = s64a_ref[...]
    s64b = s64b_ref[...]

    def rope128(v):
        return v * c128 + pltpu.roll(v, 64, 1) * s128

    def rope64(v):
        return v * c64 + pltpu.roll(v, 96, 1) * s64a + pltpu.roll(v, 32, 1) * s64b

    for h in range(RET_HEADS):
        sl = slice(h * RET_DK, (h + 1) * RET_DK)
        qr_ref[:, sl] = rope128(z[:, sl]).astype(BF16)
        ksl = slice(RET_W + h * RET_DK, RET_W + (h + 1) * RET_DK)
        kr_ref[:, sl] = (rope128(z[:, ksl]) * (RET_DK ** -0.5)).astype(BF16)
    vr_ref[...] = z[:, 2 * RET_W:3 * RET_W].astype(BF16)
    gr_ref[...] = z[:, 3 * RET_W:4 * RET_W].astype(BF16)

    o = 4 * RET_W
    cqn = _rms(z[:, o:o + Q_LORA], gq_ref[...])
    q2 = jnp.dot(cqn.astype(BF16), wuq_ref[...], preferred_element_type=F32)
    for h in range(MLA_HEADS):
        b0 = h * 2 * LANES
        nope = q2[:, b0:b0 + QK_NOPE]
        pe = rope64(q2[:, b0 + LANES:b0 + 2 * LANES])
        qlat = jnp.dot(nope.astype(BF16), wuk_ref[h], preferred_element_type=F32)
        qcat_ref[h, :, 0:KV_LORA] = (qlat * SOFTMAX_SCALE).astype(BF16)
        qcat_ref[h, :, KV_LORA:QCAT_W] = (pe * SOFTMAX_SCALE).astype(BF16)

    o += Q_LORA
    ckvn = _rms(z[:, o:o + KV_LORA], gkv_ref[...])
    ckv_ref[...] = ckvn
    o += KV_LORA
    kp = rope64(z[:, o:o + LANES])
    kpe_ref[...] = kp[:, 0:QK_ROPE]
    kvcat_ref[:, 0:KV_LORA] = ckvn.astype(BF16)
    kvcat_ref[:, KV_LORA:QCAT_W] = kp.astype(BF16)


def _project(x, tabs, wts, tm):
    n, d = x.shape
    nt = tabs[0].shape[0] // tm
    gmix, win, gq, wuq, gkv, wuk = wts
    row = lambda w: pl.BlockSpec((tm, w), lambda i: (i, 0))
    tab = pl.BlockSpec((tm, LANES), lambda i: (i % nt, 0))
    out_shapes = (
        jax.ShapeDtypeStruct((n, RET_W), BF16), jax.ShapeDtypeStruct((n, RET_W), BF16),
        jax.ShapeDtypeStruct((n, RET_W), BF16), jax.ShapeDtypeStruct((n, RET_W), BF16),
        jax.ShapeDtypeStruct((MLA_HEADS, n, QCAT_W), BF16),
        jax.ShapeDtypeStruct((n, QCAT_W), BF16),
        jax.ShapeDtypeStruct((n, KV_LORA), F32),
        jax.ShapeDtypeStruct((n, QK_ROPE), F32),
    )
    out_specs = (row(RET_W), row(RET_W), row(RET_W), row(RET_W),
                 pl.BlockSpec((MLA_HEADS, tm, QCAT_W), lambda i: (0, i, 0)),
                 row(QCAT_W), row(KV_LORA), row(QK_ROPE))
    return pl.pallas_call(
        _proj_kernel,
        grid=(n // tm,),
        in_specs=[row(d), tab, tab, tab, tab, tab,
                  _const_spec(gmix.shape), _const_spec(win.shape), _const_spec(gq.shape),
                  _const_spec(wuq.shape), _const_spec(gkv.shape), _const_spec(wuk.shape)],
        out_specs=out_specs,
        out_shape=out_shapes,
        compiler_params=pltpu.CompilerParams(dimension_semantics=("arbitrary",),
                                             vmem_limit_bytes=VMEM_LIMIT),
        name="proj",
    )(x, *tabs, gmix, win, gq, wuq, gkv, wuk)


def _ret_chunk_kernel(q_ref, k_ref, v_ref, g_ref, s0_ref, o_ref, sf_ref, st_ref, *, lv):
    c = pl.program_id(0)
    nb, L, _ = q_ref.shape

    @pl.when(c == 0)
    def _():
        for b in range(nb):
            st_ref[b] = s0_ref[0]

    li = lax.broadcasted_iota(jnp.int32, (L, L), 0)
    mi = lax.broadcasted_iota(jnp.int32, (L, L), 1)
    diff = (li - mi).astype(F32)
    n = lax.broadcasted_iota(jnp.int32, (L, 1), 0).astype(F32)
    for h in range(RET_HEADS):
        lg = LOG_GAMMA[h]
        sl = slice(h * RET_DK, (h + 1) * RET_DK)
        decay = jnp.where(diff >= 0, jnp.exp(lg * jnp.maximum(diff, 0.0)), 0.0)
        qdec = jnp.exp(lg * (n + 1.0))
        kdec = jnp.where(n < lv, jnp.exp(lg * jnp.maximum(lv - 1.0 - n, 0.0)), 0.0)
        for b in range(nb):
            q = q_ref[b, :, sl]
            k = k_ref[b, :, sl]
            v = v_ref[b, :, sl]
            s = lax.dot_general(q, k, (((1,), (1,)), ((), ())), preferred_element_type=F32) * decay
            inner = jnp.dot(s.astype(BF16), v, preferred_element_type=F32)
            state = st_ref[b, h]
            qd = (q.astype(F32) * qdec).astype(BF16)
            cross = jnp.dot(qd, state.astype(BF16), preferred_element_type=F32)
            o = inner + cross
            kd = (k.astype(F32) * kdec).astype(BF16)
            st_ref[b, h] = math.exp(lg * lv) * state + lax.dot_general(
                kd, v, (((0,), (0,)), ((), ())), preferred_element_type=F32)
            of = o * lax.rsqrt(jnp.mean(o * o, axis=-1, keepdims=True) + EPS)
            g = g_ref[b, :, sl].astype(F32)
            o_ref[b, :, sl] = (of * (g * jax.nn.sigmoid(g))).astype(BF16)

    @pl.when(c == pl.num_programs(0) - 1)
    def _():
        sf_ref[...] = st_ref[...]


def _retention_chunks(qr, kr, vr, gr, s0, nb, L, lv):
    n = qr.shape[0]
    t = n // nb
    as3 = lambda a: a.reshape(nb, t, RET_W)
    blk = pl.BlockSpec((nb, L, RET_W), lambda c: (0, c, 0))
    return pl.pallas_call(
        functools.partial(_ret_chunk_kernel, lv=float(lv)),
        grid=(t // L,),
        in_specs=[blk, blk, blk, blk, _const_spec(s0.shape)],
        out_specs=(blk, pl.BlockSpec((nb, RET_HEADS, RET_DK, RET_DV), lambda c: (0, 0, 0, 0))),
        out_shape=(jax.ShapeDtypeStruct((nb, t, RET_W), BF16),
                   jax.ShapeDtypeStruct((nb, RET_HEADS, RET_DK, RET_DV), F32)),
        scratch_shapes=[pltpu.VMEM((nb, RET_HEADS, RET_DK, RET_DV), F32)],
        compiler_params=pltpu.CompilerParams(dimension_semantics=("arbitrary",),
                                             vmem_limit_bytes=VMEM_LIMIT),
        name="ret_chunks",
    )(as3(qr), as3(kr), as3(vr), as3(gr), s0)


def _ret_decode_kernel(q_ref, k_ref, v_ref, g_ref, s_ref, o_ref, sn_ref, *, ls):
    R = q_ref.shape[0]
    per_tile = ROW_TILE // ls
    li = lax.broadcasted_iota(jnp.int32, (R, R), 0)
    mi = lax.broadcasted_iota(jnp.int32, (R, R), 1)
    same = (li // ls) == (mi // ls)
    diff = ((li % ls) - (mi % ls)).astype(F32)
    t_col = (lax.broadcasted_iota(jnp.int32, (R, 1), 0) % ls).astype(F32)
    seq_in_tile = lax.broadcasted_iota(jnp.int32, (ROW_TILE, 1), 0) // ls
    for h in range(RET_HEADS):
        lg = LOG_GAMMA[h]
        sl = slice(h * RET_DK, (h + 1) * RET_DK)
        q = q_ref[:, sl]
        k = k_ref[:, sl]
        v = v_ref[:, sl]
        decay = jnp.where(same & (diff >= 0), jnp.exp(lg * jnp.maximum(diff, 0.0)), 0.0)
        s = lax.dot_general(q, k, (((1,), (1,)), ((), ())), preferred_element_type=F32) * decay
        inner = jnp.dot(s.astype(BF16), v, preferred_element_type=F32)
        qd = q.astype(F32) * jnp.exp(lg * (t_col + 1.0))
        kd = k.astype(F32) * jnp.exp(lg * (ls - 1.0 - t_col))
        sdec = math.exp(lg * ls)
        cross_tiles = []
        for tt in range(R // ROW_TILE):
            rows = slice(tt * ROW_TILE, (tt + 1) * ROW_TILE)
            qd_t = qd[rows].astype(BF16)
            kd_t = kd[rows]
            v_t = v[rows]
            acc = jnp.zeros((ROW_TILE, RET_DV), F32)
            for j in range(per_tile):
                b = tt * per_tile + j
                state = s_ref[b, h]
                mine = seq_in_tile == j
                cr = jnp.dot(qd_t, state.astype(BF16), preferred_element_type=F32)
                acc = jnp.where(mine, cr, acc)
                kdm = jnp.where(mine, kd_t, 0.0).astype(BF16)
                sn_ref[b, h] = sdec * state + lax.dot_general(
                    kdm, v_t, (((0,), (0,)), ((), ())), preferred_element_type=F32)
            cross_tiles.append(acc)
        o = inner + jnp.concatenate(cross_tiles, axis=0)
        of = o * lax.rsqrt(jnp.mean(o * o, axis=-1, keepdims=True) + EPS)
        g = g_ref[:, sl].astype(F32)
        o_ref[:, sl] = (of * (g * jax.nn.sigmoid(g))).astype(BF16)


def _retention_decode(qr, kr, vr, gr, state, ls, rows):
    n = qr.shape[0]
    nseq = rows // ls
    row = pl.BlockSpec((rows, RET_W), lambda i: (i, 0))
    st = pl.BlockSpec((nseq, RET_HEADS, RET_DK, RET_DV), lambda i: (i, 0, 0, 0))
    return pl.pallas_call(
        functools.partial(_ret_decode_kernel, ls=ls),
        grid=(n // rows,),
        in_specs=[row, row, row, row, st],
        out_specs=(row, st),
        out_shape=(jax.ShapeDtypeStruct((n, RET_W), BF16),
                   jax.ShapeDtypeStruct(state.shape, F32)),
        compiler_params=pltpu.CompilerParams(dimension_semantics=("arbitrary",),
                                             vmem_limit_bytes=VMEM_LIMIT),
        name="ret_decode",
    )(qr, kr, vr, gr, state)


def _mla_prompt_kernel(q_ref, k_ref, km_ref, o_ref, vt_ref, vmt_ref, m_ref, l_ref, acc_ref, *, tq, tk):
    assert tk % tq == 0 and tk <= 2 * tq
    i = pl.program_id(1)
    nkm = km_ref.shape[0]
    nq = MLA_HEADS * tq

    @pl.when(i == 0)
    def _():
        vmt_ref[...] = km_ref[:, 0:KV_LORA].T
        for c in range(k_ref.shape[0] // tq):
            vt_ref[:, c * tq:(c + 1) * tq] = k_ref[c * tq:(c + 1) * tq, 0:KV_LORA].T

    q_all = q_ref[...].reshape(nq, QCAT_W)
    groups = [slice(g * nq // 2, (g + 1) * nq // 2) for g in range(2)]

    def scores(kc, grp):
        return lax.dot_general(kc, q_all[grp], (((1,), (1,)), ((), ())),
                               preferred_element_type=F32)

    s = scores(km_ref[...], slice(0, nq))
    s = jnp.where(lax.broadcasted_iota(jnp.int32, (nkm, nq), 0) < N_META, s, NEG_INF)
    m = jnp.max(s, axis=0, keepdims=True)
    p = jnp.exp(s - m)
    m_ref[...] = m
    l_ref[...] = jnp.sum(p, axis=0, keepdims=True)
    acc_ref[...] = jnp.dot(vmt_ref[...], p.astype(BF16), preferred_element_type=F32)

    def update(start, size, mask):
        kc = k_ref[pl.ds(start, size), :]
        vt = vt_ref[:, pl.ds(start, size)]
        ss = [scores(kc, grp) for grp in groups]
        for grp, s in zip(groups, ss):
            if mask is not None:
                s = jnp.where(mask[:, grp], s, NEG_INF)
            m_prev = m_ref[:, grp]
            m_new = jnp.maximum(m_prev, jnp.max(s, axis=0, keepdims=True))
            alpha = jnp.exp(m_prev - m_new)
            p = jnp.exp(s - m_new)
            m_ref[:, grp] = m_new
            l_ref[:, grp] = alpha * l_ref[:, grp] + jnp.sum(p, axis=0, keepdims=True)
            acc_ref[:, grp] = alpha * acc_ref[:, grp] + jnp.dot(
                vt, p.astype(BF16), preferred_element_type=F32)

    n_vis = i * tq

    def body(j, carry):
        update(pl.multiple_of(j * tk, tk), tk, None)
        return carry

    lax.fori_loop(0, n_vis // tk, body, 0)
    if tk > tq:
        @pl.when(n_vis % tk != 0)
        def _():
            update(pl.multiple_of(n_vis - tq, tq), tq, None)

    key = lax.broadcasted_iota(jnp.int32, (tq, nq), 0)
    qry = lax.broadcasted_iota(jnp.int32, (tq, nq), 1) % tq
    update(pl.multiple_of(n_vis, tq), tq, key <= qry)
    out = acc_ref[...] / l_ref[...]
    for h in range(MLA_HEADS):
        o_ref[:, h * KV_LORA:(h + 1) * KV_LORA] = out[:, h * tq:(h + 1) * tq].T.astype(BF16)


def _mla_prompt(qcat, kvcat, kmeta, nb, tq, tk):
    n = kvcat.shape[0]
    t = n // nb
    nq = t // tq
    return pl.pallas_call(
        functools.partial(_mla_prompt_kernel, tq=tq, tk=tk),
        grid=(nb, nq),
        in_specs=[pl.BlockSpec((MLA_HEADS, tq, QCAT_W), lambda b, i: (0, b * nq + i, 0)),
                  pl.BlockSpec((t, QCAT_W), lambda b, i: (b, 0)),
                  _const_spec(kmeta.shape)],
        out_specs=pl.BlockSpec((tq, MLA_HEADS * KV_LORA), lambda b, i: (b * nq + i, 0)),
        out_shape=jax.ShapeDtypeStruct((n, MLA_HEADS * KV_LORA), BF16),
        scratch_shapes=[pltpu.VMEM((KV_LORA, t), BF16), pltpu.VMEM((KV_LORA, kmeta.shape[0]), BF16),
                        pltpu.VMEM((1, MLA_HEADS * tq), F32), pltpu.VMEM((1, MLA_HEADS * tq), F32),
                        pltpu.VMEM((KV_LORA, MLA_HEADS * tq), F32)],
        compiler_params=pltpu.CompilerParams(dimension_semantics=("arbitrary", "arbitrary"),
                                             vmem_limit_bytes=VMEM_LIMIT),
        name="mla_prompt",
    )(qcat, kvcat, kmeta)


def _mla_meta_kernel(q_ref, km_ref, o_ref):
    km = km_ref[...]
    r = q_ref.shape[1]
    row = lax.broadcasted_iota(jnp.int32, (r, km.shape[0]), 0)
    col = lax.broadcasted_iota(jnp.int32, (r, km.shape[0]), 1)
    mask = (col <= row) & (col < N_META)
    for h in range(MLA_HEADS):
        s = lax.dot_general(q_ref[h], km, (((1,), (1,)), ((), ())), preferred_element_type=F32)
        s = jnp.where(mask, s, NEG_INF)
        p = jnp.exp(s - jnp.max(s, axis=-1, keepdims=True))
        acc = jnp.dot(p.astype(BF16), km[:, 0:KV_LORA], preferred_element_type=F32)
        o_ref[:, h * KV_LORA:(h + 1) * KV_LORA] = (
            acc / jnp.sum(p, axis=-1, keepdims=True)).astype(BF16)


def _mla_meta(qcat, kmeta):
    r = qcat.shape[1]
    return pl.pallas_call(
        _mla_meta_kernel,
        out_shape=jax.ShapeDtypeStruct((r, MLA_HEADS * KV_LORA), BF16),
        compiler_params=pltpu.CompilerParams(vmem_limit_bytes=VMEM_LIMIT),
        name="mla_meta",
    )(qcat, kmeta)


def _mla_decode_kernel(pt_ref, q_ref, kself_ref, peself_ref, kv_hbm, pe_hbm, o_ref,
                       kvbuf, pebuf, kb16, pb16, s_ref, sem, *, n_pages, page, ls, chunk):
    b = pl.program_id(0)
    nb = pl.num_programs(0)
    slot = b % 2
    n_keys = n_pages * page
    rope_rows = pb16.shape[0]

    def page_copies(seq, sl):
        out = []
        for j in range(n_pages):
            pid = pt_ref[seq, j]
            out.append(pltpu.make_async_copy(kv_hbm.at[pid], kvbuf.at[sl, pl.ds(j * page, page)],
                                             sem.at[0, sl]))
            out.append(pltpu.make_async_copy(pe_hbm.at[pid], pebuf.at[sl, j], sem.at[1, sl]))
        return out

    @pl.when(b == 0)
    def _():
        pb16[QK_ROPE:rope_rows, :] = jnp.zeros((rope_rows - QK_ROPE, n_keys), BF16)
        for c in page_copies(0, 0):
            c.start()

    @pl.when(b + 1 < nb)
    def _():
        for c in page_copies(b + 1, 1 - slot):
            c.start()

    for c in page_copies(b, slot):
        c.wait()

    q = q_ref[0]
    rows = q.shape[0]
    ql = q[:, 0:KV_LORA]
    qp = q[:, KV_LORA:QCAT_W]
    ppc = chunk // page
    n_chunks = n_keys // chunk
    for c in range(n_chunks):
        ksl = slice(c * chunk, (c + 1) * chunk)
        kb16[ksl, :] = kvbuf[slot, ksl, :].astype(BF16)
        for j in range(ppc):
            jj = c * ppc + j
            pb16[0:QK_ROPE, jj * page:(jj + 1) * page] = pebuf[slot, jj].astype(BF16)
        s_ref[:, ksl] = (
            lax.dot_general(ql, kb16[ksl, :], (((1,), (1,)), ((), ())), preferred_element_type=F32)
            + jnp.dot(qp, pb16[:, ksl], preferred_element_type=F32))

    parts = []
    n_split = 4 if n_chunks % 4 == 0 else 1
    for h in range(n_split):
        hsl = slice(h * n_keys // n_split, (h + 1) * n_keys // n_split)
        s_h = s_ref[:, hsl]
        m_h = jnp.max(s_h, axis=-1, keepdims=True)
        p_h = jnp.exp(s_h - m_h)
        parts.append((m_h, jnp.sum(p_h, axis=-1, keepdims=True),
                      jnp.dot(p_h.astype(BF16), kb16[hsl, :], preferred_element_type=F32)))

    qf = q.astype(F32)
    kvs = kself_ref[0]
    pes = peself_ref[0]
    t_row = lax.broadcasted_iota(jnp.int32, (rows, 1), 0) // MLA_HEADS
    ss = []
    for t2 in range(ls):
        s_t = (jnp.sum(qf[:, 0:KV_LORA] * kvs[t2:t2 + 1, :], axis=-1, keepdims=True)
               + jnp.sum(qf[:, KV_LORA:KV_LORA + QK_ROPE] * pes[t2:t2 + 1, :], axis=-1, keepdims=True))
        ss.append(jnp.where(t_row >= t2, s_t, NEG_INF))
    m_s = ss[0]
    for s_t in ss[1:]:
        m_s = jnp.maximum(m_s, s_t)
    l_s = jnp.zeros((rows, 1), F32)
    acc_s = jnp.zeros((rows, KV_LORA), F32)
    for t2 in range(ls):
        p_t = jnp.exp(ss[t2] - m_s)
        l_s = l_s + p_t
        acc_s = acc_s + p_t * kvs[t2:t2 + 1, :]
    parts.append((m_s, l_s, acc_s))

    m = parts[0][0]
    for m_h, _, _ in parts[1:]:
        m = jnp.maximum(m, m_h)
    l = jnp.zeros((rows, 1), F32)
    acc = jnp.zeros((rows, KV_LORA), F32)
    for m_h, l_h, acc_h in parts:
        w_h = jnp.exp(m_h - m)
        l = l + w_h * l_h
        acc = acc + w_h * acc_h
    o_ref[0] = (acc / l).astype(BF16)


def _mla_decode(page_table, q, kself, peself, cache_kv, cache_pe_t, chunk):
    db, rows, _ = q.shape
    ls = kself.shape[1]
    n_pages = page_table.shape[1]
    page = cache_kv.shape[1]
    n_keys = n_pages * page
    grid_spec = pltpu.PrefetchScalarGridSpec(
        num_scalar_prefetch=1,
        grid=(db,),
        in_specs=[pl.BlockSpec((1, rows, QCAT_W), lambda b, pt: (b, 0, 0)),
                  pl.BlockSpec((1, ls, KV_LORA), lambda b, pt: (b, 0, 0)),
                  pl.BlockSpec((1, ls, QK_ROPE), lambda b, pt: (b, 0, 0)),
                  pl.BlockSpec(memory_space=pl.ANY),
                  pl.BlockSpec(memory_space=pl.ANY)],
        out_specs=pl.BlockSpec((1, rows, KV_LORA), lambda b, pt: (b, 0, 0)),
        scratch_shapes=[pltpu.VMEM((2, n_keys, KV_LORA), F32),
                        pltpu.VMEM((2, n_pages, QK_ROPE, page), F32),
                        pltpu.VMEM((n_keys, KV_LORA), BF16),
                        pltpu.VMEM((QCAT_W - KV_LORA, n_keys), BF16),
                        pltpu.VMEM((rows, n_keys), F32),
                        pltpu.SemaphoreType.DMA((2, 2))],
    )
    return pl.pallas_call(
        functools.partial(_mla_decode_kernel, n_pages=n_pages, page=page, ls=ls, chunk=chunk),
        grid_spec=grid_spec,
        out_shape=jax.ShapeDtypeStruct((db, rows, KV_LORA), BF16),
        compiler_params=pltpu.CompilerParams(dimension_semantics=("arbitrary",),
                                             vmem_limit_bytes=VMEM_LIMIT),
        name="mla_decode",
    )(page_table, q, kself, peself, cache_kv, cache_pe_t)


def _mix_ffn_kernel(x_ref, ret_ref, olat_ref, cin_ref, wuv_ref, wout_ref, gffn_ref, wup_ref,
                    cw_ref, cb_ref, wdown_ref, gfin_ref, y_ref, cout_ref,
                    carry_ref, stage_ref, *, stride, carry_end, cw):
    t = pl.program_id(1)
    tm = x_ref.shape[0]
    hal = carry_ref.shape[0]
    d_ff = wdown_ref.shape[0]

    @pl.when(t == 0)
    def _():
        carry_ref[...] = cin_ref[...]

    mla = jnp.concatenate(
        [jnp.dot(olat_ref[:, h * KV_LORA:(h + 1) * KV_LORA], wuv_ref[h],
                 preferred_element_type=F32).astype(BF16) for h in range(MLA_HEADS)], axis=1)
    mixed = jnp.concatenate([ret_ref[...], mla], axis=1)
    h1 = x_ref[...] + jnp.dot(mixed, wout_ref[...], preferred_element_type=F32)
    a2 = _rms(h1, gffn_ref[...]).astype(BF16)

    def conv_half(c0, slot):
        u = jnp.dot(a2, wup_ref[:, c0:c0 + cw], preferred_element_type=F32)
        stage_ref[slot, 0:hal, :] = carry_ref[:, c0:c0 + cw]
        stage_ref[slot, hal:hal + tm, :] = u
        carry_ref[:, c0:c0 + cw] = u[carry_end - hal:carry_end, :]
        um1 = stage_ref[slot, hal - stride:hal - stride + tm, :]
        um2 = stage_ref[slot, hal - 2 * stride:hal - 2 * stride + tm, :]
        return (cb_ref[:, c0:c0 + cw] + cw_ref[0:1, c0:c0 + cw] * um2
                + cw_ref[1:2, c0:c0 + cw] * um1 + cw_ref[2:3, c0:c0 + cw] * u)

    n_slots = stage_ref.shape[0]
    y = None
    for jc in range(d_ff // cw):
        ca = conv_half(jc * cw, (2 * jc) % n_slots)
        cg = conv_half(d_ff + jc * cw, (2 * jc + 1) % n_slots)
        hmid = (cg * jax.nn.sigmoid(cg) * ca).astype(BF16)
        part = jnp.dot(hmid, wdown_ref[jc * cw:(jc + 1) * cw, :], preferred_element_type=F32)
        y = part if y is None else y + part
    h2 = h1 + y
    y_ref[...] = _rms(h2, gfin_ref[...])

    @pl.when(t == pl.num_programs(1) - 1)
    def _():
        cout_ref[0] = carry_ref[...]


def _mix_ffn(x, ret, olat, carry_in, wts, nseq, tm, stride, carry_end, cw):
    n, d = x.shape
    hal = carry_in.shape[0]
    nt = n // (nseq * tm)
    wuv, wout, gffn, wup, convw, convb, wdown, gfin = wts
    row = lambda w: pl.BlockSpec((tm, w), lambda b, t: (b * nt + t, 0))
    return pl.pallas_call(
        functools.partial(_mix_ffn_kernel, stride=stride, carry_end=carry_end, cw=cw),
        grid=(nseq, nt),
        in_specs=[row(d), row(RET_W), row(MLA_HEADS * KV_LORA), _const_spec(carry_in.shape),
                  _const_spec(wuv.shape), _const_spec(wout.shape), _const_spec(gffn.shape),
                  _const_spec(wup.shape), _const_spec(convw.shape), _const_spec(convb.shape),
                  _const_spec(wdown.shape), _const_spec(gfin.shape)],
        out_specs=(row(d), pl.BlockSpec((1, hal, wup.shape[1]), lambda b, t: (b, 0, 0))),
        out_shape=(jax.ShapeDtypeStruct((n, d), F32),
                   jax.ShapeDtypeStruct((nseq, hal, wup.shape[1]), F32)),
        scratch_shapes=[pltpu.VMEM((hal, wup.shape[1]), F32),
                        pltpu.VMEM((4, hal + tm, cw), F32)],
        compiler_params=pltpu.CompilerParams(dimension_semantics=("arbitrary", "arbitrary"),
                                             vmem_limit_bytes=VMEM_LIMIT),
        name="mix_ffn",
    )(x, ret, olat, carry_in, wuv, wout, gffn, wup, convw, convb, wdown, gfin)


def _rope_tables(pos):
    pos = pos.astype(F32)[:, None]

    def cs(dim):
        inv = ROPE_THETA ** (-jnp.arange(0, dim, 2, dtype=F32) / dim)
        ang = pos * inv[None, :]
        return jnp.cos(ang), jnp.sin(ang)

    c, s = cs(RET_DK)
    c128 = jnp.concatenate([c, c], axis=-1)
    s128 = jnp.concatenate([-s, s], axis=-1)
    c, s = cs(QK_ROPE)
    z32 = jnp.zeros_like(s)
    z64 = jnp.zeros((pos.shape[0], LANES - QK_ROPE), F32)
    c64 = jnp.concatenate([c, c, z64], axis=-1)
    s64a = jnp.concatenate([-s, z32, z64], axis=-1)
    s64b = jnp.concatenate([z32, s, z64], axis=-1)
    return c128, s128, c64, s64a, s64b


def kernel(x_prompt, x_sample, cache_kv_latent, cache_k_rope, state_retention, state_ffn_conv,
           page_table, meta_tokens, g_mix, w_in, g_q, w_uq, g_kv, w_uk, w_uv, w_out,
           g_ffn, w_up, conv_w, conv_b, w_down, g_final):
    nb, seq, d = x_prompt.shape
    db, ls, _ = x_sample.shape
    depth = w_in.shape[0]
    assert depth == 1, "single-layer step"
    n_pages = page_table.shape[1]
    page = cache_kv_latent.shape[2]
    past_len = n_pages * page
    d_ff = w_down.shape[1]
    l = 0

    w = w_in[l]
    win = jnp.concatenate([w, jnp.zeros((d, LANES - QK_ROPE), w.dtype)], axis=1).astype(BF16)
    wq = w_uq[l].reshape(Q_LORA, MLA_HEADS, QK_NOPE + QK_ROPE)
    wuq = jnp.concatenate([wq, jnp.zeros((Q_LORA, MLA_HEADS, LANES - QK_ROPE), wq.dtype)],
                          axis=-1).reshape(Q_LORA, MLA_HEADS * 2 * LANES).astype(BF16)
    wuk = jnp.transpose(w_uk[l], (1, 2, 0)).astype(BF16)
    wuv = jnp.transpose(w_uv[l], (1, 0, 2)).astype(BF16)
    proj_w = (g_mix[l][None, :], win, g_q[l][None, :], wuq, g_kv[l][None, :], wuk)
    ffn_w = (wuv, w_out[l].astype(BF16), g_ffn[l][None, :], w_up[l].astype(BF16),
             conv_w[l], conv_b[l][None, :], w_down[l].astype(BF16), g_final[None, :])

    tile = 128
    xm = jnp.concatenate([meta_tokens.astype(F32), jnp.zeros((tile - N_META, d), F32)], axis=0)
    tabs_m = _rope_tables(jnp.arange(tile))
    qr, kr, vr, gr, qcat_m, kvcat_m, ckv_m, kpe_m = _project(xm, tabs_m, proj_w, tile)
    zero_state = jnp.zeros((1, RET_HEADS, RET_DK, RET_DV), F32)
    ret_m, state_m = _retention_chunks(qr, kr, vr, gr, zero_state, 1, tile, N_META)
    ret_m = ret_m.reshape(tile, RET_W)
    row_valid = (jnp.arange(tile) < N_META)[:, None]
    kmeta = jnp.where(row_valid, kvcat_m, jnp.zeros_like(kvcat_m))
    olat_m = _mla_meta(qcat_m, kmeta)
    _, carry_m = _mix_ffn(xm, ret_m, olat_m, jnp.zeros((8, 2 * d_ff), F32), ffn_w,
                          1, tile, 1, N_META, 256)

    tm = 512
    xp = x_prompt.reshape(nb * seq, d)
    tabs_p = _rope_tables(N_META + jnp.arange(seq))
    qr, kr, vr, gr, qcat, kvcat, ckv_p, kpe_p = _project(xp, tabs_p, proj_w, tm)
    ret_p, state_p = _retention_chunks(qr, kr, vr, gr, state_m, nb, 128, 128)
    ret_p = ret_p.reshape(nb * seq, RET_W)
    olat_p = _mla_prompt(qcat, kvcat, kmeta, nb, 256, 512)
    y_p, carry_p = _mix_ffn(xp, ret_p, olat_p, carry_m[0], ffn_w, nb, 512, 1, 512, 256)

    ns = db * ls
    xs = x_sample.reshape(ns, d)
    pos_s = past_len + jnp.arange(ls)
    tabs_s = tuple(jnp.tile(t, (db, 1)) for t in _rope_tables(pos_s))
    qr, kr, vr, gr, qcat_s, _, ckv_s, kpe_s = _project(xs, tabs_s, proj_w, min(ns, 256))
    ret_s, state_s = _retention_decode(qr, kr, vr, gr, state_retention[l], ls, min(ns, 64))
    q_s = jnp.transpose(qcat_s.reshape(MLA_HEADS, db, ls, QCAT_W), (1, 2, 0, 3)).reshape(
        db, ls * MLA_HEADS, QCAT_W)
    cache_pe_t = jnp.swapaxes(cache_k_rope[l], 1, 2)
    olat_s = _mla_decode(page_table, q_s, ckv_s.reshape(db, ls, KV_LORA),
                         kpe_s.reshape(db, ls, QK_ROPE), cache_kv_latent[l], cache_pe_t,
                         min(1024, past_len))
    olat_s = olat_s.reshape(ns, MLA_HEADS * KV_LORA)
    tmaj = lambda a: jnp.transpose(a.reshape(db, ls, a.shape[-1]), (1, 0, 2)).reshape(ns, a.shape[-1])
    carry_s_in = jnp.transpose(state_ffn_conv[l], (1, 0, 2)).reshape((CONV_W - 1) * db, 2 * d_ff)
    y_s, carry_s = _mix_ffn(tmaj(xs), tmaj(ret_s), tmaj(olat_s), carry_s_in, ffn_w,
                            1, ns, db, ns, 256)
    y_s = jnp.transpose(y_s.reshape(ls, db, d), (1, 0, 2))
    conv_s = jnp.transpose(carry_s.reshape(CONV_W - 1, db, 2 * d_ff), (1, 0, 2))

    bc = lambda a: jnp.broadcast_to(a[:N_META][None], (nb, N_META, a.shape[-1]))
    kv_p = jnp.concatenate([bc(ckv_m), ckv_p.reshape(nb, seq, KV_LORA)], axis=1)
    pe_p = jnp.concatenate([bc(kpe_m), kpe_p.reshape(nb, seq, QK_ROPE)], axis=1)
    return (y_p.reshape(nb, seq, d), y_s,
            kv_p[None], pe_p[None],
            state_p[None],
            carry_p[:, 8 - (CONV_W - 1):, :][None],
            ckv_s.reshape(db, ls, KV_LORA)[None], kpe_s.reshape(db, ls, QK_ROPE)[None],
            state_s[None], conv_s[None])
```

```python
import functools
import math

import jax
import jax.numpy as jnp
from jax import lax
from jax.experimental import pallas as pl
from jax.experimental.pallas import tpu as pltpu

F32 = jnp.float32
BF16 = jnp.bfloat16

N_META = 16
RET_HEADS = 4
RET_DK = 128
RET_DV = 128
MLA_HEADS = 4
Q_LORA = 384
KV_LORA = 256
QK_NOPE = 128
QK_ROPE = 64
V_DIM = 128
CONV_W = 3
ROPE_THETA = 10000.0
EPS = 1e-6
SOFTMAX_SCALE = (QK_NOPE + QK_ROPE) ** -0.5
RET_W = RET_HEADS * RET_DV
LANES = 128
ROW_TILE = 16
QCAT_W = KV_LORA + LANES
LOG_GAMMA = tuple(math.log1p(-(2.0 ** (-5.0 - h))) for h in range(RET_HEADS))
VMEM_LIMIT = 56 * 1024 * 1024
NEG_INF = float("-inf")


def _const_spec(shape):
    nd = len(shape)
    return pl.BlockSpec(shape, lambda *_: (0,) * nd, pipeline_mode=pl.Buffered(1))


def _rms(x, g):
    return x * lax.rsqrt(jnp.mean(x * x, axis=-1, keepdims=True) + EPS) * g


def _proj_kernel(x_ref, c128_ref, s128_ref, c64_ref, s64a_ref, s64b_ref,
                 gmix_ref, win_ref, gq_ref, wuq_ref, gkv_ref, wuk_ref,
                 qr_ref, kr_ref, vr_ref, gr_ref, qcat_ref, kvcat_ref, ckv_ref, kpe_ref):
    a = _rms(x_ref[...], gmix_ref[...])
    z = jnp.dot(a.astype(BF16), win_ref[...], preferred_element_type=F32)
    c128 = c128_ref[...]
    s128 = s128_ref[...]
    c64 = c64_ref[...]
    s64a = s64a_ref[...]
    s64b = s64b_ref[...]

    def rope128(v):
        return v * c128 + pltpu.roll(v, 64, 1) * s128

    def rope64(v):
        return v * c64 + pltpu.roll(v, 96, 1) * s64a + pltpu.roll(v, 32, 1) * s64b

    for h in range(RET_HEADS):
        sl = slice(h * RET_DK, (h + 1) * RET_DK)
        qr_ref[:, sl] = rope128(z[:, sl]).astype(BF16)
        ksl = slice(RET_W + h * RET_DK, RET_W + (h + 1) * RET_DK)
        kr_ref[:, sl] = (rope128(z[:, ksl]) * (RET_DK ** -0.5)).astype(BF16)
    vr_ref[...] = z[:, 2 * RET_W:3 * RET_W].astype(BF16)
    gr_ref[...] = z[:, 3 * RET_W:4 * RET_W].astype(BF16)

    o = 4 * RET_W
    cqn = _rms(z[:, o:o + Q_LORA], gq_ref[...])
    q2 = jnp.dot(cqn.astype(BF16), wuq_ref[...], preferred_element_type=F32)
    for h in range(MLA_HEADS):
        b0 = h * 2 * LANES
        nope = q2[:, b0:b0 + QK_NOPE]
        pe = rope64(q2[:, b0 + LANES:b0 + 2 * LANES])
        qlat = jnp.dot(nope.astype(BF16), wuk_ref[h], preferred_element_type=F32)
        qcat_ref[h, :, 0:KV_LORA] = (qlat * SOFTMAX_SCALE).astype(BF16)
        qcat_ref[h, :, KV_LORA:QCAT_W] = (pe * SOFTMAX_SCALE).astype(BF16)

    o += Q_LORA
    ckvn = _rms(z[:, o:o + KV_LORA], gkv_ref[...])
    ckv_ref[...] = ckvn
    o += KV_LORA
    kp = rope64(z[:, o:o + LANES])
    kpe_ref[...] = kp[:, 0:QK_ROPE]
    kvcat_ref[:, 0:KV_LORA] = ckvn.astype(BF16)
    kvcat_ref[:, KV_LORA:QCAT_W] = kp.astype(BF16)


def _project(x, tabs, wts, tm):
    n, d = x.shape
    nt = tabs[0].shape[0] // tm
    gmix, win, gq, wuq, gkv, wuk = wts
    row = lambda w: pl.BlockSpec((tm, w), lambda i: (i, 0))
    tab = pl.BlockSpec((tm, LANES), lambda i: (i % nt, 0))
    out_shapes = (
        jax.ShapeDtypeStruct((n, RET_W), BF16), jax.ShapeDtypeStruct((n, RET_W), BF16),
        jax.ShapeDtypeStruct((n, RET_W), BF16), jax.ShapeDtypeStruct((n, RET_W), BF16),
        jax.ShapeDtypeStruct((MLA_HEADS, n, QCAT_W), BF16),
        jax.ShapeDtypeStruct((n, QCAT_W), BF16),
        jax.ShapeDtypeStruct((n, KV_LORA), F32),
        jax.ShapeDtypeStruct((n, QK_ROPE), F32),
    )
    out_specs = (row(RET_W), row(RET_W), row(RET_W), row(RET_W),
                 pl.BlockSpec((MLA_HEADS, tm, QCAT_W), lambda i: (0, i, 0)),
                 row(QCAT_W), row(KV_LORA), row(QK_ROPE))
    return pl.pallas_call(
        _proj_kernel,
        grid=(n // tm,),
        in_specs=[row(d), tab, tab, tab, tab, tab,
                  _const_spec(gmix.shape), _const_spec(win.shape), _const_spec(gq.shape),
                  _const_spec(wuq.shape), _const_spec(gkv.shape), _const_spec(wuk.shape)],
        out_specs=out_specs,
        out_shape=out_shapes,
        compiler_params=pltpu.CompilerParams(dimension_semantics=("arbitrary",),
                                             vmem_limit_bytes=VMEM_LIMIT),
        name="proj",
    )(x, *tabs, gmix, win, gq, wuq, gkv, wuk)


def _ret_chunk_kernel(q_ref, k_ref, v_ref, g_ref, s0_ref, o_ref, sf_ref, st_ref, *, lv):
    c = pl.program_id(0)
    nb, L, _ = q_ref.shape

    @pl.when(c == 0)
    def _():
        for b in range(nb):
            st_ref[b] = s0_ref[0]

    li = lax.broadcasted_iota(jnp.int32, (L, L), 0)
    mi = lax.broadcasted_iota(jnp.int32, (L, L), 1)
    diff = (li - mi).astype(F32)
    n = lax.broadcasted_iota(jnp.int32, (L, 1), 0).astype(F32)
    for h in range(RET_HEADS):
        lg = LOG_GAMMA[h]
        sl = slice(h * RET_DK, (h + 1) * RET_DK)
        decay = jnp.where(diff >= 0, jnp.exp(lg * jnp.maximum(diff, 0.0)), 0.0)
        qdec = jnp.exp(lg * (n + 1.0))
        kdec = jnp.where(n < lv, jnp.exp(lg * jnp.maximum(lv - 1.0 - n, 0.0)), 0.0)
        for b in range(nb):
            q = q_ref[b, :, sl]
            k = k_ref[b, :, sl]
            v = v_ref[b, :, sl]
            s = lax.dot_general(q, k, (((1,), (1,)), ((), ())), preferred_element_type=F32) * decay
            inner = jnp.dot(s.astype(BF16), v, preferred_element_type=F32)
            state = st_ref[b, h]
            qd = (q.astype(F32) * qdec).astype(BF16)
            cross = jnp.dot(qd, state.astype(BF16), preferred_element_type=F32)
            o = inner + cross
            kd = (k.astype(F32) * kdec).astype(BF16)
            st_ref[b, h] = math.exp(lg * lv) * state + lax.dot_general(
                kd, v, (((0,), (0,)), ((), ())), preferred_element_type=F32)
            of = o * lax.rsqrt(jnp.mean(o * o, axis=-1, keepdims=True) + EPS)
            g = g_ref[b, :, sl].astype(F32)
            o_ref[b, :, sl] = (of * (g * jax.nn.sigmoid(g))).astype(BF16)

    @pl.when(c == pl.num_programs(0) - 1)
    def _():
        sf_ref[...] = st_ref[...]


def _retention_chunks(qr, kr, vr, gr, s0, nb, L, lv):
    n = qr.shape[0]
    t = n // nb
    as3 = lambda a: a.reshape(nb, t, RET_W)
    blk = pl.BlockSpec((nb, L, RET_W), lambda c: (0, c, 0))
    return pl.pallas_call(
        functools.partial(_ret_chunk_kernel, lv=float(lv)),
        grid=(t // L,),
        in_specs=[blk, blk, blk, blk, _const_spec(s0.shape)],
        out_specs=(blk, pl.BlockSpec((nb, RET_HEADS, RET_DK, RET_DV), lambda c: (0, 0, 0, 0))),
        out_shape=(jax.ShapeDtypeStruct((nb, t, RET_W), BF16),
                   jax.ShapeDtypeStruct((nb, RET_HEADS, RET_DK, RET_DV), F32)),
        scratch_shapes=[pltpu.VMEM((nb, RET_HEADS, RET_DK, RET_DV), F32)],
        compiler_params=pltpu.CompilerParams(dimension_semantics=("arbitrary",),
                                             vmem_limit_bytes=VMEM_LIMIT),
        name="ret_chunks",
    )(as3(qr), as3(kr), as3(vr), as3(gr), s0)


def _ret_decode_kernel(q_ref, k_ref, v_ref, g_ref, s_ref, o_ref, sn_ref, *, ls):
    R = q_ref.shape[0]
    per_tile = ROW_TILE // ls
    li = lax.broadcasted_iota(jnp.int32, (R, R), 0)
    mi = lax.broadcasted_iota(jnp.int32, (R, R), 1)
    same = (li // ls) == (mi // ls)
    diff = ((li % ls) - (mi % ls)).astype(F32)
    t_col = (lax.broadcasted_iota(jnp.int32, (R, 1), 0) % ls).astype(F32)
    seq_in_tile = lax.broadcasted_iota(jnp.int32, (ROW_TILE, 1), 0) // ls
    for h in range(RET_HEADS):
        lg = LOG_GAMMA[h]
        sl = slice(h * RET_DK, (h + 1) * RET_DK)
        q = q_ref[:, sl]
        k = k_ref[:, sl]
        v = v_ref[:, sl]
        decay = jnp.where(same & (diff >= 0), jnp.exp(lg * jnp.maximum(diff, 0.0)), 0.0)
        s = lax.dot_general(q, k, (((1,), (1,)), ((), ())), preferred_element_type=F32) * decay
        inner = jnp.dot(s.astype(BF16), v, preferred_element_type=F32)
        qd = q.astype(F32) * jnp.exp(lg * (t_col + 1.0))
        kd = k.astype(F32) * jnp.exp(lg * (ls - 1.0 - t_col))
        sdec = math.exp(lg * ls)
        cross_tiles = []
        for tt in range(R // ROW_TILE):
            rows = slice(tt * ROW_TILE, (tt + 1) * ROW_TILE)
            qd_t = qd[rows].astype(BF16)
            kd_t = kd[rows]
            v_t = v[rows]
            acc = jnp.zeros((ROW_TILE, RET_DV), F32)
            for j in range(per_tile):
                b = tt * per_tile + j
                state = s_ref[b, h]
                mine = seq_in_tile == j
                cr = jnp.dot(qd_t, state.astype(BF16), preferred_element_type=F32)
                acc = jnp.where(mine, cr, acc)
                kdm = jnp.where(mine, kd_t, 0.0).astype(BF16)
                sn_ref[b, h] = sdec * state + lax.dot_general(
                    kdm, v_t, (((0,), (0,)), ((), ())), preferred_element_type=F32)
            cross_tiles.append(acc)
        o = inner + jnp.concatenate(cross_tiles, axis=0)
        of = o * lax.rsqrt(jnp.mean(o * o, axis=-1, keepdims=True) + EPS)
        g = g_ref[:, sl].astype(F32)
        o_ref[:, sl] = (of * (g * jax.nn.sigmoid(g))).astype(BF16)


def _retention_decode(qr, kr, vr, gr, state, ls, rows):
    n = qr.shape[0]
    nseq = rows // ls
    row = pl.BlockSpec((rows, RET_W), lambda i: (i, 0))
    st = pl.BlockSpec((nseq, RET_HEADS, RET_DK, RET_DV), lambda i: (i, 0, 0, 0))
    return pl.pallas_call(
        functools.partial(_ret_decode_kernel, ls=ls),
        grid=(n // rows,),
        in_specs=[row, row, row, row, st],
        out_specs=(row, st),
        out_shape=(jax.ShapeDtypeStruct((n, RET_W), BF16),
                   jax.ShapeDtypeStruct(state.shape, F32)),
        compiler_params=pltpu.CompilerParams(dimension_semantics=("arbitrary",),
                                             vmem_limit_bytes=VMEM_LIMIT),
        name="ret_decode",
    )(qr, kr, vr, gr, state)


def _mla_prompt_kernel(q_ref, k_ref, km_ref, o_ref, vt_ref, vmt_ref, m_ref, l_ref, acc_ref, *, tq, tk):
    assert tk % tq == 0 and tk <= 2 * tq
    i = pl.program_id(1)
    nkm = km_ref.shape[0]
    nq = MLA_HEADS * tq

    @pl.when(i == 0)
    def _():
        vmt_ref[...] = km_ref[:, 0:KV_LORA].T
        for c in range(k_ref.shape[0] // tq):
            vt_ref[:, c * tq:(c + 1) * tq] = k_ref[c * tq:(c + 1) * tq, 0:KV_LORA].T

    q_all = q_ref[...].reshape(nq, QCAT_W)
    groups = [slice(g * nq // 2, (g + 1) * nq // 2) for g in range(2)]

    def scores(kc, grp):
        return lax.dot_general(kc, q_all[grp], (((1,), (1,)), ((), ())),
                               preferred_element_type=F32)

    s = scores(km_ref[...], slice(0, nq))
    s = jnp.where(lax.broadcasted_iota(jnp.int32, (nkm, nq), 0) < N_META, s, NEG_INF)
    m = jnp.max(s, axis=0, keepdims=True)
    p = jnp.exp(s - m)
    m_ref[...] = m
    l_ref[...] = jnp.sum(p, axis=0, keepdims=True)
    acc_ref[...] = jnp.dot(vmt_ref[...], p.astype(BF16), preferred_element_type=F32)

    def update(start, size, mask):
        kc = k_ref[pl.ds(start, size), :]
        vt = vt_ref[:, pl.ds(start, size)]
        ss = [scores(kc, grp) for grp in groups]
        for grp, s in zip(groups, ss):
            if mask is not None:
                s = jnp.where(mask[:, grp], s, NEG_INF)
            m_prev = m_ref[:, grp]
            m_new = jnp.maximum(m_prev, jnp.max(s, axis=0, keepdims=True))
            alpha = jnp.exp(m_prev - m_new)
            p = jnp.exp(s - m_new)
            m_ref[:, grp] = m_new
            l_ref[:, grp] = alpha * l_ref[:, grp] + jnp.sum(p, axis=0, keepdims=True)
            acc_ref[:, grp] = alpha * acc_ref[:, grp] + jnp.dot(
                vt, p.astype(BF16), preferred_element_type=F32)

    n_vis = i * tq

    def body(j, carry):
        update(pl.multiple_of(j * tk, tk), tk, None)
        return carry

    lax.fori_loop(0, n_vis // tk, body, 0)
    if tk > tq:
        @pl.when(n_vis % tk != 0)
        def _():
            update(pl.multiple_of(n_vis - tq, tq), tq, None)

    key = lax.broadcasted_iota(jnp.int32, (tq, nq), 0)
    qry = lax.broadcasted_iota(jnp.int32, (tq, nq), 1) % tq
    update(pl.multiple_of(n_vis, tq), tq, key <= qry)
    out = acc_ref[...] / l_ref[...]
    for h in range(MLA_HEADS):
        o_ref[:, h * KV_LORA:(h + 1) * KV_LORA] = out[:, h * tq:(h + 1) * tq].T.astype(BF16)


def _mla_prompt(qcat, kvcat, kmeta, nb, tq, tk):
    n = kvcat.shape[0]
    t = n // nb
    nq = t // tq
    return pl.pallas_call(
        functools.partial(_mla_prompt_kernel, tq=tq, tk=tk),
        grid=(nb, nq),
        in_specs=[pl.BlockSpec((MLA_HEADS, tq, QCAT_W), lambda b, i: (0, b * nq + i, 0)),
                  pl.BlockSpec((t, QCAT_W), lambda b, i: (b, 0)),
                  _const_spec(kmeta.shape)],
        out_specs=pl.BlockSpec((tq, MLA_HEADS * KV_LORA), lambda b, i: (b * nq + i, 0)),
        out_shape=jax.ShapeDtypeStruct((n, MLA_HEADS * KV_LORA), BF16),
        scratch_shapes=[pltpu.VMEM((KV_LORA, t), BF16), pltpu.VMEM((KV_LORA, kmeta.shape[0]), BF16),
                        pltpu.VMEM((1, MLA_HEADS * tq), F32), pltpu.VMEM((1, MLA_HEADS * tq), F32),
                        pltpu.VMEM((KV_LORA, MLA_HEADS * tq), F32)],
        compiler_params=pltpu.CompilerParams(dimension_semantics=("arbitrary", "arbitrary"),
                                             vmem_limit_bytes=VMEM_LIMIT),
        name="mla_prompt",
    )(qcat, kvcat, kmeta)


def _mla_meta_kernel(q_ref, km_ref, o_ref):
    km = km_ref[...]
    r = q_ref.shape[1]
    row = lax.broadcasted_iota(jnp.int32, (r, km.shape[0]), 0)
    col = lax.broadcasted_iota(jnp.int32, (r, km.shape[0]), 1)
    mask = (col <= row) & (col < N_META)
    for h in range(MLA_HEADS):
        s = lax.dot_general(q_ref[h], km, (((1,), (1,)), ((), ())), preferred_element_type=F32)
        s = jnp.where(mask, s, NEG_INF)
        p = jnp.exp(s - jnp.max(s, axis=-1, keepdims=True))
        acc = jnp.dot(p.astype(BF16), km[:, 0:KV_LORA], preferred_element_type=F32)
        o_ref[:, h * KV_LORA:(h + 1) * KV_LORA] = (
            acc / jnp.sum(p, axis=-1, keepdims=True)).astype(BF16)


def _mla_meta(qcat, kmeta):
    r = qcat.shape[1]
    return pl.pallas_call(
        _mla_meta_kernel,
        out_shape=jax.ShapeDtypeStruct((r, MLA_HEADS * KV_LORA), BF16),
        compiler_params=pltpu.CompilerParams(vmem_limit_bytes=VMEM_LIMIT),
        name="mla_meta",
    )(qcat, kmeta)


def _mla_decode_kernel(pt_ref, q_ref, kself_ref, peself_ref, kv_hbm, pe_hbm, o_ref,
                       kvbuf, pebuf, kb16, pb16, s_ref, sem, *, n_pages, page, ls, chunk):
    b = pl.program_id(0)
    nb = pl.num_programs(0)
    slot = b % 2
    n_keys = n_pages * page
    rope_rows = pb16.shape[0]

    def page_copies(seq, sl):
        out = []
        for j in range(n_pages):
            pid = pt_ref[seq, j]
            out.append(pltpu.make_async_copy(kv_hbm.at[pid], kvbuf.at[sl, pl.ds(j * page, page)],
                                             sem.at[0, sl]))
            out.append(pltpu.make_async_copy(pe_hbm.at[pid], pebuf.at[sl, j], sem.at[1, sl]))
        return out

    @pl.when(b == 0)
    def _():
        pb16[QK_ROPE:rope_rows, :] = jnp.zeros((rope_rows - QK_ROPE, n_keys), BF16)
        for c in page_copies(0, 0):
            c.start()

    @pl.when(b + 1 < nb)
    def _():
        for c in page_copies(b + 1, 1 - slot):
            c.start()

    for c in page_copies(b, slot):
        c.wait()

    q = q_ref[0]
    rows = q.shape[0]
    ql = q[:, 0:KV_LORA]
    qp = q[:, KV_LORA:QCAT_W]
    ppc = chunk // page
    n_chunks = n_keys // chunk
    for c in range(n_chunks):
        ksl = slice(c * chunk, (c + 1) * chunk)
        kb16[ksl, :] = kvbuf[slot, ksl, :].astype(BF16)
        for j in range(ppc):
            jj = c * ppc + j
            pb16[0:QK_ROPE, jj * page:(jj + 1) * page] = pebuf[slot, jj].astype(BF16)
        s_ref[:, ksl] = (
            lax.dot_general(ql, kb16[ksl, :], (((1,), (1,)), ((), ())), preferred_element_type=F32)
            + jnp.dot(qp, pb16[:, ksl], preferred_element_type=F32))

    parts = []
    n_split = 4 if n_chunks % 4 == 0 else 1
    for h in range(n_split):
        hsl = slice(h * n_keys // n_split, (h + 1) * n_keys // n_split)
        s_h = s_ref[:, hsl]
        m_h = jnp.max(s_h, axis=-1, keepdims=True)
        p_h = jnp.exp(s_h - m_h)
        parts.append((m_h, jnp.sum(p_h, axis=-1, keepdims=True),
                      jnp.dot(p_h.astype(BF16), kb16[hsl, :], preferred_element_type=F32)))

    qf = q.astype(F32)
    kvs = kself_ref[0]
    pes = peself_ref[0]
    t_row = lax.broadcasted_iota(jnp.int32, (rows, 1), 0) // MLA_HEADS
    ss = []
    for t2 in range(ls):
        s_t = (jnp.sum(qf[:, 0:KV_LORA] * kvs[t2:t2 + 1, :], axis=-1, keepdims=True)
               + jnp.sum(qf[:, KV_LORA:KV_LORA + QK_ROPE] * pes[t2:t2 + 1, :], axis=-1, keepdims=True))
        ss.append(jnp.where(t_row >= t2, s_t, NEG_INF))
    m_s = ss[0]
    for s_t in ss[1:]:
        m_s = jnp.maximum(m_s, s_t)
    l_s = jnp.zeros((rows, 1), F32)
    acc_s = jnp.zeros((rows, KV_LORA), F32)
    for t2 in range(ls):
        p_t = jnp.exp(ss[t2] - m_s)
        l_s = l_s + p_t
        acc_s = acc_s + p_t * kvs[t2:t2 + 1, :]
    parts.append((m_s, l_s, acc_s))

    m = parts[0][0]
    for m_h, _, _ in parts[1:]:
        m = jnp.maximum(m, m_h)
    l = jnp.zeros((rows, 1), F32)
    acc = jnp.zeros((rows, KV_LORA), F32)
    for m_h, l_h, acc_h in parts:
        w_h = jnp.exp(m_h - m)
        l = l + w_h * l_h
        acc = acc + w_h * acc_h
    o_ref[0] = (acc / l).astype(BF16)


def _mla_decode(page_table, q, kself, peself, cache_kv, cache_pe_t, chunk):
    db, rows, _ = q.shape
    ls = kself.shape[1]
    n_pages = page_table.shape[1]
    page = cache_kv.shape[1]
    n_keys = n_pages * page
    grid_spec = pltpu.PrefetchScalarGridSpec(
        num_scalar_prefetch=1,
        grid=(db,),
        in_specs=[pl.BlockSpec((1, rows, QCAT_W), lambda b, pt: (b, 0, 0)),
                  pl.BlockSpec((1, ls, KV_LORA), lambda b, pt: (b, 0, 0)),
                  pl.BlockSpec((1, ls, QK_ROPE), lambda b, pt: (b, 0, 0)),
                  pl.BlockSpec(memory_space=pl.ANY),
                  pl.BlockSpec(memory_space=pl.ANY)],
        out_specs=pl.BlockSpec((1, rows, KV_LORA), lambda b, pt: (b, 0, 0)),
        scratch_shapes=[pltpu.VMEM((2, n_keys, KV_LORA), F32),
                        pltpu.VMEM((2, n_pages, QK_ROPE, page), F32),
                        pltpu.VMEM((n_keys, KV_LORA), BF16),
                        pltpu.VMEM((QCAT_W - KV_LORA, n_keys), BF16),
                        pltpu.VMEM((rows, n_keys), F32),
                        pltpu.SemaphoreType.DMA((2, 2))],
    )
    return pl.pallas_call(
        functools.partial(_mla_decode_kernel, n_pages=n_pages, page=page, ls=ls, chunk=chunk),
        grid_spec=grid_spec,
        out_shape=jax.ShapeDtypeStruct((db, rows, KV_LORA), BF16),
        compiler_params=pltpu.CompilerParams(dimension_semantics=("arbitrary",),
                                             vmem_limit_bytes=VMEM_LIMIT),
        name="mla_decode",
    )(page_table, q, kself, peself, cache_kv, cache_pe_t)


def _mix_ffn_kernel(x_ref, ret_ref, olat_ref, cin_ref, wuv_ref, wout_ref, gffn_ref, wup_ref,
                    cw_ref, cb_ref, wdown_ref, gfin_ref, y_ref, cout_ref,
                    carry_ref, stage_ref, hmid_ref, *, stride, carry_end, cw):
    t = pl.program_id(1)
    tm = x_ref.shape[0]
    hal = carry_ref.shape[0]
    d_ff = wdown_ref.shape[0]

    @pl.when(t == 0)
    def _():
        carry_ref[...] = cin_ref[...]

    mla = jnp.concatenate(
        [jnp.dot(olat_ref[:, h * KV_LORA:(h + 1) * KV_LORA], wuv_ref[h],
                 preferred_element_type=F32).astype(BF16) for h in range(MLA_HEADS)], axis=1)
    mixed = jnp.concatenate([ret_ref[...], mla], axis=1)
    h1 = x_ref[...] + jnp.dot(mixed, wout_ref[...], preferred_element_type=F32)
    a2 = _rms(h1, gffn_ref[...]).astype(BF16)

    def conv_half(c0, slot):
        u = jnp.dot(a2, wup_ref[:, c0:c0 + cw], preferred_element_type=F32)
        stage_ref[slot, 0:hal, :] = carry_ref[:, c0:c0 + cw]
        stage_ref[slot, hal:hal + tm, :] = u
        carry_ref[:, c0:c0 + cw] = u[carry_end - hal:carry_end, :]
        um1 = stage_ref[slot, hal - stride:hal - stride + tm, :]
        um2 = stage_ref[slot, hal - 2 * stride:hal - 2 * stride + tm, :]
        return (cb_ref[:, c0:c0 + cw] + cw_ref[0:1, c0:c0 + cw] * um2
                + cw_ref[1:2, c0:c0 + cw] * um1 + cw_ref[2:3, c0:c0 + cw] * u)

    n_slots = stage_ref.shape[0]
    for jc in range(d_ff // cw):
        ca = conv_half(jc * cw, (2 * jc) % n_slots)
        cg = conv_half(d_ff + jc * cw, (2 * jc + 1) % n_slots)
        hmid_ref[:, jc * cw:(jc + 1) * cw] = (cg * jax.nn.sigmoid(cg) * ca).astype(BF16)
    h2 = h1 + jnp.dot(hmid_ref[...], wdown_ref[...], preferred_element_type=F32)
    y_ref[...] = _rms(h2, gfin_ref[...])

    @pl.when(t == pl.num_programs(1) - 1)
    def _():
        cout_ref[0] = carry_ref[...]


def _mix_ffn(x, ret, olat, carry_in, wts, nseq, tm, stride, carry_end, cw):
    n, d = x.shape
    hal = carry_in.shape[0]
    nt = n // (nseq * tm)
    wuv, wout, gffn, wup, convw, convb, wdown, gfin = wts
    row = lambda w: pl.BlockSpec((tm, w), lambda b, t: (b * nt + t, 0))
    return pl.pallas_call(
        functools.partial(_mix_ffn_kernel, stride=stride, carry_end=carry_end, cw=cw),
        grid=(nseq, nt),
        in_specs=[row(d), row(RET_W), row(MLA_HEADS * KV_LORA), _const_spec(carry_in.shape),
                  _const_spec(wuv.shape), _const_spec(wout.shape), _const_spec(gffn.shape),
                  _const_spec(wup.shape), _const_spec(convw.shape), _const_spec(convb.shape),
                  _const_spec(wdown.shape), _const_spec(gfin.shape)],
        out_specs=(row(d), pl.BlockSpec((1, hal, wup.shape[1]), lambda b, t: (b, 0, 0))),
        out_shape=(jax.ShapeDtypeStruct((n, d), F32),
                   jax.ShapeDtypeStruct((nseq, hal, wup.shape[1]), F32)),
        scratch_shapes=[pltpu.VMEM((hal, wup.shape[1]), F32),
                        pltpu.VMEM((4, hal + tm, cw), F32),
                        pltpu.VMEM((tm, wdown.shape[0]), BF16)],
        compiler_params=pltpu.CompilerParams(dimension_semantics=("arbitrary", "arbitrary"),
                                             vmem_limit_bytes=VMEM_LIMIT),
        name="mix_ffn",
    )(x, ret, olat, carry_in, wuv, wout, gffn, wup, convw, convb, wdown, gfin)


def _rope_tables(pos):
    pos = pos.astype(F32)[:, None]

    def cs(dim):
        inv = ROPE_THETA ** (-jnp.arange(0, dim, 2, dtype=F32) / dim)
        ang = pos * inv[None, :]
        return jnp.cos(ang), jnp.sin(ang)

    c, s = cs(RET_DK)
    c128 = jnp.concatenate([c, c], axis=-1)
    s128 = jnp.concatenate([-s, s], axis=-1)
    c, s = cs(QK_ROPE)
    z32 = jnp.zeros_like(s)
    z64 = jnp.zeros((pos.shape[0], LANES - QK_ROPE), F32)
    c64 = jnp.concatenate([c, c, z64], axis=-1)
    s64a = jnp.concatenate([-s, z32, z64], axis=-1)
    s64b = jnp.concatenate([z32, s, z64], axis=-1)
    return c128, s128, c64, s64a, s64b


def kernel(x_prompt, x_sample, cache_kv_latent, cache_k_rope, state_retention, state_ffn_conv,
           page_table, meta_tokens, g_mix, w_in, g_q, w_uq, g_kv, w_uk, w_uv, w_out,
           g_ffn, w_up, conv_w, conv_b, w_down, g_final):
    nb, seq, d = x_prompt.shape
    db, ls, _ = x_sample.shape
    depth = w_in.shape[0]
    assert depth == 1, "single-layer step"
    n_pages = page_table.shape[1]
    page = cache_kv_latent.shape[2]
    past_len = n_pages * page
    d_ff = w_down.shape[1]
    l = 0

    w = w_in[l]
    win = jnp.concatenate([w, jnp.zeros((d, LANES - QK_ROPE), w.dtype)], axis=1).astype(BF16)
    wq = w_uq[l].reshape(Q_LORA, MLA_HEADS, QK_NOPE + QK_ROPE)
    wuq = jnp.concatenate([wq, jnp.zeros((Q_LORA, MLA_HEADS, LANES - QK_ROPE), wq.dtype)],
                          axis=-1).reshape(Q_LORA, MLA_HEADS * 2 * LANES).astype(BF16)
    wuk = jnp.transpose(w_uk[l], (1, 2, 0)).astype(BF16)
    wuv = jnp.transpose(w_uv[l], (1, 0, 2)).astype(BF16)
    proj_w = (g_mix[l][None, :], win, g_q[l][None, :], wuq, g_kv[l][None, :], wuk)
    ffn_w = (wuv, w_out[l].astype(BF16), g_ffn[l][None, :], w_up[l].astype(BF16),
             conv_w[l], conv_b[l][None, :], w_down[l].astype(BF16), g_final[None, :])

    tile = 128
    xm = jnp.concatenate([meta_tokens.astype(F32), jnp.zeros((tile - N_META, d), F32)], axis=0)
    tabs_m = _rope_tables(jnp.arange(tile))
    qr, kr, vr, gr, qcat_m, kvcat_m, ckv_m, kpe_m = _project(xm, tabs_m, proj_w, tile)
    zero_state = jnp.zeros((1, RET_HEADS, RET_DK, RET_DV), F32)
    ret_m, state_m = _retention_chunks(qr, kr, vr, gr, zero_state, 1, tile, N_META)
    ret_m = ret_m.reshape(tile, RET_W)
    row_valid = (jnp.arange(tile) < N_META)[:, None]
    kmeta = jnp.where(row_valid, kvcat_m, jnp.zeros_like(kvcat_m))
    olat_m = _mla_meta(qcat_m, kmeta)
    _, carry_m = _mix_ffn(xm, ret_m, olat_m, jnp.zeros((8, 2 * d_ff), F32), ffn_w,
                          1, tile, 1, N_META, 256)

    tm = 512
    xp = x_prompt.reshape(nb * seq, d)
    tabs_p = _rope_tables(N_META + jnp.arange(seq))
    qr, kr, vr, gr, qcat, kvcat, ckv_p, kpe_p = _project(xp, tabs_p, proj_w, tm)
    ret_p, state_p = _retention_chunks(qr, kr, vr, gr, state_m, nb, 128, 128)
    ret_p = ret_p.reshape(nb * seq, RET_W)
    olat_p = _mla_prompt(qcat, kvcat, kmeta, nb, 256, 512)
    y_p, carry_p = _mix_ffn(xp, ret_p, olat_p, carry_m[0], ffn_w, nb, 512, 1, 512, 256)

    ns = db * ls
    xs = x_sample.reshape(ns, d)
    pos_s = past_len + jnp.arange(ls)
    tabs_s = tuple(jnp.tile(t, (db, 1)) for t in _rope_tables(pos_s))
    qr, kr, vr, gr, qcat_s, _, ckv_s, kpe_s = _project(xs, tabs_s, proj_w, min(ns, 256))
    ret_s, state_s = _retention_decode(qr, kr, vr, gr, state_retention[l], ls, min(ns, 64))
    q_s = jnp.transpose(qcat_s.reshape(MLA_HEADS, db, ls, QCAT_W), (1, 2, 0, 3)).reshape(
        db, ls * MLA_HEADS, QCAT_W)
    cache_pe_t = jnp.swapaxes(cache_k_rope[l], 1, 2)
    olat_s = _mla_decode(page_table, q_s, ckv_s.reshape(db, ls, KV_LORA),
                         kpe_s.reshape(db, ls, QK_ROPE), cache_kv_latent[l], cache_pe_t,
                         min(1024, past_len))
    olat_s = olat_s.reshape(ns, MLA_HEADS * KV_LORA)
    tmaj = lambda a: jnp.transpose(a.reshape(db, ls, a.shape[-1]), (1, 0, 2)).reshape(ns, a.shape[-1])
    carry_s_in = jnp.transpose(state_ffn_conv[l], (1, 0, 2)).reshape((CONV_W - 1) * db, 2 * d_ff)
    y_s, carry_s = _mix_ffn(tmaj(xs), tmaj(ret_s), tmaj(olat_s), carry_s_in, ffn_w,
                            1, ns, db, ns, 256)
    y_s = jnp.transpose(y_s.reshape(ls, db, d), (1, 0, 2))
    conv_s = jnp.transpose(carry_s.reshape(CONV_W - 1, db, 2 * d_ff), (1, 0, 2))

    bc = lambda a: jnp.broadcast_to(a[:N_META][None], (nb, N_META, a.shape[-1]))
    kv_p = jnp.concatenate([bc(ckv_m), ckv_p.reshape(nb, seq, KV_LORA)], axis=1)
    pe_p = jnp.concatenate([bc(kpe_m), kpe_p.reshape(nb, seq, QK_ROPE)], axis=1)
    return (y_p.reshape(nb, seq, d), y_s,
            kv_p[None], pe_p[None],
            state_p[None],
            carry_p[:, 8 - (CONV_W - 1):, :][None],
            ckv_s.reshape(db, ls, KV_LORA)[None], kpe_s.reshape(db, ls, QK_ROPE)[None],
            state_s[None], conv_s[None])
```

```python
import functools
import math

import jax
import jax.numpy as jnp
from jax import lax
from jax.experimental import pallas as pl
from jax.experimental.pallas import tpu as pltpu

F32 = jnp.float32
BF16 = jnp.bfloat16

N_META = 16
RET_HEADS = 4
RET_DK = 128
RET_DV = 128
MLA_HEADS = 4
Q_LORA = 384
KV_LORA = 256
QK_NOPE = 128
QK_ROPE = 64
V_DIM = 128
CONV_W = 3
ROPE_THETA = 10000.0
EPS = 1e-6
SOFTMAX_SCALE = (QK_NOPE + QK_ROPE) ** -0.5
RET_W = RET_HEADS * RET_DV
LANES = 128
ROW_TILE = 16
GROUP_LANES = 512
QCAT_W = KV_LORA + LANES
LOG_GAMMA = tuple(math.log1p(-(2.0 ** (-5.0 - h))) for h in range(RET_HEADS))
VMEM_LIMIT = 56 * 1024 * 1024
NEG_INF = float("-inf")


def _const_spec(shape):
    nd = len(shape)
    return pl.BlockSpec(shape, lambda *_: (0,) * nd, pipeline_mode=pl.Buffered(1))


def _rms(x, g):
    return x * lax.rsqrt(jnp.mean(x * x, axis=-1, keepdims=True) + EPS) * g


def _proj_kernel(x_ref, c128_ref, s128_ref, c64_ref, s64a_ref, s64b_ref,
                 gmix_ref, win_ref, gq_ref, wuq_ref, gkv_ref, wuk_ref,
                 qr_ref, kr_ref, vr_ref, gr_ref, qcat_ref, kvcat_ref, ckv_ref, kpe_ref):
    a = _rms(x_ref[...], gmix_ref[...])
    z = jnp.dot(a.astype(BF16), win_ref[...], preferred_element_type=F32)
    c128 = c128_ref[...]
    s128 = s128_ref[...]
    c64 = c64_ref[...]
    s64a = s64a_ref[...]
    s64b = s64b_ref[...]

    def rope128(v):
        return v * c128 + pltpu.roll(v, 64, 1) * s128

    def rope64(v):
        return v * c64 + pltpu.roll(v, 96, 1) * s64a + pltpu.roll(v, 32, 1) * s64b

    for h in range(RET_HEADS):
        sl = slice(h * RET_DK, (h + 1) * RET_DK)
        qr_ref[:, sl] = rope128(z[:, sl]).astype(BF16)
        ksl = slice(RET_W + h * RET_DK, RET_W + (h + 1) * RET_DK)
        kr_ref[:, sl] = (rope128(z[:, ksl]) * (RET_DK ** -0.5)).astype(BF16)
    vr_ref[...] = z[:, 2 * RET_W:3 * RET_W].astype(BF16)
    gr_ref[...] = z[:, 3 * RET_W:4 * RET_W].astype(BF16)

    o = 4 * RET_W
    cqn = _rms(z[:, o:o + Q_LORA], gq_ref[...])
    q2 = jnp.dot(cqn.astype(BF16), wuq_ref[...], preferred_element_type=F32)
    for h in range(MLA_HEADS):
        b0 = h * 2 * LANES
        nope = q2[:, b0:b0 + QK_NOPE]
        pe = rope64(q2[:, b0 + LANES:b0 + 2 * LANES])
        qlat = jnp.dot(nope.astype(BF16), wuk_ref[h], preferred_element_type=F32)
        qcat_ref[h, :, 0:KV_LORA] = (qlat * SOFTMAX_SCALE).astype(BF16)
        qcat_ref[h, :, KV_LORA:QCAT_W] = (pe * SOFTMAX_SCALE).astype(BF16)

    o += Q_LORA
    ckvn = _rms(z[:, o:o + KV_LORA], gkv_ref[...])
    ckv_ref[...] = ckvn
    o += KV_LORA
    kp = rope64(z[:, o:o + LANES])
    kpe_ref[...] = kp[:, 0:QK_ROPE]
    kvcat_ref[:, 0:KV_LORA] = ckvn.astype(BF16)
    kvcat_ref[:, KV_LORA:QCAT_W] = kp.astype(BF16)


def _project(x, tabs, wts, tm):
    n, d = x.shape
    nt = tabs[0].shape[0] // tm
    gmix, win, gq, wuq, gkv, wuk = wts
    row = lambda w: pl.BlockSpec((tm, w), lambda i: (i, 0))
    tab = pl.BlockSpec((tm, LANES), lambda i: (i % nt, 0))
    out_shapes = (
        jax.ShapeDtypeStruct((n, RET_W), BF16), jax.ShapeDtypeStruct((n, RET_W), BF16),
        jax.ShapeDtypeStruct((n, RET_W), BF16), jax.ShapeDtypeStruct((n, RET_W), BF16),
        jax.ShapeDtypeStruct((MLA_HEADS, n, QCAT_W), BF16),
        jax.ShapeDtypeStruct((n, QCAT_W), BF16),
        jax.ShapeDtypeStruct((n, KV_LORA), F32),
        jax.ShapeDtypeStruct((n, QK_ROPE), F32),
    )
    out_specs = (row(RET_W), row(RET_W), row(RET_W), row(RET_W),
                 pl.BlockSpec((MLA_HEADS, tm, QCAT_W), lambda i: (0, i, 0)),
                 row(QCAT_W), row(KV_LORA), row(QK_ROPE))
    return pl.pallas_call(
        _proj_kernel,
        grid=(n // tm,),
        in_specs=[row(d), tab, tab, tab, tab, tab,
                  _const_spec(gmix.shape), _const_spec(win.shape), _const_spec(gq.shape),
                  _const_spec(wuq.shape), _const_spec(gkv.shape), _const_spec(wuk.shape)],
        out_specs=out_specs,
        out_shape=out_shapes,
        compiler_params=pltpu.CompilerParams(dimension_semantics=("arbitrary",),
                                             vmem_limit_bytes=VMEM_LIMIT),
        name="proj",
    )(x, *tabs, gmix, win, gq, wuq, gkv, wuk)


def _ret_chunk_kernel(q_ref, k_ref, v_ref, g_ref, s0_ref, o_ref, sf_ref, st_ref, *, lv):
    c = pl.program_id(0)
    nb, L, _ = q_ref.shape

    @pl.when(c == 0)
    def _():
        for b in range(nb):
            st_ref[b] = s0_ref[0]

    li = lax.broadcasted_iota(jnp.int32, (L, L), 0)
    mi = lax.broadcasted_iota(jnp.int32, (L, L), 1)
    diff = (li - mi).astype(F32)
    n = lax.broadcasted_iota(jnp.int32, (L, 1), 0).astype(F32)
    for h in range(RET_HEADS):
        lg = LOG_GAMMA[h]
        sl = slice(h * RET_DK, (h + 1) * RET_DK)
        decay = jnp.where(diff >= 0, jnp.exp(lg * jnp.maximum(diff, 0.0)), 0.0)
        qdec = jnp.exp(lg * (n + 1.0))
        kdec = jnp.where(n < lv, jnp.exp(lg * jnp.maximum(lv - 1.0 - n, 0.0)), 0.0)
        for b in range(nb):
            q = q_ref[b, :, sl]
            k = k_ref[b, :, sl]
            v = v_ref[b, :, sl]
            s = lax.dot_general(q, k, (((1,), (1,)), ((), ())), preferred_element_type=F32) * decay
            inner = jnp.dot(s.astype(BF16), v, preferred_element_type=F32)
            state = st_ref[b, h]
            qd = (q.astype(F32) * qdec).astype(BF16)
            cross = jnp.dot(qd, state.astype(BF16), preferred_element_type=F32)
            o = inner + cross
            kd = (k.astype(F32) * kdec).astype(BF16)
            st_ref[b, h] = math.exp(lg * lv) * state + lax.dot_general(
                kd, v, (((0,), (0,)), ((), ())), preferred_element_type=F32)
            of = o * lax.rsqrt(jnp.mean(o * o, axis=-1, keepdims=True) + EPS)
            g = g_ref[b, :, sl].astype(F32)
            o_ref[b, :, sl] = (of * (g * jax.nn.sigmoid(g))).astype(BF16)

    @pl.when(c == pl.num_programs(0) - 1)
    def _():
        sf_ref[...] = st_ref[...]


def _retention_chunks(qr, kr, vr, gr, s0, nb, L, lv):
    n = qr.shape[0]
    t = n // nb
    as3 = lambda a: a.reshape(nb, t, RET_W)
    blk = pl.BlockSpec((nb, L, RET_W), lambda c: (0, c, 0))
    return pl.pallas_call(
        functools.partial(_ret_chunk_kernel, lv=float(lv)),
        grid=(t // L,),
        in_specs=[blk, blk, blk, blk, _const_spec(s0.shape)],
        out_specs=(blk, pl.BlockSpec((nb, RET_HEADS, RET_DK, RET_DV), lambda c: (0, 0, 0, 0))),
        out_shape=(jax.ShapeDtypeStruct((nb, t, RET_W), BF16),
                   jax.ShapeDtypeStruct((nb, RET_HEADS, RET_DK, RET_DV), F32)),
        scratch_shapes=[pltpu.VMEM((nb, RET_HEADS, RET_DK, RET_DV), F32)],
        compiler_params=pltpu.CompilerParams(dimension_semantics=("arbitrary",),
                                             vmem_limit_bytes=VMEM_LIMIT),
        name="ret_chunks",
    )(as3(qr), as3(kr), as3(vr), as3(gr), s0)


def _ret_decode_kernel(q_ref, k_ref, v_ref, g_ref, s_ref, o_ref, sn_ref, *, ls):
    R = q_ref.shape[0]
    per_tile = ROW_TILE // ls
    li = lax.broadcasted_iota(jnp.int32, (R, R), 0)
    mi = lax.broadcasted_iota(jnp.int32, (R, R), 1)
    same = (li // ls) == (mi // ls)
    diff = ((li % ls) - (mi % ls)).astype(F32)
    t_col = (lax.broadcasted_iota(jnp.int32, (R, 1), 0) % ls).astype(F32)
    seq_in_tile = lax.broadcasted_iota(jnp.int32, (ROW_TILE, 1), 0) // ls
    for h in range(RET_HEADS):
        lg = LOG_GAMMA[h]
        sl = slice(h * RET_DK, (h + 1) * RET_DK)
        q = q_ref[:, sl]
        k = k_ref[:, sl]
        v = v_ref[:, sl]
        decay = jnp.where(same & (diff >= 0), jnp.exp(lg * jnp.maximum(diff, 0.0)), 0.0)
        s = lax.dot_general(q, k, (((1,), (1,)), ((), ())), preferred_element_type=F32) * decay
        inner = jnp.dot(s.astype(BF16), v, preferred_element_type=F32)
        qd = q.astype(F32) * jnp.exp(lg * (t_col + 1.0))
        kd = k.astype(F32) * jnp.exp(lg * (ls - 1.0 - t_col))
        sdec = math.exp(lg * ls)
        cross_tiles = []
        for tt in range(R // ROW_TILE):
            rows = slice(tt * ROW_TILE, (tt + 1) * ROW_TILE)
            qd_t = qd[rows].astype(BF16)
            kd_t = kd[rows]
            v_t = v[rows]
            acc = jnp.zeros((ROW_TILE, RET_DV), F32)
            for j in range(per_tile):
                b = tt * per_tile + j
                state = s_ref[b, h]
                mine = seq_in_tile == j
                cr = jnp.dot(qd_t, state.astype(BF16), preferred_element_type=F32)
                acc = jnp.where(mine, cr, acc)
                kdm = jnp.where(mine, kd_t, 0.0).astype(BF16)
                sn_ref[b, h] = sdec * state + lax.dot_general(
                    kdm, v_t, (((0,), (0,)), ((), ())), preferred_element_type=F32)
            cross_tiles.append(acc)
        o = inner + jnp.concatenate(cross_tiles, axis=0)
        of = o * lax.rsqrt(jnp.mean(o * o, axis=-1, keepdims=True) + EPS)
        g = g_ref[:, sl].astype(F32)
        o_ref[:, sl] = (of * (g * jax.nn.sigmoid(g))).astype(BF16)


def _retention_decode(qr, kr, vr, gr, state, ls, rows):
    n = qr.shape[0]
    nseq = rows // ls
    row = pl.BlockSpec((rows, RET_W), lambda i: (i, 0))
    st = pl.BlockSpec((nseq, RET_HEADS, RET_DK, RET_DV), lambda i: (i, 0, 0, 0))
    return pl.pallas_call(
        functools.partial(_ret_decode_kernel, ls=ls),
        grid=(n // rows,),
        in_specs=[row, row, row, row, st],
        out_specs=(row, st),
        out_shape=(jax.ShapeDtypeStruct((n, RET_W), BF16),
                   jax.ShapeDtypeStruct(state.shape, F32)),
        compiler_params=pltpu.CompilerParams(dimension_semantics=("arbitrary",),
                                             vmem_limit_bytes=VMEM_LIMIT),
        name="ret_decode",
    )(qr, kr, vr, gr, state)


def _mla_prompt_kernel(q_ref, k_ref, km_ref, o_ref, vt_ref, vmt_ref, m_ref, l_ref, acc_ref, *, tq, tk):
    assert tk % tq == 0 and tk <= 2 * tq
    i = pl.program_id(1)
    nkm = km_ref.shape[0]
    nq = MLA_HEADS * tq

    @pl.when(i == 0)
    def _():
        vmt_ref[...] = km_ref[:, 0:KV_LORA].T
        for c in range(k_ref.shape[0] // tq):
            vt_ref[:, c * tq:(c + 1) * tq] = k_ref[c * tq:(c + 1) * tq, 0:KV_LORA].T

    q_all = q_ref[...].reshape(nq, QCAT_W)
    groups = [slice(g * GROUP_LANES, (g + 1) * GROUP_LANES) for g in range(nq // GROUP_LANES)]

    def scores(kc, grp):
        return lax.dot_general(kc, q_all[grp], (((1,), (1,)), ((), ())),
                               preferred_element_type=F32)

    s = scores(km_ref[...], slice(0, nq))
    s = jnp.where(lax.broadcasted_iota(jnp.int32, (nkm, nq), 0) < N_META, s, NEG_INF)
    m = jnp.max(s, axis=0, keepdims=True)
    p = jnp.exp(s - m)
    m_ref[...] = m
    l_ref[...] = jnp.sum(p, axis=0, keepdims=True)
    acc_ref[...] = jnp.dot(vmt_ref[...], p.astype(BF16), preferred_element_type=F32)

    def update(start, size, mask):
        kc = k_ref[pl.ds(start, size), :]
        vt = vt_ref[:, pl.ds(start, size)]
        ss = [scores(kc, grp) for grp in groups]
        for grp, s in zip(groups, ss):
            if mask is not None:
                s = jnp.where(mask[:, grp], s, NEG_INF)
            m_prev = m_ref[:, grp]
            m_new = jnp.maximum(m_prev, jnp.max(s, axis=0, keepdims=True))
            alpha = jnp.exp(m_prev - m_new)
            p = jnp.exp(s - m_new)
            m_ref[:, grp] = m_new
            l_ref[:, grp] = alpha * l_ref[:, grp] + jnp.sum(p, axis=0, keepdims=True)
            acc_ref[:, grp] = alpha * acc_ref[:, grp] + jnp.dot(
                vt, p.astype(BF16), preferred_element_type=F32)

    n_vis = i * tq

    def body(j, carry):
        update(pl.multiple_of(j * tk, tk), tk, None)
        return carry

    lax.fori_loop(0, n_vis // tk, body, 0)
    if tk > tq:
        @pl.when(n_vis % tk != 0)
        def _():
            update(pl.multiple_of(n_vis - tq, tq), tq, None)

    key = lax.broadcasted_iota(jnp.int32, (tq, nq), 0)
    qry = lax.broadcasted_iota(jnp.int32, (tq, nq), 1) % tq
    update(pl.multiple_of(n_vis, tq), tq, key <= qry)
    out = acc_ref[...] / l_ref[...]
    for h in range(MLA_HEADS):
        o_ref[:, h * KV_LORA:(h + 1) * KV_LORA] = out[:, h * tq:(h + 1) * tq].T.astype(BF16)


def _mla_prompt(qcat, kvcat, kmeta, nb, tq, tk):
    n = kvcat.shape[0]
    t = n // nb
    nq = t // tq
    return pl.pallas_call(
        functools.partial(_mla_prompt_kernel, tq=tq, tk=tk),
        grid=(nb, nq),
        in_specs=[pl.BlockSpec((MLA_HEADS, tq, QCAT_W), lambda b, i: (0, b * nq + i, 0)),
                  pl.BlockSpec((t, QCAT_W), lambda b, i: (b, 0)),
                  _const_spec(kmeta.shape)],
        out_specs=pl.BlockSpec((tq, MLA_HEADS * KV_LORA), lambda b, i: (b * nq + i, 0)),
        out_shape=jax.ShapeDtypeStruct((n, MLA_HEADS * KV_LORA), BF16),
        scratch_shapes=[pltpu.VMEM((KV_LORA, t), BF16), pltpu.VMEM((KV_LORA, kmeta.shape[0]), BF16),
                        pltpu.VMEM((1, MLA_HEADS * tq), F32), pltpu.VMEM((1, MLA_HEADS * tq), F32),
                        pltpu.VMEM((KV_LORA, MLA_HEADS * tq), F32)],
        compiler_params=pltpu.CompilerParams(dimension_semantics=("arbitrary", "arbitrary"),
                                             vmem_limit_bytes=VMEM_LIMIT),
        name="mla_prompt",
    )(qcat, kvcat, kmeta)


def _mla_meta_kernel(q_ref, km_ref, o_ref):
    km = km_ref[...]
    r = q_ref.shape[1]
    row = lax.broadcasted_iota(jnp.int32, (r, km.shape[0]), 0)
    col = lax.broadcasted_iota(jnp.int32, (r, km.shape[0]), 1)
    mask = (col <= row) & (col < N_META)
    for h in range(MLA_HEADS):
        s = lax.dot_general(q_ref[h], km, (((1,), (1,)), ((), ())), preferred_element_type=F32)
        s = jnp.where(mask, s, NEG_INF)
        p = jnp.exp(s - jnp.max(s, axis=-1, keepdims=True))
        acc = jnp.dot(p.astype(BF16), km[:, 0:KV_LORA], preferred_element_type=F32)
        o_ref[:, h * KV_LORA:(h + 1) * KV_LORA] = (
            acc / jnp.sum(p, axis=-1, keepdims=True)).astype(BF16)


def _mla_meta(qcat, kmeta):
    r = qcat.shape[1]
    return pl.pallas_call(
        _mla_meta_kernel,
        out_shape=jax.ShapeDtypeStruct((r, MLA_HEADS * KV_LORA), BF16),
        compiler_params=pltpu.CompilerParams(vmem_limit_bytes=VMEM_LIMIT),
        name="mla_meta",
    )(qcat, kmeta)


def _mla_decode_kernel(pt_ref, q_ref, kself_ref, peself_ref, kv_hbm, pe_hbm, o_ref,
                       kvbuf, pebuf, kb16, pb16, s_ref, sem, *, n_pages, page, ls, chunk):
    b = pl.program_id(0)
    nb = pl.num_programs(0)
    slot = b % 2
    n_keys = n_pages * page
    rope_rows = pb16.shape[0]

    def page_copies(seq, sl):
        out = []
        for j in range(n_pages):
            pid = pt_ref[seq, j]
            out.append(pltpu.make_async_copy(kv_hbm.at[pid], kvbuf.at[sl, pl.ds(j * page, page)],
                                             sem.at[0, sl]))
            out.append(pltpu.make_async_copy(pe_hbm.at[pid], pebuf.at[sl, j], sem.at[1, sl]))
        return out

    @pl.when(b == 0)
    def _():
        pb16[QK_ROPE:rope_rows, :] = jnp.zeros((rope_rows - QK_ROPE, n_keys), BF16)
        for c in page_copies(0, 0):
            c.start()

    @pl.when(b + 1 < nb)
    def _():
        for c in page_copies(b + 1, 1 - slot):
            c.start()

    for c in page_copies(b, slot):
        c.wait()

    q = q_ref[0]
    rows = q.shape[0]
    ql = q[:, 0:KV_LORA]
    qp = q[:, KV_LORA:QCAT_W]
    ppc = chunk // page
    n_chunks = n_keys // chunk
    for c in range(n_chunks):
        ksl = slice(c * chunk, (c + 1) * chunk)
        kb16[ksl, :] = kvbuf[slot, ksl, :].astype(BF16)
        for j in range(ppc):
            jj = c * ppc + j
            pb16[0:QK_ROPE, jj * page:(jj + 1) * page] = pebuf[slot, jj].astype(BF16)
        s_ref[:, ksl] = (
            lax.dot_general(ql, kb16[ksl, :], (((1,), (1,)), ((), ())), preferred_element_type=F32)
            + jnp.dot(qp, pb16[:, ksl], preferred_element_type=F32))

    parts = []
    n_split = 4 if n_chunks % 4 == 0 else 1
    for h in range(n_split):
        hsl = slice(h * n_keys // n_split, (h + 1) * n_keys // n_split)
        s_h = s_ref[:, hsl]
        m_h = jnp.max(s_h, axis=-1, keepdims=True)
        p_h = jnp.exp(s_h - m_h)
        parts.append((m_h, jnp.sum(p_h, axis=-1, keepdims=True),
                      jnp.dot(p_h.astype(BF16), kb16[hsl, :], preferred_element_type=F32)))

    qf = q.astype(F32)
    kvs = kself_ref[0]
    pes = peself_ref[0]
    t_row = lax.broadcasted_iota(jnp.int32, (rows, 1), 0) // MLA_HEADS
    ss = []
    for t2 in range(ls):
        s_t = (jnp.sum(qf[:, 0:KV_LORA] * kvs[t2:t2 + 1, :], axis=-1, keepdims=True)
               + jnp.sum(qf[:, KV_LORA:KV_LORA + QK_ROPE] * pes[t2:t2 + 1, :], axis=-1, keepdims=True))
        ss.append(jnp.where(t_row >= t2, s_t, NEG_INF))
    m_s = ss[0]
    for s_t in ss[1:]:
        m_s = jnp.maximum(m_s, s_t)
    l_s = jnp.zeros((rows, 1), F32)
    acc_s = jnp.zeros((rows, KV_LORA), F32)
    for t2 in range(ls):
        p_t = jnp.exp(ss[t2] - m_s)
        l_s = l_s + p_t
        acc_s = acc_s + p_t * kvs[t2:t2 + 1, :]
    parts.append((m_s, l_s, acc_s))

    m = parts[0][0]
    for m_h, _, _ in parts[1:]:
        m = jnp.maximum(m, m_h)
    l = jnp.zeros((rows, 1), F32)
    acc = jnp.zeros((rows, KV_LORA), F32)
    for m_h, l_h, acc_h in parts:
        w_h = jnp.exp(m_h - m)
        l = l + w_h * l_h
        acc = acc + w_h * acc_h
    o_ref[0] = (acc / l).astype(BF16)


def _mla_decode(page_table, q, kself, peself, cache_kv, cache_pe_t, chunk):
    db, rows, _ = q.shape
    ls = kself.shape[1]
    n_pages = page_table.shape[1]
    page = cache_kv.shape[1]
    n_keys = n_pages * page
    grid_spec = pltpu.PrefetchScalarGridSpec(
        num_scalar_prefetch=1,
        grid=(db,),
        in_specs=[pl.BlockSpec((1, rows, QCAT_W), lambda b, pt: (b, 0, 0)),
                  pl.BlockSpec((1, ls, KV_LORA), lambda b, pt: (b, 0, 0)),
                  pl.BlockSpec((1, ls, QK_ROPE), lambda b, pt: (b, 0, 0)),
                  pl.BlockSpec(memory_space=pl.ANY),
                  pl.BlockSpec(memory_space=pl.ANY)],
        out_specs=pl.BlockSpec((1, rows, KV_LORA), lambda b, pt: (b, 0, 0)),
        scratch_shapes=[pltpu.VMEM((2, n_keys, KV_LORA), F32),
                        pltpu.VMEM((2, n_pages, QK_ROPE, page), F32),
                        pltpu.VMEM((n_keys, KV_LORA), BF16),
                        pltpu.VMEM((QCAT_W - KV_LORA, n_keys), BF16),
                        pltpu.VMEM((rows, n_keys), F32),
                        pltpu.SemaphoreType.DMA((2, 2))],
    )
    return pl.pallas_call(
        functools.partial(_mla_decode_kernel, n_pages=n_pages, page=page, ls=ls, chunk=chunk),
        grid_spec=grid_spec,
        out_shape=jax.ShapeDtypeStruct((db, rows, KV_LORA), BF16),
        compiler_params=pltpu.CompilerParams(dimension_semantics=("arbitrary",),
                                             vmem_limit_bytes=VMEM_LIMIT),
        name="mla_decode",
    )(page_table, q, kself, peself, cache_kv, cache_pe_t)


def _mix_ffn_kernel(x_ref, ret_ref, olat_ref, cin_ref, wuv_ref, wout_ref, gffn_ref, wup_ref,
                    cw_ref, cb_ref, wdown_ref, gfin_ref, y_ref, cout_ref,
                    carry_ref, stage_ref, hmid_ref, *, stride, carry_end, cw):
    t = pl.program_id(1)
    tm = x_ref.shape[0]
    hal = carry_ref.shape[0]
    d_ff = wdown_ref.shape[0]

    @pl.when(t == 0)
    def _():
        carry_ref[...] = cin_ref[...]

    mla = jnp.concatenate(
        [jnp.dot(olat_ref[:, h * KV_LORA:(h + 1) * KV_LORA], wuv_ref[h],
                 preferred_element_type=F32).astype(BF16) for h in range(MLA_HEADS)], axis=1)
    mixed = jnp.concatenate([ret_ref[...], mla], axis=1)
    h1 = x_ref[...] + jnp.dot(mixed, wout_ref[...], preferred_element_type=F32)
    a2 = _rms(h1, gffn_ref[...]).astype(BF16)

    def conv_half(c0, slot):
        u = jnp.dot(a2, wup_ref[:, c0:c0 + cw], preferred_element_type=F32)
        stage_ref[slot, 0:hal, :] = carry_ref[:, c0:c0 + cw]
        stage_ref[slot, hal:hal + tm, :] = u
        carry_ref[:, c0:c0 + cw] = u[carry_end - hal:carry_end, :]
        um1 = stage_ref[slot, hal - stride:hal - stride + tm, :]
        um2 = stage_ref[slot, hal - 2 * stride:hal - 2 * stride + tm, :]
        return (cb_ref[:, c0:c0 + cw] + cw_ref[0:1, c0:c0 + cw] * um2
                + cw_ref[1:2, c0:c0 + cw] * um1 + cw_ref[2:3, c0:c0 + cw] * u)

    n_slots = stage_ref.shape[0]
    for jc in range(d_ff // cw):
        ca = conv_half(jc * cw, (2 * jc) % n_slots)
        cg = conv_half(d_ff + jc * cw, (2 * jc + 1) % n_slots)
        hmid_ref[:, jc * cw:(jc + 1) * cw] = (cg * jax.nn.sigmoid(cg) * ca).astype(BF16)
    h2 = h1 + jnp.dot(hmid_ref[...], wdown_ref[...], preferred_element_type=F32)
    y_ref[...] = _rms(h2, gfin_ref[...])

    @pl.when(t == pl.num_programs(1) - 1)
    def _():
        cout_ref[0] = carry_ref[...]


def _mix_ffn(x, ret, olat, carry_in, wts, nseq, tm, stride, carry_end, cw):
    n, d = x.shape
    hal = carry_in.shape[0]
    nt = n // (nseq * tm)
    wuv, wout, gffn, wup, convw, convb, wdown, gfin = wts
    row = lambda w: pl.BlockSpec((tm, w), lambda b, t: (b * nt + t, 0))
    return pl.pallas_call(
        functools.partial(_mix_ffn_kernel, stride=stride, carry_end=carry_end, cw=cw),
        grid=(nseq, nt),
        in_specs=[row(d), row(RET_W), row(MLA_HEADS * KV_LORA), _const_spec(carry_in.shape),
                  _const_spec(wuv.shape), _const_spec(wout.shape), _const_spec(gffn.shape),
                  _const_spec(wup.shape), _const_spec(convw.shape), _const_spec(convb.shape),
                  _const_spec(wdown.shape), _const_spec(gfin.shape)],
        out_specs=(row(d), pl.BlockSpec((1, hal, wup.shape[1]), lambda b, t: (b, 0, 0))),
        out_shape=(jax.ShapeDtypeStruct((n, d), F32),
                   jax.ShapeDtypeStruct((nseq, hal, wup.shape[1]), F32)),
        scratch_shapes=[pltpu.VMEM((hal, wup.shape[1]), F32),
                        pltpu.VMEM((4, hal + tm, cw), F32),
                        pltpu.VMEM((tm, wdown.shape[0]), BF16)],
        compiler_params=pltpu.CompilerParams(dimension_semantics=("arbitrary", "arbitrary"),
                                             vmem_limit_bytes=VMEM_LIMIT),
        name="mix_ffn",
    )(x, ret, olat, carry_in, wuv, wout, gffn, wup, convw, convb, wdown, gfin)


def _rope_tables(pos):
    pos = pos.astype(F32)[:, None]

    def cs(dim):
        inv = ROPE_THETA ** (-jnp.arange(0, dim, 2, dtype=F32) / dim)
        ang = pos * inv[None, :]
        return jnp.cos(ang), jnp.sin(ang)

    c, s = cs(RET_DK)
    c128 = jnp.concatenate([c, c], axis=-1)
    s128 = jnp.concatenate([-s, s], axis=-1)
    c, s = cs(QK_ROPE)
    z32 = jnp.zeros_like(s)
    z64 = jnp.zeros((pos.shape[0], LANES - QK_ROPE), F32)
    c64 = jnp.concatenate([c, c, z64], axis=-1)
    s64a = jnp.concatenate([-s, z32, z64], axis=-1)
    s64b = jnp.concatenate([z32, s, z64], axis=-1)
    return c128, s128, c64, s64a, s64b


def kernel(x_prompt, x_sample, cache_kv_latent, cache_k_rope, state_retention, state_ffn_conv,
           page_table, meta_tokens, g_mix, w_in, g_q, w_uq, g_kv, w_uk, w_uv, w_out,
           g_ffn, w_up, conv_w, conv_b, w_down, g_final):
    nb, seq, d = x_prompt.shape
    db, ls, _ = x_sample.shape
    depth = w_in.shape[0]
    assert depth == 1, "single-layer step"
    n_pages = page_table.shape[1]
    page = cache_kv_latent.shape[2]
    past_len = n_pages * page
    d_ff = w_down.shape[1]
    l = 0

    w = w_in[l]
    win = jnp.concatenate([w, jnp.zeros((d, LANES - QK_ROPE), w.dtype)], axis=1).astype(BF16)
    wq = w_uq[l].reshape(Q_LORA, MLA_HEADS, QK_NOPE + QK_ROPE)
    wuq = jnp.concatenate([wq, jnp.zeros((Q_LORA, MLA_HEADS, LANES - QK_ROPE), wq.dtype)],
                          axis=-1).reshape(Q_LORA, MLA_HEADS * 2 * LANES).astype(BF16)
    wuk = jnp.transpose(w_uk[l], (1, 2, 0)).astype(BF16)
    wuv = jnp.transpose(w_uv[l], (1, 0, 2)).astype(BF16)
    proj_w = (g_mix[l][None, :], win, g_q[l][None, :], wuq, g_kv[l][None, :], wuk)
    ffn_w = (wuv, w_out[l].astype(BF16), g_ffn[l][None, :], w_up[l].astype(BF16),
             conv_w[l], conv_b[l][None, :], w_down[l].astype(BF16), g_final[None, :])

    tile = 128
    xm = jnp.concatenate([meta_tokens.astype(F32), jnp.zeros((tile - N_META, d), F32)], axis=0)
    tabs_m = _rope_tables(jnp.arange(tile))
    qr, kr, vr, gr, qcat_m, kvcat_m, ckv_m, kpe_m = _project(xm, tabs_m, proj_w, tile)
    zero_state = jnp.zeros((1, RET_HEADS, RET_DK, RET_DV), F32)
    ret_m, state_m = _retention_chunks(qr, kr, vr, gr, zero_state, 1, tile, N_META)
    ret_m = ret_m.reshape(tile, RET_W)
    row_valid = (jnp.arange(tile) < N_META)[:, None]
    kmeta = jnp.where(row_valid, kvcat_m, jnp.zeros_like(kvcat_m))
    olat_m = _mla_meta(qcat_m, kmeta)
    _, carry_m = _mix_ffn(xm, ret_m, olat_m, jnp.zeros((8, 2 * d_ff), F32), ffn_w,
                          1, tile, 1, N_META, 256)

    tm = 512
    xp = x_prompt.reshape(nb * seq, d)
    tabs_p = _rope_tables(N_META + jnp.arange(seq))
    qr, kr, vr, gr, qcat, kvcat, ckv_p, kpe_p = _project(xp, tabs_p, proj_w, tm)
    ret_p, state_p = _retention_chunks(qr, kr, vr, gr, state_m, nb, 128, 128)
    ret_p = ret_p.reshape(nb * seq, RET_W)
    olat_p = _mla_prompt(qcat, kvcat, kmeta, nb, 512, 1024)
    y_p, carry_p = _mix_ffn(xp, ret_p, olat_p, carry_m[0], ffn_w, nb, 512, 1, 512, 256)

    ns = db * ls
    xs = x_sample.reshape(ns, d)
    pos_s = past_len + jnp.arange(ls)
    tabs_s = tuple(jnp.tile(t, (db, 1)) for t in _rope_tables(pos_s))
    qr, kr, vr, gr, qcat_s, _, ckv_s, kpe_s = _project(xs, tabs_s, proj_w, min(ns, 256))
    ret_s, state_s = _retention_decode(qr, kr, vr, gr, state_retention[l], ls, min(ns, 64))
    q_s = jnp.transpose(qcat_s.reshape(MLA_HEADS, db, ls, QCAT_W), (1, 2, 0, 3)).reshape(
        db, ls * MLA_HEADS, QCAT_W)
    cache_pe_t = jnp.swapaxes(cache_k_rope[l], 1, 2)
    olat_s = _mla_decode(page_table, q_s, ckv_s.reshape(db, ls, KV_LORA),
                         kpe_s.reshape(db, ls, QK_ROPE), cache_kv_latent[l], cache_pe_t,
                         min(1024, past_len))
    olat_s = olat_s.reshape(ns, MLA_HEADS * KV_LORA)
    tmaj = lambda a: jnp.transpose(a.reshape(db, ls, a.shape[-1]), (1, 0, 2)).reshape(ns, a.shape[-1])
    carry_s_in = jnp.transpose(state_ffn_conv[l], (1, 0, 2)).reshape((CONV_W - 1) * db, 2 * d_ff)
    y_s, carry_s = _mix_ffn(tmaj(xs), tmaj(ret_s), tmaj(olat_s), carry_s_in, ffn_w,
                            1, ns, db, ns, 256)
    y_s = jnp.transpose(y_s.reshape(ls, db, d), (1, 0, 2))
    conv_s = jnp.transpose(carry_s.reshape(CONV_W - 1, db, 2 * d_ff), (1, 0, 2))

    bc = lambda a: jnp.broadcast_to(a[:N_META][None], (nb, N_META, a.shape[-1]))
    kv_p = jnp.concatenate([bc(ckv_m), ckv_p.reshape(nb, seq, KV_LORA)], axis=1)
    pe_p = jnp.concatenate([bc(kpe_m), kpe_p.reshape(nb, seq, QK_ROPE)], axis=1)
    return (y_p.reshape(nb, seq, d), y_s,
            kv_p[None], pe_p[None],
            state_p[None],
            carry_p[:, 8 - (CONV_W - 1):, :][None],
            ckv_s.reshape(db, ls, KV_LORA)[None], kpe_s.reshape(db, ls, QK_ROPE)[None],
            state_s[None], conv_s[None])
```

```python
import functools
import math

import jax
import jax.numpy as jnp
import numpy as np
from jax import lax
from jax.experimental import pallas as pl
from jax.experimental.pallas import tpu as pltpu

F32 = jnp.float32
BF16 = jnp.bfloat16

N_META = 16
RET_HEADS = 4
RET_DK = 128
RET_DV = 128
MLA_HEADS = 4
Q_LORA = 384
KV_LORA = 256
QK_NOPE = 128
QK_ROPE = 64
V_DIM = 128
CONV_W = 3
ROPE_THETA = 10000.0
EPS = 1e-6
SOFTMAX_SCALE = (QK_NOPE + QK_ROPE) ** -0.5
RET_W = RET_HEADS * RET_DV
LANES = 128
ROW_TILE = 16
GROUP_LANES = 512
QCAT_W = KV_LORA + LANES
LOG_GAMMA = tuple(math.log1p(-(2.0 ** (-5.0 - h))) for h in range(RET_HEADS))
VMEM_LIMIT = 56 * 1024 * 1024
NEG_INF = float("-inf")


def _const_spec(shape):
    nd = len(shape)
    return pl.BlockSpec(shape, lambda *_: (0,) * nd, pipeline_mode=pl.Buffered(1))


def _rms(x, g):
    return x * lax.rsqrt(jnp.mean(x * x, axis=-1, keepdims=True) + EPS) * g


def _proj_kernel(x_ref, c128_ref, s128_ref, c64_ref, s64a_ref, s64b_ref,
                 gmix_ref, win_ref, gq_ref, wuq_ref, gkv_ref, wuk_ref, *rest, tiles_per_seq):
    if tiles_per_seq:
        pre_ckv_ref, pre_kpe_ref = rest[:2]
        rest = rest[2:]
    qr_ref, kr_ref, vr_ref, gr_ref, qcat_ref, kvcat_ref, ckv_ref, kpe_ref = rest
    a = _rms(x_ref[...], gmix_ref[...])
    z = jnp.dot(a.astype(BF16), win_ref[...], preferred_element_type=F32)
    c128 = c128_ref[...]
    s128 = s128_ref[...]
    c64 = c64_ref[...]
    s64a = s64a_ref[...]
    s64b = s64b_ref[...]

    def rope128(v):
        return v * c128 + pltpu.roll(v, 64, 1) * s128

    def rope64(v):
        return v * c64 + pltpu.roll(v, 96, 1) * s64a + pltpu.roll(v, 32, 1) * s64b

    for h in range(RET_HEADS):
        sl = slice(h * RET_DK, (h + 1) * RET_DK)
        qr_ref[:, sl] = rope128(z[:, sl]).astype(BF16)
        ksl = slice(RET_W + h * RET_DK, RET_W + (h + 1) * RET_DK)
        kr_ref[:, sl] = (rope128(z[:, ksl]) * (RET_DK ** -0.5)).astype(BF16)
    vr_ref[...] = z[:, 2 * RET_W:3 * RET_W].astype(BF16)
    gr_ref[...] = z[:, 3 * RET_W:4 * RET_W].astype(BF16)

    o = 4 * RET_W
    cqn = _rms(z[:, o:o + Q_LORA], gq_ref[...])
    q2 = jnp.dot(cqn.astype(BF16), wuq_ref[...], preferred_element_type=F32)
    for h in range(MLA_HEADS):
        b0 = h * 2 * LANES
        nope = q2[:, b0:b0 + QK_NOPE]
        pe = rope64(q2[:, b0 + LANES:b0 + 2 * LANES])
        qlat = jnp.dot(nope.astype(BF16), wuk_ref[h], preferred_element_type=F32)
        qcat_ref[h, :, 0:KV_LORA] = (qlat * SOFTMAX_SCALE).astype(BF16)
        qcat_ref[h, :, KV_LORA:QCAT_W] = (pe * SOFTMAX_SCALE).astype(BF16)

    o += Q_LORA
    ckvn = _rms(z[:, o:o + KV_LORA], gkv_ref[...])
    o += KV_LORA
    kp = rope64(z[:, o:o + LANES])
    kvcat_ref[:, 0:KV_LORA] = ckvn.astype(BF16)
    kvcat_ref[:, KV_LORA:QCAT_W] = kp.astype(BF16)
    if tiles_per_seq:
        tm = x_ref.shape[0]
        n_pre = pre_ckv_ref.shape[0]
        t = pl.program_id(0) % tiles_per_seq

        @pl.when(t == 0)
        def _():
            ckv_ref[0, 0:n_pre, :] = pre_ckv_ref[...]
            kpe_ref[0, 0:n_pre, :] = pre_kpe_ref[...]

        rows = pl.ds(pl.multiple_of(n_pre + t * tm, 8), tm)
        ckv_ref[0, rows, :] = ckvn
        kpe_ref[0, rows, :] = kp[:, 0:QK_ROPE]
    else:
        ckv_ref[...] = ckvn
        kpe_ref[...] = kp[:, 0:QK_ROPE]


def _project(x, tabs, wts, tm, prefix=None):
    n, d = x.shape
    nt = tabs[0].shape[0] // tm
    gmix, win, gq, wuq, gkv, wuk = wts
    row = lambda w: pl.BlockSpec((tm, w), lambda i: (i, 0))
    tab = pl.BlockSpec((tm, LANES), lambda i: (i % nt, 0))
    if prefix is None:
        key_shapes = (jax.ShapeDtypeStruct((n, KV_LORA), F32), jax.ShapeDtypeStruct((n, QK_ROPE), F32))
        key_specs = (row(KV_LORA), row(QK_ROPE))
        extra, extra_specs = (), []
    else:
        t_all = prefix[0].shape[0] + nt * tm
        nseq = n // (nt * tm)
        seq_blk = lambda w: pl.BlockSpec((1, t_all, w), lambda i: (i // nt, 0, 0))
        key_shapes = (jax.ShapeDtypeStruct((nseq, t_all, KV_LORA), F32),
                      jax.ShapeDtypeStruct((nseq, t_all, QK_ROPE), F32))
        key_specs = (seq_blk(KV_LORA), seq_blk(QK_ROPE))
        extra, extra_specs = tuple(prefix), [_const_spec(p.shape) for p in prefix]
    out_shapes = (
        jax.ShapeDtypeStruct((n, RET_W), BF16), jax.ShapeDtypeStruct((n, RET_W), BF16),
        jax.ShapeDtypeStruct((n, RET_W), BF16), jax.ShapeDtypeStruct((n, RET_W), BF16),
        jax.ShapeDtypeStruct((MLA_HEADS, n, QCAT_W), BF16),
        jax.ShapeDtypeStruct((n, QCAT_W), BF16)) + key_shapes
    out_specs = (row(RET_W), row(RET_W), row(RET_W), row(RET_W),
                 pl.BlockSpec((MLA_HEADS, tm, QCAT_W), lambda i: (0, i, 0)),
                 row(QCAT_W)) + key_specs
    return pl.pallas_call(
        functools.partial(_proj_kernel, tiles_per_seq=0 if prefix is None else nt),
        grid=(n // tm,),
        in_specs=[row(d), tab, tab, tab, tab, tab,
                  _const_spec(gmix.shape), _const_spec(win.shape), _const_spec(gq.shape),
                  _const_spec(wuq.shape), _const_spec(gkv.shape), _const_spec(wuk.shape)]
        + extra_specs,
        out_specs=out_specs,
        out_shape=out_shapes,
        compiler_params=pltpu.CompilerParams(dimension_semantics=("arbitrary",),
                                             vmem_limit_bytes=VMEM_LIMIT),
        name="proj",
    )(x, *tabs, gmix, win, gq, wuq, gkv, wuk, *extra)


def _ret_chunk_kernel(q_ref, k_ref, v_ref, g_ref, s0_ref, o_ref, sf_ref, st_ref, *, lv):
    c = pl.program_id(0)
    nb, L, _ = q_ref.shape

    @pl.when(c == 0)
    def _():
        for b in range(nb):
            st_ref[b] = s0_ref[0]

    li = lax.broadcasted_iota(jnp.int32, (L, L), 0)
    mi = lax.broadcasted_iota(jnp.int32, (L, L), 1)
    diff = (li - mi).astype(F32)
    n = lax.broadcasted_iota(jnp.int32, (L, 1), 0).astype(F32)
    for h in range(RET_HEADS):
        lg = LOG_GAMMA[h]
        sl = slice(h * RET_DK, (h + 1) * RET_DK)
        decay = jnp.where(diff >= 0, jnp.exp(lg * jnp.maximum(diff, 0.0)), 0.0)
        qdec = jnp.exp(lg * (n + 1.0))
        kdec = jnp.where(n < lv, jnp.exp(lg * jnp.maximum(lv - 1.0 - n, 0.0)), 0.0)
        for b in range(nb):
            q = q_ref[b, :, sl]
            k = k_ref[b, :, sl]
            v = v_ref[b, :, sl]
            s = lax.dot_general(q, k, (((1,), (1,)), ((), ())), preferred_element_type=F32) * decay
            inner = jnp.dot(s.astype(BF16), v, preferred_element_type=F32)
            state = st_ref[b, h]
            qd = (q.astype(F32) * qdec).astype(BF16)
            cross = jnp.dot(qd, state.astype(BF16), preferred_element_type=F32)
            o = inner + cross
            kd = (k.astype(F32) * kdec).astype(BF16)
            st_ref[b, h] = math.exp(lg * lv) * state + lax.dot_general(
                kd, v, (((0,), (0,)), ((), ())), preferred_element_type=F32)
            of = o * lax.rsqrt(jnp.mean(o * o, axis=-1, keepdims=True) + EPS)
            g = g_ref[b, :, sl].astype(F32)
            o_ref[b, :, sl] = (of * (g * jax.nn.sigmoid(g))).astype(BF16)

    @pl.when(c == pl.num_programs(0) - 1)
    def _():
        sf_ref[...] = st_ref[...]


def _retention_chunks(qr, kr, vr, gr, s0, nb, L, lv):
    n = qr.shape[0]
    t = n // nb
    as3 = lambda a: a.reshape(nb, t, RET_W)
    blk = pl.BlockSpec((nb, L, RET_W), lambda c: (0, c, 0))
    return pl.pallas_call(
        functools.partial(_ret_chunk_kernel, lv=float(lv)),
        grid=(t // L,),
        in_specs=[blk, blk, blk, blk, _const_spec(s0.shape)],
        out_specs=(blk, pl.BlockSpec((nb, RET_HEADS, RET_DK, RET_DV), lambda c: (0, 0, 0, 0))),
        out_shape=(jax.ShapeDtypeStruct((nb, t, RET_W), BF16),
                   jax.ShapeDtypeStruct((nb, RET_HEADS, RET_DK, RET_DV), F32)),
        scratch_shapes=[pltpu.VMEM((nb, RET_HEADS, RET_DK, RET_DV), F32)],
        compiler_params=pltpu.CompilerParams(dimension_semantics=("arbitrary",),
                                             vmem_limit_bytes=VMEM_LIMIT),
        name="ret_chunks",
    )(as3(qr), as3(kr), as3(vr), as3(gr), s0)


def _ret_decode_kernel(q_ref, k_ref, v_ref, g_ref, s_ref, o_ref, sn_ref, *, ls):
    R = q_ref.shape[0]
    per_tile = ROW_TILE // ls
    li = lax.broadcasted_iota(jnp.int32, (R, R), 0)
    mi = lax.broadcasted_iota(jnp.int32, (R, R), 1)
    same = (li // ls) == (mi // ls)
    diff = ((li % ls) - (mi % ls)).astype(F32)
    t_col = (lax.broadcasted_iota(jnp.int32, (R, 1), 0) % ls).astype(F32)
    seq_in_tile = lax.broadcasted_iota(jnp.int32, (ROW_TILE, 1), 0) // ls
    for h in range(RET_HEADS):
        lg = LOG_GAMMA[h]
        sl = slice(h * RET_DK, (h + 1) * RET_DK)
        q = q_ref[:, sl]
        k = k_ref[:, sl]
        v = v_ref[:, sl]
        decay = jnp.where(same & (diff >= 0), jnp.exp(lg * jnp.maximum(diff, 0.0)), 0.0)
        s = lax.dot_general(q, k, (((1,), (1,)), ((), ())), preferred_element_type=F32) * decay
        inner = jnp.dot(s.astype(BF16), v, preferred_element_type=F32)
        qd = q.astype(F32) * jnp.exp(lg * (t_col + 1.0))
        kd = k.astype(F32) * jnp.exp(lg * (ls - 1.0 - t_col))
        sdec = math.exp(lg * ls)
        cross_tiles = []
        for tt in range(R // ROW_TILE):
            rows = slice(tt * ROW_TILE, (tt + 1) * ROW_TILE)
            qd_t = qd[rows].astype(BF16)
            kd_t = kd[rows]
            v_t = v[rows]
            acc = jnp.zeros((ROW_TILE, RET_DV), F32)
            for j in range(per_tile):
                b = tt * per_tile + j
                state = s_ref[b, h]
                mine = seq_in_tile == j
                cr = jnp.dot(qd_t, state.astype(BF16), preferred_element_type=F32)
                acc = jnp.where(mine, cr, acc)
                kdm = jnp.where(mine, kd_t, 0.0).astype(BF16)
                sn_ref[b, h] = sdec * state + lax.dot_general(
                    kdm, v_t, (((0,), (0,)), ((), ())), preferred_element_type=F32)
            cross_tiles.append(acc)
        o = inner + jnp.concatenate(cross_tiles, axis=0)
        of = o * lax.rsqrt(jnp.mean(o * o, axis=-1, keepdims=True) + EPS)
        g = g_ref[:, sl].astype(F32)
        o_ref[:, sl] = (of * (g * jax.nn.sigmoid(g))).astype(BF16)


def _retention_decode(qr, kr, vr, gr, state, ls, rows):
    n = qr.shape[0]
    nseq = rows // ls
    row = pl.BlockSpec((rows, RET_W), lambda i: (i, 0))
    st = pl.BlockSpec((nseq, RET_HEADS, RET_DK, RET_DV), lambda i: (i, 0, 0, 0))
    return pl.pallas_call(
        functools.partial(_ret_decode_kernel, ls=ls),
        grid=(n // rows,),
        in_specs=[row, row, row, row, st],
        out_specs=(row, st),
        out_shape=(jax.ShapeDtypeStruct((n, RET_W), BF16),
                   jax.ShapeDtypeStruct(state.shape, F32)),
        compiler_params=pltpu.CompilerParams(dimension_semantics=("arbitrary",),
                                             vmem_limit_bytes=VMEM_LIMIT),
        name="ret_decode",
    )(qr, kr, vr, gr, state)


def _mla_prompt_kernel(q_ref, k_ref, km_ref, o_ref, vt_ref, vmt_ref, m_ref, l_ref, acc_ref, *, tq, tk):
    assert tk % tq == 0 and tk <= 2 * tq
    i = pl.program_id(1)
    nkm = km_ref.shape[0]
    nq = MLA_HEADS * tq

    @pl.when(i == 0)
    def _():
        vmt_ref[...] = km_ref[:, 0:KV_LORA].T
        for c in range(k_ref.shape[0] // tq):
            vt_ref[:, c * tq:(c + 1) * tq] = k_ref[c * tq:(c + 1) * tq, 0:KV_LORA].T

    q_all = q_ref[...].reshape(nq, QCAT_W)
    groups = [slice(g * GROUP_LANES, (g + 1) * GROUP_LANES) for g in range(nq // GROUP_LANES)]

    def scores(kc, grp):
        return lax.dot_general(kc, q_all[grp], (((1,), (1,)), ((), ())),
                               preferred_element_type=F32)

    s = scores(km_ref[...], slice(0, nq))
    s = jnp.where(lax.broadcasted_iota(jnp.int32, (nkm, nq), 0) < N_META, s, NEG_INF)
    m = jnp.max(s, axis=0, keepdims=True)
    p = jnp.exp(s - m)
    m_ref[...] = m
    l_ref[...] = jnp.sum(p, axis=0, keepdims=True)
    acc_ref[...] = jnp.dot(vmt_ref[...], p.astype(BF16), preferred_element_type=F32)

    def update(start, size, mask):
        kc = k_ref[pl.ds(start, size), :]
        vt = vt_ref[:, pl.ds(start, size)]
        ss = [scores(kc, grp) for grp in groups]
        for grp, s in zip(groups, ss):
            if mask is not None:
                s = jnp.where(mask[:, grp], s, NEG_INF)
            m_prev = m_ref[:, grp]
            m_new = jnp.maximum(m_prev, jnp.max(s, axis=0, keepdims=True))
            alpha = jnp.exp(m_prev - m_new)
            p = jnp.exp(s - m_new)
            m_ref[:, grp] = m_new
            l_ref[:, grp] = alpha * l_ref[:, grp] + jnp.sum(p, axis=0, keepdims=True)
            acc_ref[:, grp] = alpha * acc_ref[:, grp] + jnp.dot(
                vt, p.astype(BF16), preferred_element_type=F32)

    n_vis = i * tq

    def body(j, carry):
        update(pl.multiple_of(j * tk, tk), tk, None)
        return carry

    lax.fori_loop(0, n_vis // tk, body, 0)
    if tk > tq:
        @pl.when(n_vis % tk != 0)
        def _():
            update(pl.multiple_of(n_vis - tq, tq), tq, None)

    key = lax.broadcasted_iota(jnp.int32, (tq, nq), 0)
    qry = lax.broadcasted_iota(jnp.int32, (tq, nq), 1) % tq
    update(pl.multiple_of(n_vis, tq), tq, key <= qry)
    out = acc_ref[...] / l_ref[...]
    for h in range(MLA_HEADS):
        o_ref[:, h * KV_LORA:(h + 1) * KV_LORA] = out[:, h * tq:(h + 1) * tq].T.astype(BF16)


def _mla_prompt(qcat, kvcat, kmeta, nb, tq, tk):
    n = kvcat.shape[0]
    t = n // nb
    nq = t // tq
    return pl.pallas_call(
        functools.partial(_mla_prompt_kernel, tq=tq, tk=tk),
        grid=(nb, nq),
        in_specs=[pl.BlockSpec((MLA_HEADS, tq, QCAT_W), lambda b, i: (0, b * nq + i, 0)),
                  pl.BlockSpec((t, QCAT_W), lambda b, i: (b, 0)),
                  _const_spec(kmeta.shape)],
        out_specs=pl.BlockSpec((tq, MLA_HEADS * KV_LORA), lambda b, i: (b * nq + i, 0)),
        out_shape=jax.ShapeDtypeStruct((n, MLA_HEADS * KV_LORA), BF16),
        scratch_shapes=[pltpu.VMEM((KV_LORA, t), BF16), pltpu.VMEM((KV_LORA, kmeta.shape[0]), BF16),
                        pltpu.VMEM((1, MLA_HEADS * tq), F32), pltpu.VMEM((1, MLA_HEADS * tq), F32),
                        pltpu.VMEM((KV_LORA, MLA_HEADS * tq), F32)],
        compiler_params=pltpu.CompilerParams(dimension_semantics=("arbitrary", "arbitrary"),
                                             vmem_limit_bytes=VMEM_LIMIT),
        name="mla_prompt",
    )(qcat, kvcat, kmeta)


def _mla_meta_kernel(q_ref, km_ref, o_ref):
    km = km_ref[...]
    r = q_ref.shape[1]
    row = lax.broadcasted_iota(jnp.int32, (r, km.shape[0]), 0)
    col = lax.broadcasted_iota(jnp.int32, (r, km.shape[0]), 1)
    mask = (col <= row) & (col < N_META)
    for h in range(MLA_HEADS):
        s = lax.dot_general(q_ref[h], km, (((1,), (1,)), ((), ())), preferred_element_type=F32)
        s = jnp.where(mask, s, NEG_INF)
        p = jnp.exp(s - jnp.max(s, axis=-1, keepdims=True))
        acc = jnp.dot(p.astype(BF16), km[:, 0:KV_LORA], preferred_element_type=F32)
        o_ref[:, h * KV_LORA:(h + 1) * KV_LORA] = (
            acc / jnp.sum(p, axis=-1, keepdims=True)).astype(BF16)


def _mla_meta(qcat, kmeta):
    r = qcat.shape[1]
    return pl.pallas_call(
        _mla_meta_kernel,
        out_shape=jax.ShapeDtypeStruct((r, MLA_HEADS * KV_LORA), BF16),
        compiler_params=pltpu.CompilerParams(vmem_limit_bytes=VMEM_LIMIT),
        name="mla_meta",
    )(qcat, kmeta)


def _mla_decode_kernel(pt_ref, q_ref, kself_ref, peself_ref, kv_hbm, pe_hbm, o_ref,
                       kvbuf, pebuf, kb16, pb16, s_ref, sem, *, n_pages, page, ls, chunk):
    b = pl.program_id(0)
    nb = pl.num_programs(0)
    slot = b % 2
    n_keys = n_pages * page
    rope_rows = pb16.shape[0]

    def page_copies(seq, sl):
        out = []
        for j in range(n_pages):
            pid = pt_ref[seq, j]
            out.append(pltpu.make_async_copy(kv_hbm.at[pid], kvbuf.at[sl, pl.ds(j * page, page)],
                                             sem.at[0, sl]))
            out.append(pltpu.make_async_copy(pe_hbm.at[pid], pebuf.at[sl, j], sem.at[1, sl]))
        return out

    @pl.when(b == 0)
    def _():
        pb16[QK_ROPE:rope_rows, :] = jnp.zeros((rope_rows - QK_ROPE, n_keys), BF16)
        for c in page_copies(0, 0):
            c.start()

    @pl.when(b + 1 < nb)
    def _():
        for c in page_copies(b + 1, 1 - slot):
            c.start()

    for c in page_copies(b, slot):
        c.wait()

    q = q_ref[0]
    rows = q.shape[0]
    ql = q[:, 0:KV_LORA]
    qp = q[:, KV_LORA:QCAT_W]
    ppc = chunk // page
    n_chunks = n_keys // chunk
    for c in range(n_chunks):
        ksl = slice(c * chunk, (c + 1) * chunk)
        kb16[ksl, :] = kvbuf[slot, ksl, :].astype(BF16)
        for j in range(ppc):
            jj = c * ppc + j
            pb16[0:QK_ROPE, jj * page:(jj + 1) * page] = pebuf[slot, jj].astype(BF16)
        s_ref[:, ksl] = (
            lax.dot_general(ql, kb16[ksl, :], (((1,), (1,)), ((), ())), preferred_element_type=F32)
            + jnp.dot(qp, pb16[:, ksl], preferred_element_type=F32))

    parts = []
    n_split = 4 if n_chunks % 4 == 0 else 1
    for h in range(n_split):
        hsl = slice(h * n_keys // n_split, (h + 1) * n_keys // n_split)
        s_h = s_ref[:, hsl]
        m_h = jnp.max(s_h, axis=-1, keepdims=True)
        p_h = jnp.exp(s_h - m_h)
        parts.append((m_h, jnp.sum(p_h, axis=-1, keepdims=True),
                      jnp.dot(p_h.astype(BF16), kb16[hsl, :], preferred_element_type=F32)))

    qf = q.astype(F32)
    kvs = kself_ref[0]
    pes = peself_ref[0]
    t_row = lax.broadcasted_iota(jnp.int32, (rows, 1), 0) // MLA_HEADS
    ss = []
    for t2 in range(ls):
        s_t = (jnp.sum(qf[:, 0:KV_LORA] * kvs[t2:t2 + 1, :], axis=-1, keepdims=True)
               + jnp.sum(qf[:, KV_LORA:KV_LORA + QK_ROPE] * pes[t2:t2 + 1, :], axis=-1, keepdims=True))
        ss.append(jnp.where(t_row >= t2, s_t, NEG_INF))
    m_s = ss[0]
    for s_t in ss[1:]:
        m_s = jnp.maximum(m_s, s_t)
    l_s = jnp.zeros((rows, 1), F32)
    acc_s = jnp.zeros((rows, KV_LORA), F32)
    for t2 in range(ls):
        p_t = jnp.exp(ss[t2] - m_s)
        l_s = l_s + p_t
        acc_s = acc_s + p_t * kvs[t2:t2 + 1, :]
    parts.append((m_s, l_s, acc_s))

    m = parts[0][0]
    for m_h, _, _ in parts[1:]:
        m = jnp.maximum(m, m_h)
    l = jnp.zeros((rows, 1), F32)
    acc = jnp.zeros((rows, KV_LORA), F32)
    for m_h, l_h, acc_h in parts:
        w_h = jnp.exp(m_h - m)
        l = l + w_h * l_h
        acc = acc + w_h * acc_h
    o_ref[0] = (acc / l).astype(BF16)


def _mla_decode(page_table, q, kself, peself, cache_kv, cache_pe_t, chunk):
    db, rows, _ = q.shape
    ls = kself.shape[1]
    n_pages = page_table.shape[1]
    page = cache_kv.shape[1]
    n_keys = n_pages * page
    grid_spec = pltpu.PrefetchScalarGridSpec(
        num_scalar_prefetch=1,
        grid=(db,),
        in_specs=[pl.BlockSpec((1, rows, QCAT_W), lambda b, pt: (b, 0, 0)),
                  pl.BlockSpec((1, ls, KV_LORA), lambda b, pt: (b, 0, 0)),
                  pl.BlockSpec((1, ls, QK_ROPE), lambda b, pt: (b, 0, 0)),
                  pl.BlockSpec(memory_space=pl.ANY),
                  pl.BlockSpec(memory_space=pl.ANY)],
        out_specs=pl.BlockSpec((1, rows, KV_LORA), lambda b, pt: (b, 0, 0)),
        scratch_shapes=[pltpu.VMEM((2, n_keys, KV_LORA), F32),
                        pltpu.VMEM((2, n_pages, QK_ROPE, page), F32),
                        pltpu.VMEM((n_keys, KV_LORA), BF16),
                        pltpu.VMEM((QCAT_W - KV_LORA, n_keys), BF16),
                        pltpu.VMEM((rows, n_keys), F32),
                        pltpu.SemaphoreType.DMA((2, 2))],
    )
    return pl.pallas_call(
        functools.partial(_mla_decode_kernel, n_pages=n_pages, page=page, ls=ls, chunk=chunk),
        grid_spec=grid_spec,
        out_shape=jax.ShapeDtypeStruct((db, rows, KV_LORA), BF16),
        compiler_params=pltpu.CompilerParams(dimension_semantics=("arbitrary",),
                                             vmem_limit_bytes=VMEM_LIMIT),
        name="mla_decode",
    )(page_table, q, kself, peself, cache_kv, cache_pe_t)


def _mix_ffn_kernel(x_ref, ret_ref, olat_ref, cin_ref, wuv_ref, wout_ref, gffn_ref, wup_ref,
                    cw_ref, cb_ref, wdown_ref, gfin_ref, y_ref, cout_ref,
                    carry_ref, stage_ref, hmid_ref, *, stride, carry_end, cw):
    t = pl.program_id(1)
    tm = x_ref.shape[0]
    hal = carry_ref.shape[0]
    d_ff = wdown_ref.shape[0]

    @pl.when(t == 0)
    def _():
        carry_ref[...] = cin_ref[...]

    mla = jnp.concatenate(
        [jnp.dot(olat_ref[:, h * KV_LORA:(h + 1) * KV_LORA], wuv_ref[h],
                 preferred_element_type=F32).astype(BF16) for h in range(MLA_HEADS)], axis=1)
    mixed = jnp.concatenate([ret_ref[...], mla], axis=1)
    h1 = x_ref[...] + jnp.dot(mixed, wout_ref[...], preferred_element_type=F32)
    a2 = _rms(h1, gffn_ref[...]).astype(BF16)

    def conv_half(c0, slot):
        u = jnp.dot(a2, wup_ref[:, c0:c0 + cw], preferred_element_type=F32)
        stage_ref[slot, 0:hal, :] = carry_ref[:, c0:c0 + cw]
        stage_ref[slot, hal:hal + tm, :] = u
        carry_ref[:, c0:c0 + cw] = u[carry_end - hal:carry_end, :]
        um1 = stage_ref[slot, hal - stride:hal - stride + tm, :]
        um2 = stage_ref[slot, hal - 2 * stride:hal - 2 * stride + tm, :]
        return (cb_ref[:, c0:c0 + cw] + cw_ref[0:1, c0:c0 + cw] * um2
                + cw_ref[1:2, c0:c0 + cw] * um1 + cw_ref[2:3, c0:c0 + cw] * u)

    n_slots = stage_ref.shape[0]
    for jc in range(d_ff // cw):
        ca = conv_half(jc * cw, (2 * jc) % n_slots)
        cg = conv_half(d_ff + jc * cw, (2 * jc + 1) % n_slots)
        hmid_ref[:, jc * cw:(jc + 1) * cw] = (cg * jax.nn.sigmoid(cg) * ca).astype(BF16)
    h2 = h1 + jnp.dot(hmid_ref[...], wdown_ref[...], preferred_element_type=F32)
    y_ref[...] = _rms(h2, gfin_ref[...])

    @pl.when(t == pl.num_programs(1) - 1)
    def _():
        cout_ref[0] = carry_ref[...]


def _mix_ffn(x, ret, olat, carry_in, wts, nseq, tm, stride, carry_end, cw):
    n, d = x.shape
    hal = carry_in.shape[0]
    nt = n // (nseq * tm)
    wuv, wout, gffn, wup, convw, convb, wdown, gfin = wts
    row = lambda w: pl.BlockSpec((tm, w), lambda b, t: (b * nt + t, 0))
    return pl.pallas_call(
        functools.partial(_mix_ffn_kernel, stride=stride, carry_end=carry_end, cw=cw),
        grid=(nseq, nt),
        in_specs=[row(d), row(RET_W), row(MLA_HEADS * KV_LORA), _const_spec(carry_in.shape),
                  _const_spec(wuv.shape), _const_spec(wout.shape), _const_spec(gffn.shape),
                  _const_spec(wup.shape), _const_spec(convw.shape), _const_spec(convb.shape),
                  _const_spec(wdown.shape), _const_spec(gfin.shape)],
        out_specs=(row(d), pl.BlockSpec((1, hal, wup.shape[1]), lambda b, t: (b, 0, 0))),
        out_shape=(jax.ShapeDtypeStruct((n, d), F32),
                   jax.ShapeDtypeStruct((nseq, hal, wup.shape[1]), F32)),
        scratch_shapes=[pltpu.VMEM((hal, wup.shape[1]), F32),
                        pltpu.VMEM((4, hal + tm, cw), F32),
                        pltpu.VMEM((tm, wdown.shape[0]), BF16)],
        compiler_params=pltpu.CompilerParams(dimension_semantics=("arbitrary", "arbitrary"),
                                             vmem_limit_bytes=VMEM_LIMIT),
        name="mix_ffn",
    )(x, ret, olat, carry_in, wuv, wout, gffn, wup, convw, convb, wdown, gfin)


def _rope_tables(pos):
    pos = np.asarray(pos, np.float64)[:, None]

    def cs(dim):
        inv = ROPE_THETA ** (-np.arange(0, dim, 2, dtype=np.float64) / dim)
        ang = pos * inv[None, :]
        return np.cos(ang), np.sin(ang)

    c, s = cs(RET_DK)
    c128 = np.concatenate([c, c], axis=-1)
    s128 = np.concatenate([-s, s], axis=-1)
    c, s = cs(QK_ROPE)
    z32 = np.zeros_like(s)
    z64 = np.zeros((pos.shape[0], LANES - QK_ROPE))
    c64 = np.concatenate([c, c, z64], axis=-1)
    s64a = np.concatenate([-s, z32, z64], axis=-1)
    s64b = np.concatenate([z32, s, z64], axis=-1)
    return tuple(np.asarray(t, np.float32) for t in (c128, s128, c64, s64a, s64b))


def kernel(x_prompt, x_sample, cache_kv_latent, cache_k_rope, state_retention, state_ffn_conv,
           page_table, meta_tokens, g_mix, w_in, g_q, w_uq, g_kv, w_uk, w_uv, w_out,
           g_ffn, w_up, conv_w, conv_b, w_down, g_final):
    nb, seq, d = x_prompt.shape
    db, ls, _ = x_sample.shape
    depth = w_in.shape[0]
    assert depth == 1, "single-layer step"
    n_pages = page_table.shape[1]
    page = cache_kv_latent.shape[2]
    past_len = n_pages * page
    d_ff = w_down.shape[1]
    l = 0

    w = w_in[l]
    win = jnp.concatenate([w, jnp.zeros((d, LANES - QK_ROPE), w.dtype)], axis=1).astype(BF16)
    wq = w_uq[l].reshape(Q_LORA, MLA_HEADS, QK_NOPE + QK_ROPE)
    wuq = jnp.concatenate([wq, jnp.zeros((Q_LORA, MLA_HEADS, LANES - QK_ROPE), wq.dtype)],
                          axis=-1).reshape(Q_LORA, MLA_HEADS * 2 * LANES).astype(BF16)
    wuk = jnp.transpose(w_uk[l], (1, 2, 0)).astype(BF16)
    wuv = jnp.transpose(w_uv[l], (1, 0, 2)).astype(BF16)
    proj_w = (g_mix[l][None, :], win, g_q[l][None, :], wuq, g_kv[l][None, :], wuk)
    ffn_w = (wuv, w_out[l].astype(BF16), g_ffn[l][None, :], w_up[l].astype(BF16),
             conv_w[l], conv_b[l][None, :], w_down[l].astype(BF16), g_final[None, :])

    tile = 128
    xm = jnp.concatenate([meta_tokens.astype(F32), jnp.zeros((tile - N_META, d), F32)], axis=0)
    tabs_m = _rope_tables(np.arange(tile))
    qr, kr, vr, gr, qcat_m, kvcat_m, ckv_m, kpe_m = _project(xm, tabs_m, proj_w, tile)
    zero_state = jnp.zeros((1, RET_HEADS, RET_DK, RET_DV), F32)
    ret_m, state_m = _retention_chunks(qr, kr, vr, gr, zero_state, 1, tile, N_META)
    ret_m = ret_m.reshape(tile, RET_W)
    row_valid = (jnp.arange(tile) < N_META)[:, None]
    kmeta = jnp.where(row_valid, kvcat_m, jnp.zeros_like(kvcat_m))
    olat_m = _mla_meta(qcat_m, kmeta)
    _, carry_m = _mix_ffn(xm, ret_m, olat_m, jnp.zeros((8, 2 * d_ff), F32), ffn_w,
                          1, tile, 1, N_META, 256)

    tm = 512
    xp = x_prompt.reshape(nb * seq, d)
    tabs_p = _rope_tables(N_META + np.arange(seq))
    qr, kr, vr, gr, qcat, kvcat, kv_p, pe_p = _project(xp, tabs_p, proj_w, tm,
                                                      prefix=(ckv_m[:N_META], kpe_m[:N_META]))
    ret_p, state_p = _retention_chunks(qr, kr, vr, gr, state_m, nb, 128, 128)
    ret_p = ret_p.reshape(nb * seq, RET_W)
    olat_p = _mla_prompt(qcat, kvcat, kmeta, nb, 512, 1024)
    y_p, carry_p = _mix_ffn(xp, ret_p, olat_p, carry_m[0], ffn_w, nb, 512, 1, 512, 256)

    ns = db * ls
    xs = x_sample.reshape(ns, d)
    pos_s = past_len + np.arange(ls)
    tabs_s = tuple(np.tile(t, (db, 1)) for t in _rope_tables(pos_s))
    qr, kr, vr, gr, qcat_s, _, ckv_s, kpe_s = _project(xs, tabs_s, proj_w, min(ns, 256))
    ret_s, state_s = _retention_decode(qr, kr, vr, gr, state_retention[l], ls, min(ns, 64))
    q_s = jnp.transpose(qcat_s.reshape(MLA_HEADS, db, ls, QCAT_W), (1, 2, 0, 3)).reshape(
        db, ls * MLA_HEADS, QCAT_W)
    cache_pe_t = jnp.swapaxes(cache_k_rope[l], 1, 2)
    olat_s = _mla_decode(page_table, q_s, ckv_s.reshape(db, ls, KV_LORA),
                         kpe_s.reshape(db, ls, QK_ROPE), cache_kv_latent[l], cache_pe_t,
                         min(1024, past_len))
    olat_s = olat_s.reshape(ns, MLA_HEADS * KV_LORA)
    tmaj = lambda a: jnp.transpose(a.reshape(db, ls, a.shape[-1]), (1, 0, 2)).reshape(ns, a.shape[-1])
    carry_s_in = jnp.transpose(state_ffn_conv[l], (1, 0, 2)).reshape((CONV_W - 1) * db, 2 * d_ff)
    y_s, carry_s = _mix_ffn(tmaj(xs), tmaj(ret_s), tmaj(olat_s), carry_s_in, ffn_w,
                            1, ns, db, ns, 256)
    y_s = jnp.transpose(y_s.reshape(ls, db, d), (1, 0, 2))
    conv_s = jnp.transpose(carry_s.reshape(CONV_W - 1, db, 2 * d_ff), (1, 0, 2))

    return (y_p.reshape(nb, seq, d), y_s,
            kv_p[None], pe_p[None],
            state_p[None],
            carry_p[:, 8 - (CONV_W - 1):, :][None],
            ckv_s.reshape(db, ls, KV_LORA)[None], kpe_s.reshape(db, ls, QK_ROPE)[None],
            state_s[None], conv_s[None])
```

```python
import functools
import math

import jax
import jax.numpy as jnp
import numpy as np
from jax import lax
from jax.experimental import pallas as pl
from jax.experimental.pallas import tpu as pltpu

F32 = jnp.float32
BF16 = jnp.bfloat16

N_META = 16
RET_HEADS = 4
RET_DK = 128
RET_DV = 128
MLA_HEADS = 4
Q_LORA = 384
KV_LORA = 256
QK_NOPE = 128
QK_ROPE = 64
V_DIM = 128
CONV_W = 3
ROPE_THETA = 10000.0
EPS = 1e-6
SOFTMAX_SCALE = (QK_NOPE + QK_ROPE) ** -0.5
RET_W = RET_HEADS * RET_DV
LANES = 128
ROW_TILE = 16
GROUP_LANES = 512
RET_CHUNK = 128
QCAT_W = KV_LORA + LANES
LOG_GAMMA = tuple(math.log1p(-(2.0 ** (-5.0 - h))) for h in range(RET_HEADS))
VMEM_LIMIT = 56 * 1024 * 1024
NEG_INF = float("-inf")


def _const_spec(shape):
    nd = len(shape)
    return pl.BlockSpec(shape, lambda *_: (0,) * nd, pipeline_mode=pl.Buffered(1))


def _rms(x, g):
    return x * lax.rsqrt(jnp.mean(x * x, axis=-1, keepdims=True) + EPS) * g


def _proj_kernel(x_ref, c128_ref, s128_ref, c64_ref, s64a_ref, s64b_ref,
                 gmix_ref, win_ref, gq_ref, wuq_ref, gkv_ref, wuk_ref, *rest, tiles_per_seq):
    if tiles_per_seq:
        pre_ckv_ref, pre_kpe_ref = rest[:2]
        rest = rest[2:]
    qr_ref, kr_ref, vr_ref, gr_ref, qcat_ref, kvcat_ref, ckv_ref, kpe_ref = rest
    a = _rms(x_ref[...], gmix_ref[...])
    z = jnp.dot(a.astype(BF16), win_ref[...], preferred_element_type=F32)
    c128 = c128_ref[...]
    s128 = s128_ref[...]
    c64 = c64_ref[...]
    s64a = s64a_ref[...]
    s64b = s64b_ref[...]

    def rope128(v):
        return v * c128 + pltpu.roll(v, 64, 1) * s128

    def rope64(v):
        return v * c64 + pltpu.roll(v, 96, 1) * s64a + pltpu.roll(v, 32, 1) * s64b

    for h in range(RET_HEADS):
        sl = slice(h * RET_DK, (h + 1) * RET_DK)
        qr_ref[:, sl] = rope128(z[:, sl]).astype(BF16)
        ksl = slice(RET_W + h * RET_DK, RET_W + (h + 1) * RET_DK)
        kr_ref[:, sl] = (rope128(z[:, ksl]) * (RET_DK ** -0.5)).astype(BF16)
    vr_ref[...] = z[:, 2 * RET_W:3 * RET_W].astype(BF16)
    gr_ref[...] = z[:, 3 * RET_W:4 * RET_W].astype(BF16)

    o = 4 * RET_W
    cqn = _rms(z[:, o:o + Q_LORA], gq_ref[...])
    q2 = jnp.dot(cqn.astype(BF16), wuq_ref[...], preferred_element_type=F32)
    for h in range(MLA_HEADS):
        b0 = h * 2 * LANES
        nope = q2[:, b0:b0 + QK_NOPE]
        pe = rope64(q2[:, b0 + LANES:b0 + 2 * LANES])
        qlat = jnp.dot(nope.astype(BF16), wuk_ref[h], preferred_element_type=F32)
        qcat_ref[h, :, 0:KV_LORA] = (qlat * SOFTMAX_SCALE).astype(BF16)
        qcat_ref[h, :, KV_LORA:QCAT_W] = (pe * SOFTMAX_SCALE).astype(BF16)

    o += Q_LORA
    ckvn = _rms(z[:, o:o + KV_LORA], gkv_ref[...])
    o += KV_LORA
    kp = rope64(z[:, o:o + LANES])
    kvcat_ref[:, 0:KV_LORA] = ckvn.astype(BF16)
    kvcat_ref[:, KV_LORA:QCAT_W] = kp.astype(BF16)
    if tiles_per_seq:
        tm = x_ref.shape[0]
        n_pre = pre_ckv_ref.shape[0]
        t = pl.program_id(0) % tiles_per_seq

        @pl.when(t == 0)
        def _():
            ckv_ref[0, 0:n_pre, :] = pre_ckv_ref[...]
            kpe_ref[0, 0:n_pre, :] = pre_kpe_ref[...]

        rows = pl.ds(pl.multiple_of(n_pre + t * tm, 8), tm)
        ckv_ref[0, rows, :] = ckvn
        kpe_ref[0, rows, :] = kp[:, 0:QK_ROPE]
    else:
        ckv_ref[...] = ckvn
        kpe_ref[...] = kp[:, 0:QK_ROPE]


def _project(x, tabs, wts, tm, prefix=None):
    n, d = x.shape
    nt = tabs[0].shape[0] // tm
    gmix, win, gq, wuq, gkv, wuk = wts
    row = lambda w: pl.BlockSpec((tm, w), lambda i: (i, 0))
    tab = pl.BlockSpec((tm, LANES), lambda i: (i % nt, 0))
    if prefix is None:
        key_shapes = (jax.ShapeDtypeStruct((n, KV_LORA), F32), jax.ShapeDtypeStruct((n, QK_ROPE), F32))
        key_specs = (row(KV_LORA), row(QK_ROPE))
        extra, extra_specs = (), []
    else:
        t_all = prefix[0].shape[0] + nt * tm
        nseq = n // (nt * tm)
        seq_blk = lambda w: pl.BlockSpec((1, t_all, w), lambda i: (i // nt, 0, 0))
        key_shapes = (jax.ShapeDtypeStruct((nseq, t_all, KV_LORA), F32),
                      jax.ShapeDtypeStruct((nseq, t_all, QK_ROPE), F32))
        key_specs = (seq_blk(KV_LORA), seq_blk(QK_ROPE))
        extra, extra_specs = tuple(prefix), [_const_spec(p.shape) for p in prefix]
    out_shapes = (
        jax.ShapeDtypeStruct((n, RET_W), BF16), jax.ShapeDtypeStruct((n, RET_W), BF16),
        jax.ShapeDtypeStruct((n, RET_W), BF16), jax.ShapeDtypeStruct((n, RET_W), BF16),
        jax.ShapeDtypeStruct((MLA_HEADS, n, QCAT_W), BF16),
        jax.ShapeDtypeStruct((n, QCAT_W), BF16)) + key_shapes
    out_specs = (row(RET_W), row(RET_W), row(RET_W), row(RET_W),
                 pl.BlockSpec((MLA_HEADS, tm, QCAT_W), lambda i: (0, i, 0)),
                 row(QCAT_W)) + key_specs
    return pl.pallas_call(
        functools.partial(_proj_kernel, tiles_per_seq=0 if prefix is None else nt),
        grid=(n // tm,),
        in_specs=[row(d), tab, tab, tab, tab, tab,
                  _const_spec(gmix.shape), _const_spec(win.shape), _const_spec(gq.shape),
                  _const_spec(wuq.shape), _const_spec(gkv.shape), _const_spec(wuk.shape)]
        + extra_specs,
        out_specs=out_specs,
        out_shape=out_shapes,
        compiler_params=pltpu.CompilerParams(dimension_semantics=("arbitrary",),
                                             vmem_limit_bytes=VMEM_LIMIT),
        name="proj",
    )(x, *tabs, gmix, win, gq, wuq, gkv, wuk, *extra)


def _ret_chunk(q_ref, k_ref, v_ref, g_ref, o_ref, st_ref, lv):
    L = q_ref.shape[0]
    li = lax.broadcasted_iota(jnp.int32, (L, L), 0)
    mi = lax.broadcasted_iota(jnp.int32, (L, L), 1)
    diff = (li - mi).astype(F32)
    n = lax.broadcasted_iota(jnp.int32, (L, 1), 0).astype(F32)
    for h in range(RET_HEADS):
        lg = LOG_GAMMA[h]
        sl = slice(h * RET_DK, (h + 1) * RET_DK)
        decay = jnp.where(diff >= 0, jnp.exp(lg * jnp.maximum(diff, 0.0)), 0.0)
        qdec = jnp.exp(lg * (n + 1.0))
        kdec = jnp.where(n < lv, jnp.exp(lg * jnp.maximum(lv - 1.0 - n, 0.0)), 0.0)
        q = q_ref[:, sl]
        k = k_ref[:, sl]
        v = v_ref[:, sl]
        s = lax.dot_general(q, k, (((1,), (1,)), ((), ())), preferred_element_type=F32) * decay
        inner = jnp.dot(s.astype(BF16), v, preferred_element_type=F32)
        state = st_ref[h]
        qd = (q.astype(F32) * qdec).astype(BF16)
        cross = jnp.dot(qd, state.astype(BF16), preferred_element_type=F32)
        o = inner + cross
        kd = (k.astype(F32) * kdec).astype(BF16)
        st_ref[h] = math.exp(lg * lv) * state + lax.dot_general(
            kd, v, (((0,), (0,)), ((), ())), preferred_element_type=F32)
        of = o * lax.rsqrt(jnp.mean(o * o, axis=-1, keepdims=True) + EPS)
        g = g_ref[:, sl].astype(F32)
        o_ref[:, sl] = (of * (g * jax.nn.sigmoid(g))).astype(BF16)


def _ret_chunk_kernel(q_ref, k_ref, v_ref, g_ref, s0_ref, o_ref, sf_ref, st_ref, *, lv):
    c = pl.program_id(0)
    nb = q_ref.shape[0]

    @pl.when(c == 0)
    def _():
        for b in range(nb):
            st_ref[b] = s0_ref[0]

    for b in range(nb):
        _ret_chunk(q_ref.at[b], k_ref.at[b], v_ref.at[b], g_ref.at[b], o_ref.at[b],
                   st_ref.at[b], lv)

    @pl.when(c == pl.num_programs(0) - 1)
    def _():
        sf_ref[...] = st_ref[...]


def _retention_chunks(qr, kr, vr, gr, s0, nb, L, lv):
    n = qr.shape[0]
    t = n // nb
    as3 = lambda a: a.reshape(nb, t, RET_W)
    blk = pl.BlockSpec((nb, L, RET_W), lambda c: (0, c, 0))
    return pl.pallas_call(
        functools.partial(_ret_chunk_kernel, lv=float(lv)),
        grid=(t // L,),
        in_specs=[blk, blk, blk, blk, _const_spec(s0.shape)],
        out_specs=(blk, pl.BlockSpec((nb, RET_HEADS, RET_DK, RET_DV), lambda c: (0, 0, 0, 0))),
        out_shape=(jax.ShapeDtypeStruct((nb, t, RET_W), BF16),
                   jax.ShapeDtypeStruct((nb, RET_HEADS, RET_DK, RET_DV), F32)),
        scratch_shapes=[pltpu.VMEM((nb, RET_HEADS, RET_DK, RET_DV), F32)],
        compiler_params=pltpu.CompilerParams(dimension_semantics=("arbitrary",),
                                             vmem_limit_bytes=VMEM_LIMIT),
        name="ret_chunks",
    )(as3(qr), as3(kr), as3(vr), as3(gr), s0)


def _ret_decode_kernel(q_ref, k_ref, v_ref, g_ref, s_ref, o_ref, sn_ref, *, ls):
    R = q_ref.shape[0]
    per_tile = ROW_TILE // ls
    li = lax.broadcasted_iota(jnp.int32, (R, R), 0)
    mi = lax.broadcasted_iota(jnp.int32, (R, R), 1)
    same = (li // ls) == (mi // ls)
    diff = ((li % ls) - (mi % ls)).astype(F32)
    t_col = (lax.broadcasted_iota(jnp.int32, (R, 1), 0) % ls).astype(F32)
    seq_in_tile = lax.broadcasted_iota(jnp.int32, (ROW_TILE, 1), 0) // ls
    for h in range(RET_HEADS):
        lg = LOG_GAMMA[h]
        sl = slice(h * RET_DK, (h + 1) * RET_DK)
        q = q_ref[:, sl]
        k = k_ref[:, sl]
        v = v_ref[:, sl]
        decay = jnp.where(same & (diff >= 0), jnp.exp(lg * jnp.maximum(diff, 0.0)), 0.0)
        s = lax.dot_general(q, k, (((1,), (1,)), ((), ())), preferred_element_type=F32) * decay
        inner = jnp.dot(s.astype(BF16), v, preferred_element_type=F32)
        qd = q.astype(F32) * jnp.exp(lg * (t_col + 1.0))
        kd = k.astype(F32) * jnp.exp(lg * (ls - 1.0 - t_col))
        sdec = math.exp(lg * ls)
        cross_tiles = []
        for tt in range(R // ROW_TILE):
            rows = slice(tt * ROW_TILE, (tt + 1) * ROW_TILE)
            qd_t = qd[rows].astype(BF16)
            kd_t = kd[rows]
            v_t = v[rows]
            acc = jnp.zeros((ROW_TILE, RET_DV), F32)
            for j in range(per_tile):
                b = tt * per_tile + j
                state = s_ref[b, h]
                mine = seq_in_tile == j
                cr = jnp.dot(qd_t, state.astype(BF16), preferred_element_type=F32)
                acc = jnp.where(mine, cr, acc)
                kdm = jnp.where(mine, kd_t, 0.0).astype(BF16)
                sn_ref[b, h] = sdec * state + lax.dot_general(
                    kdm, v_t, (((0,), (0,)), ((), ())), preferred_element_type=F32)
            cross_tiles.append(acc)
        o = inner + jnp.concatenate(cross_tiles, axis=0)
        of = o * lax.rsqrt(jnp.mean(o * o, axis=-1, keepdims=True) + EPS)
        g = g_ref[:, sl].astype(F32)
        o_ref[:, sl] = (of * (g * jax.nn.sigmoid(g))).astype(BF16)


def _retention_decode(qr, kr, vr, gr, state, ls, rows):
    n = qr.shape[0]
    nseq = rows // ls
    row = pl.BlockSpec((rows, RET_W), lambda i: (i, 0))
    st = pl.BlockSpec((nseq, RET_HEADS, RET_DK, RET_DV), lambda i: (i, 0, 0, 0))
    return pl.pallas_call(
        functools.partial(_ret_decode_kernel, ls=ls),
        grid=(n // rows,),
        in_specs=[row, row, row, row, st],
        out_specs=(row, st),
        out_shape=(jax.ShapeDtypeStruct((n, RET_W), BF16),
                   jax.ShapeDtypeStruct(state.shape, F32)),
        compiler_params=pltpu.CompilerParams(dimension_semantics=("arbitrary",),
                                             vmem_limit_bytes=VMEM_LIMIT),
        name="ret_decode",
    )(qr, kr, vr, gr, state)


def _mla_prompt_kernel(q_ref, k_ref, km_ref, o_ref, vt_ref, vmt_ref, m_ref, l_ref, acc_ref, *, tq, tk):
    assert tk % tq == 0 and tk <= 2 * tq
    i = pl.program_id(1)
    nkm = km_ref.shape[0]
    nq = MLA_HEADS * tq

    @pl.when(i == 0)
    def _():
        vmt_ref[...] = km_ref[:, 0:KV_LORA].T
        for c in range(k_ref.shape[0] // tq):
            vt_ref[:, c * tq:(c + 1) * tq] = k_ref[c * tq:(c + 1) * tq, 0:KV_LORA].T

    q_all = q_ref[...].reshape(nq, QCAT_W)
    groups = [slice(g * GROUP_LANES, (g + 1) * GROUP_LANES) for g in range(nq // GROUP_LANES)]

    def scores(kc, grp):
        return lax.dot_general(kc, q_all[grp], (((1,), (1,)), ((), ())),
                               preferred_element_type=F32)

    s = scores(km_ref[...], slice(0, nq))
    s = jnp.where(lax.broadcasted_iota(jnp.int32, (nkm, nq), 0) < N_META, s, NEG_INF)
    m = jnp.max(s, axis=0, keepdims=True)
    p = jnp.exp(s - m)
    m_ref[...] = m
    l_ref[...] = jnp.sum(p, axis=0, keepdims=True)
    acc_ref[...] = jnp.dot(vmt_ref[...], p.astype(BF16), preferred_element_type=F32)

    def update(start, size, mask):
        kc = k_ref[pl.ds(start, size), :]
        vt = vt_ref[:, pl.ds(start, size)]
        ss = [scores(kc, grp) for grp in groups]
        for grp, s in zip(groups, ss):
            if mask is not None:
                s = jnp.where(mask[:, grp], s, NEG_INF)
            m_prev = m_ref[:, grp]
            m_new = jnp.maximum(m_prev, jnp.max(s, axis=0, keepdims=True))
            alpha = jnp.exp(m_prev - m_new)
            p = jnp.exp(s - m_new)
            m_ref[:, grp] = m_new
            l_ref[:, grp] = alpha * l_ref[:, grp] + jnp.sum(p, axis=0, keepdims=True)
            acc_ref[:, grp] = alpha * acc_ref[:, grp] + jnp.dot(
                vt, p.astype(BF16), preferred_element_type=F32)

    n_vis = i * tq

    def body(j, carry):
        update(pl.multiple_of(j * tk, tk), tk, None)
        return carry

    lax.fori_loop(0, n_vis // tk, body, 0)
    if tk > tq:
        @pl.when(n_vis % tk != 0)
        def _():
            update(pl.multiple_of(n_vis - tq, tq), tq, None)

    key = lax.broadcasted_iota(jnp.int32, (tq, nq), 0)
    qry = lax.broadcasted_iota(jnp.int32, (tq, nq), 1) % tq
    update(pl.multiple_of(n_vis, tq), tq, key <= qry)
    out = acc_ref[...] / l_ref[...]
    for h in range(MLA_HEADS):
        o_ref[:, h * KV_LORA:(h + 1) * KV_LORA] = out[:, h * tq:(h + 1) * tq].T.astype(BF16)


def _mla_prompt(qcat, kvcat, kmeta, nb, tq, tk):
    n = kvcat.shape[0]
    t = n // nb
    nq = t // tq
    return pl.pallas_call(
        functools.partial(_mla_prompt_kernel, tq=tq, tk=tk),
        grid=(nb, nq),
        in_specs=[pl.BlockSpec((MLA_HEADS, tq, QCAT_W), lambda b, i: (0, b * nq + i, 0)),
                  pl.BlockSpec((t, QCAT_W), lambda b, i: (b, 0)),
                  _const_spec(kmeta.shape)],
        out_specs=pl.BlockSpec((tq, MLA_HEADS * KV_LORA), lambda b, i: (b * nq + i, 0)),
        out_shape=jax.ShapeDtypeStruct((n, MLA_HEADS * KV_LORA), BF16),
        scratch_shapes=[pltpu.VMEM((KV_LORA, t), BF16), pltpu.VMEM((KV_LORA, kmeta.shape[0]), BF16),
                        pltpu.VMEM((1, MLA_HEADS * tq), F32), pltpu.VMEM((1, MLA_HEADS * tq), F32),
                        pltpu.VMEM((KV_LORA, MLA_HEADS * tq), F32)],
        compiler_params=pltpu.CompilerParams(dimension_semantics=("arbitrary", "arbitrary"),
                                             vmem_limit_bytes=VMEM_LIMIT),
        name="mla_prompt",
    )(qcat, kvcat, kmeta)


def _mla_meta_kernel(q_ref, km_ref, o_ref):
    km = km_ref[...]
    r = q_ref.shape[1]
    row = lax.broadcasted_iota(jnp.int32, (r, km.shape[0]), 0)
    col = lax.broadcasted_iota(jnp.int32, (r, km.shape[0]), 1)
    mask = (col <= row) & (col < N_META)
    for h in range(MLA_HEADS):
        s = lax.dot_general(q_ref[h], km, (((1,), (1,)), ((), ())), preferred_element_type=F32)
        s = jnp.where(mask, s, NEG_INF)
        p = jnp.exp(s - jnp.max(s, axis=-1, keepdims=True))
        acc = jnp.dot(p.astype(BF16), km[:, 0:KV_LORA], preferred_element_type=F32)
        o_ref[:, h * KV_LORA:(h + 1) * KV_LORA] = (
            acc / jnp.sum(p, axis=-1, keepdims=True)).astype(BF16)


def _mla_meta(qcat, kmeta):
    r = qcat.shape[1]
    return pl.pallas_call(
        _mla_meta_kernel,
        out_shape=jax.ShapeDtypeStruct((r, MLA_HEADS * KV_LORA), BF16),
        compiler_params=pltpu.CompilerParams(vmem_limit_bytes=VMEM_LIMIT),
        name="mla_meta",
    )(qcat, kmeta)


def _mla_decode_kernel(pt_ref, q_ref, kself_ref, peself_ref, kv_hbm, pe_hbm, *rest,
                       n_pages, page, ls, chunk, side_lv):
    if side_lv is None:
        o_ref, kvbuf, pebuf, kb16, pb16, s_ref, sem = rest
    else:
        (rq_ref, rk_ref, rv_ref, rg_ref, s0_ref, o_ref, ro_ref, sf_ref,
         kvbuf, pebuf, kb16, pb16, s_ref, sem, st_ref) = rest
    b = pl.program_id(0)
    nb = pl.num_programs(0)
    slot = b % 2
    n_keys = n_pages * page
    rope_rows = pb16.shape[0]

    def page_copies(seq, sl):
        out = []
        for j in range(n_pages):
            pid = pt_ref[seq, j]
            out.append(pltpu.make_async_copy(kv_hbm.at[pid], kvbuf.at[sl, pl.ds(j * page, page)],
                                             sem.at[0, sl]))
            out.append(pltpu.make_async_copy(pe_hbm.at[pid], pebuf.at[sl, j], sem.at[1, sl]))
        return out

    @pl.when(b == 0)
    def _():
        pb16[QK_ROPE:rope_rows, :] = jnp.zeros((rope_rows - QK_ROPE, n_keys), BF16)
        for c in page_copies(0, 0):
            c.start()

    @pl.when(b + 1 < nb)
    def _():
        for c in page_copies(b + 1, 1 - slot):
            c.start()

    for c in page_copies(b, slot):
        c.wait()

    q = q_ref[0]
    rows = q.shape[0]
    ql = q[:, 0:KV_LORA]
    qp = q[:, KV_LORA:QCAT_W]
    ppc = chunk // page
    n_chunks = n_keys // chunk
    for c in range(n_chunks):
        ksl = slice(c * chunk, (c + 1) * chunk)
        kb16[ksl, :] = kvbuf[slot, ksl, :].astype(BF16)
        for j in range(ppc):
            jj = c * ppc + j
            pb16[0:QK_ROPE, jj * page:(jj + 1) * page] = pebuf[slot, jj].astype(BF16)
        s_ref[:, ksl] = (
            lax.dot_general(ql, kb16[ksl, :], (((1,), (1,)), ((), ())), preferred_element_type=F32)
            + jnp.dot(qp, pb16[:, ksl], preferred_element_type=F32))

    parts = []
    n_split = 4 if n_chunks % 4 == 0 else 1
    for h in range(n_split):
        hsl = slice(h * n_keys // n_split, (h + 1) * n_keys // n_split)
        s_h = s_ref[:, hsl]
        m_h = jnp.max(s_h, axis=-1, keepdims=True)
        p_h = jnp.exp(s_h - m_h)
        parts.append((m_h, jnp.sum(p_h, axis=-1, keepdims=True),
                      jnp.dot(p_h.astype(BF16), kb16[hsl, :], preferred_element_type=F32)))

    qf = q.astype(F32)
    kvs = kself_ref[0]
    pes = peself_ref[0]
    t_row = lax.broadcasted_iota(jnp.int32, (rows, 1), 0) // MLA_HEADS
    ss = []
    for t2 in range(ls):
        s_t = (jnp.sum(qf[:, 0:KV_LORA] * kvs[t2:t2 + 1, :], axis=-1, keepdims=True)
               + jnp.sum(qf[:, KV_LORA:KV_LORA + QK_ROPE] * pes[t2:t2 + 1, :], axis=-1, keepdims=True))
        ss.append(jnp.where(t_row >= t2, s_t, NEG_INF))
    m_s = ss[0]
    for s_t in ss[1:]:
        m_s = jnp.maximum(m_s, s_t)
    l_s = jnp.zeros((rows, 1), F32)
    acc_s = jnp.zeros((rows, KV_LORA), F32)
    for t2 in range(ls):
        p_t = jnp.exp(ss[t2] - m_s)
        l_s = l_s + p_t
        acc_s = acc_s + p_t * kvs[t2:t2 + 1, :]
    parts.append((m_s, l_s, acc_s))

    m = parts[0][0]
    for m_h, _, _ in parts[1:]:
        m = jnp.maximum(m, m_h)
    l = jnp.zeros((rows, 1), F32)
    acc = jnp.zeros((rows, KV_LORA), F32)
    for m_h, l_h, acc_h in parts:
        w_h = jnp.exp(m_h - m)
        l = l + w_h * l_h
        acc = acc + w_h * acc_h
    o_ref[0] = (acc / l).astype(BF16)

    if side_lv is not None:
        n_seq = st_ref.shape[0]
        seq = b % n_seq

        @pl.when(b < n_seq)
        def _():
            st_ref[seq] = s0_ref[0]

        _ret_chunk(rq_ref.at[0], rk_ref.at[0], rv_ref.at[0], rg_ref.at[0], ro_ref.at[0],
                   st_ref.at[seq], side_lv)

        @pl.when(b == nb - 1)
        def _():
            sf_ref[...] = st_ref[...]


def _mla_decode(page_table, q, kself, peself, cache_kv, cache_pe_t, chunk, side=None):
    db, rows, _ = q.shape
    ls = kself.shape[1]
    n_pages = page_table.shape[1]
    page = cache_kv.shape[1]
    n_keys = n_pages * page
    in_specs = [pl.BlockSpec((1, rows, QCAT_W), lambda b, pt: (b, 0, 0)),
                pl.BlockSpec((1, ls, KV_LORA), lambda b, pt: (b, 0, 0)),
                pl.BlockSpec((1, ls, QK_ROPE), lambda b, pt: (b, 0, 0)),
                pl.BlockSpec(memory_space=pl.ANY),
                pl.BlockSpec(memory_space=pl.ANY)]
    out_specs = [pl.BlockSpec((1, rows, KV_LORA), lambda b, pt: (b, 0, 0))]
    out_shape = [jax.ShapeDtypeStruct((db, rows, KV_LORA), BF16)]
    scratch = [pltpu.VMEM((2, n_keys, KV_LORA), F32),
               pltpu.VMEM((2, n_pages, QK_ROPE, page), F32),
               pltpu.VMEM((n_keys, KV_LORA), BF16),
               pltpu.VMEM((QCAT_W - KV_LORA, n_keys), BF16),
               pltpu.VMEM((rows, n_keys), F32),
               pltpu.SemaphoreType.DMA((2, 2))]
    operands = [page_table, q, kself, peself, cache_kv, cache_pe_t]
    side_lv = None
    if side is not None:
        qr, kr, vr, gr, s0, L = side
        n_seq, t, _ = qr.shape
        assert n_seq * (t // L) == db, "one retention chunk per decode step"
        blk = pl.BlockSpec((1, L, RET_W), lambda b, pt: (b % n_seq, b // n_seq, 0))
        st_shape = (n_seq, RET_HEADS, RET_DK, RET_DV)
        in_specs += [blk, blk, blk, blk, pl.BlockSpec(s0.shape, lambda b, pt: (0, 0, 0, 0))]
        out_specs += [blk, pl.BlockSpec(st_shape, lambda b, pt: (0, 0, 0, 0))]
        out_shape += [jax.ShapeDtypeStruct((n_seq, t, RET_W), BF16),
                      jax.ShapeDtypeStruct(st_shape, F32)]
        scratch += [pltpu.VMEM(st_shape, F32)]
        operands += [qr, kr, vr, gr, s0]
        side_lv = float(L)
    grid_spec = pltpu.PrefetchScalarGridSpec(
        num_scalar_prefetch=1, grid=(db,), in_specs=in_specs, out_specs=out_specs,
        scratch_shapes=scratch)
    out = pl.pallas_call(
        functools.partial(_mla_decode_kernel, n_pages=n_pages, page=page, ls=ls, chunk=chunk,
                          side_lv=side_lv),
        grid_spec=grid_spec,
        out_shape=out_shape,
        compiler_params=pltpu.CompilerParams(dimension_semantics=("arbitrary",),
                                             vmem_limit_bytes=VMEM_LIMIT),
        name="mla_decode",
    )(*operands)
    return out[0] if side is None else tuple(out)


def _mix_ffn_kernel(x_ref, ret_ref, olat_ref, cin_ref, wuv_ref, wout_ref, gffn_ref, wup_ref,
                    cw_ref, cb_ref, wdown_ref, gfin_ref, y_ref, cout_ref,
                    carry_ref, stage_ref, hmid_ref, *, stride, carry_end, cw):
    t = pl.program_id(1)
    tm = x_ref.shape[0]
    hal = carry_ref.shape[0]
    d_ff = wdown_ref.shape[0]

    @pl.when(t == 0)
    def _():
        carry_ref[...] = cin_ref[...]

    mla = jnp.concatenate(
        [jnp.dot(olat_ref[:, h * KV_LORA:(h + 1) * KV_LORA], wuv_ref[h],
                 preferred_element_type=F32).astype(BF16) for h in range(MLA_HEADS)], axis=1)
    mixed = jnp.concatenate([ret_ref[...], mla], axis=1)
    h1 = x_ref[...] + jnp.dot(mixed, wout_ref[...], preferred_element_type=F32)
    a2 = _rms(h1, gffn_ref[...]).astype(BF16)

    def conv_half(c0, slot):
        u = jnp.dot(a2, wup_ref[:, c0:c0 + cw], preferred_element_type=F32)
        stage_ref[slot, 0:hal, :] = carry_ref[:, c0:c0 + cw]
        stage_ref[slot, hal:hal + tm, :] = u
        carry_ref[:, c0:c0 + cw] = u[carry_end - hal:carry_end, :]
        um1 = stage_ref[slot, hal - stride:hal - stride + tm, :]
        um2 = stage_ref[slot, hal - 2 * stride:hal - 2 * stride + tm, :]
        return (cb_ref[:, c0:c0 + cw] + cw_ref[0:1, c0:c0 + cw] * um2
                + cw_ref[1:2, c0:c0 + cw] * um1 + cw_ref[2:3, c0:c0 + cw] * u)

    n_slots = stage_ref.shape[0]
    for jc in range(d_ff // cw):
        ca = conv_half(jc * cw, (2 * jc) % n_slots)
        cg = conv_half(d_ff + jc * cw, (2 * jc + 1) % n_slots)
        hmid_ref[:, jc * cw:(jc + 1) * cw] = (cg * jax.nn.sigmoid(cg) * ca).astype(BF16)
    h2 = h1 + jnp.dot(hmid_ref[...], wdown_ref[...], preferred_element_type=F32)
    y_ref[...] = _rms(h2, gfin_ref[...])

    @pl.when(t == pl.num_programs(1) - 1)
    def _():
        cout_ref[0] = carry_ref[...]


def _mix_ffn(x, ret, olat, carry_in, wts, nseq, tm, stride, carry_end, cw):
    n, d = x.shape
    hal = carry_in.shape[0]
    nt = n // (nseq * tm)
    wuv, wout, gffn, wup, convw, convb, wdown, gfin = wts
    row = lambda w: pl.BlockSpec((tm, w), lambda b, t: (b * nt + t, 0))
    return pl.pallas_call(
        functools.partial(_mix_ffn_kernel, stride=stride, carry_end=carry_end, cw=cw),
        grid=(nseq, nt),
        in_specs=[row(d), row(RET_W), row(MLA_HEADS * KV_LORA), _const_spec(carry_in.shape),
                  _const_spec(wuv.shape), _const_spec(wout.shape), _const_spec(gffn.shape),
                  _const_spec(wup.shape), _const_spec(convw.shape), _const_spec(convb.shape),
                  _const_spec(wdown.shape), _const_spec(gfin.shape)],
        out_specs=(row(d), pl.BlockSpec((1, hal, wup.shape[1]), lambda b, t: (b, 0, 0))),
        out_shape=(jax.ShapeDtypeStruct((n, d), F32),
                   jax.ShapeDtypeStruct((nseq, hal, wup.shape[1]), F32)),
        scratch_shapes=[pltpu.VMEM((hal, wup.shape[1]), F32),
                        pltpu.VMEM((4, hal + tm, cw), F32),
                        pltpu.VMEM((tm, wdown.shape[0]), BF16)],
        compiler_params=pltpu.CompilerParams(dimension_semantics=("arbitrary", "arbitrary"),
                                             vmem_limit_bytes=VMEM_LIMIT),
        name="mix_ffn",
    )(x, ret, olat, carry_in, wuv, wout, gffn, wup, convw, convb, wdown, gfin)


def _rope_tables(pos):
    pos = np.asarray(pos, np.float64)[:, None]

    def cs(dim):
        inv = ROPE_THETA ** (-np.arange(0, dim, 2, dtype=np.float64) / dim)
        ang = pos * inv[None, :]
        return np.cos(ang), np.sin(ang)

    c, s = cs(RET_DK)
    c128 = np.concatenate([c, c], axis=-1)
    s128 = np.concatenate([-s, s], axis=-1)
    c, s = cs(QK_ROPE)
    z32 = np.zeros_like(s)
    z64 = np.zeros((pos.shape[0], LANES - QK_ROPE))
    c64 = np.concatenate([c, c, z64], axis=-1)
    s64a = np.concatenate([-s, z32, z64], axis=-1)
    s64b = np.concatenate([z32, s, z64], axis=-1)
    return tuple(np.asarray(t, np.float32) for t in (c128, s128, c64, s64a, s64b))


def kernel(x_prompt, x_sample, cache_kv_latent, cache_k_rope, state_retention, state_ffn_conv,
           page_table, meta_tokens, g_mix, w_in, g_q, w_uq, g_kv, w_uk, w_uv, w_out,
           g_ffn, w_up, conv_w, conv_b, w_down, g_final):
    nb, seq, d = x_prompt.shape
    db, ls, _ = x_sample.shape
    depth = w_in.shape[0]
    assert depth == 1, "single-layer step"
    n_pages = page_table.shape[1]
    page = cache_kv_latent.shape[2]
    past_len = n_pages * page
    d_ff = w_down.shape[1]
    l = 0

    w = w_in[l]
    win = jnp.concatenate([w, jnp.zeros((d, LANES - QK_ROPE), w.dtype)], axis=1).astype(BF16)
    wq = w_uq[l].reshape(Q_LORA, MLA_HEADS, QK_NOPE + QK_ROPE)
    wuq = jnp.concatenate([wq, jnp.zeros((Q_LORA, MLA_HEADS, LANES - QK_ROPE), wq.dtype)],
                          axis=-1).reshape(Q_LORA, MLA_HEADS * 2 * LANES).astype(BF16)
    wuk = jnp.transpose(w_uk[l], (1, 2, 0)).astype(BF16)
    wuv = jnp.transpose(w_uv[l], (1, 0, 2)).astype(BF16)
    proj_w = (g_mix[l][None, :], win, g_q[l][None, :], wuq, g_kv[l][None, :], wuk)
    ffn_w = (wuv, w_out[l].astype(BF16), g_ffn[l][None, :], w_up[l].astype(BF16),
             conv_w[l], conv_b[l][None, :], w_down[l].astype(BF16), g_final[None, :])

    tile = 128
    xm = jnp.concatenate([meta_tokens.astype(F32), jnp.zeros((tile - N_META, d), F32)], axis=0)
    tabs_m = _rope_tables(np.arange(tile))
    qr, kr, vr, gr, qcat_m, kvcat_m, ckv_m, kpe_m = _project(xm, tabs_m, proj_w, tile)
    zero_state = jnp.zeros((1, RET_HEADS, RET_DK, RET_DV), F32)
    ret_m, state_m = _retention_chunks(qr, kr, vr, gr, zero_state, 1, tile, N_META)
    ret_m = ret_m.reshape(tile, RET_W)
    row_valid = (jnp.arange(tile) < N_META)[:, None]
    kmeta = jnp.where(row_valid, kvcat_m, jnp.zeros_like(kvcat_m))
    olat_m = _mla_meta(qcat_m, kmeta)
    _, carry_m = _mix_ffn(xm, ret_m, olat_m, jnp.zeros((8, 2 * d_ff), F32), ffn_w,
                          1, tile, 1, N_META, 256)

    tm = 512
    xp = x_prompt.reshape(nb * seq, d)
    tabs_p = _rope_tables(N_META + np.arange(seq))
    qr_p, kr_p, vr_p, gr_p, qcat, kvcat, kv_p, pe_p = _project(
        xp, tabs_p, proj_w, tm, prefix=(ckv_m[:N_META], kpe_m[:N_META]))
    ns = db * ls
    xs = x_sample.reshape(ns, d)
    pos_s = past_len + np.arange(ls)
    tabs_s = tuple(np.tile(t, (db, 1)) for t in _rope_tables(pos_s))
    qr, kr, vr, gr, qcat_s, _, ckv_s, kpe_s = _project(xs, tabs_s, proj_w, min(ns, 256))

    ret_s, state_s = _retention_decode(qr, kr, vr, gr, state_retention[l], ls, min(ns, 64))
    q_s = jnp.transpose(qcat_s.reshape(MLA_HEADS, db, ls, QCAT_W), (1, 2, 0, 3)).reshape(
        db, ls * MLA_HEADS, QCAT_W)
    cache_pe_t = jnp.swapaxes(cache_k_rope[l], 1, 2)
    decode_args = (page_table, q_s, ckv_s.reshape(db, ls, KV_LORA), kpe_s.reshape(db, ls, QK_ROPE),
                   cache_kv_latent[l], cache_pe_t, min(1024, past_len))
    if nb * (seq // RET_CHUNK) == db:
        as3 = lambda a: a.reshape(nb, seq, RET_W)
        olat_s, ret_p, state_p = _mla_decode(
            *decode_args, side=(as3(qr_p), as3(kr_p), as3(vr_p), as3(gr_p), state_m, RET_CHUNK))
    else:
        olat_s = _mla_decode(*decode_args)
        ret_p, state_p = _retention_chunks(qr_p, kr_p, vr_p, gr_p, state_m, nb, RET_CHUNK, RET_CHUNK)
    ret_p = ret_p.reshape(nb * seq, RET_W)

    olat_p = _mla_prompt(qcat, kvcat, kmeta, nb, 512, 1024)
    y_p, carry_p = _mix_ffn(xp, ret_p, olat_p, carry_m[0], ffn_w, nb, 512, 1, 512, 256)

    olat_s = olat_s.reshape(ns, MLA_HEADS * KV_LORA)
    tmaj = lambda a: jnp.transpose(a.reshape(db, ls, a.shape[-1]), (1, 0, 2)).reshape(ns, a.shape[-1])
    carry_s_in = jnp.transpose(state_ffn_conv[l], (1, 0, 2)).reshape((CONV_W - 1) * db, 2 * d_ff)
    y_s, carry_s = _mix_ffn(tmaj(xs), tmaj(ret_s), tmaj(olat_s), carry_s_in, ffn_w,
                            1, ns, db, ns, 256)
    y_s = jnp.transpose(y_s.reshape(ls, db, d), (1, 0, 2))
    conv_s = jnp.transpose(carry_s.reshape(CONV_W - 1, db, 2 * d_ff), (1, 0, 2))

    return (y_p.reshape(nb, seq, d), y_s,
            kv_p[None], pe_p[None],
            state_p[None],
            carry_p[:, 8 - (CONV_W - 1):, :][None],
            ckv_s.reshape(db, ls, KV_LORA)[None], kpe_s.reshape(db, ls, QK_ROPE)[None],
            state_s[None], conv_s[None])
```

```python
import functools
import math

import jax
import jax.numpy as jnp
import numpy as np
from jax import lax
from jax.experimental import pallas as pl
from jax.experimental.pallas import tpu as pltpu

F32 = jnp.float32
BF16 = jnp.bfloat16

N_META = 16
RET_HEADS = 4
RET_DK = 128
RET_DV = 128
MLA_HEADS = 4
Q_LORA = 384
KV_LORA = 256
QK_NOPE = 128
QK_ROPE = 64
V_DIM = 128
CONV_W = 3
ROPE_THETA = 10000.0
EPS = 1e-6
SOFTMAX_SCALE = (QK_NOPE + QK_ROPE) ** -0.5
RET_W = RET_HEADS * RET_DV
LANES = 128
ROW_TILE = 16
GROUP_LANES = 512
RET_CHUNK = 128
QCAT_W = KV_LORA + LANES
LOG_GAMMA = tuple(math.log1p(-(2.0 ** (-5.0 - h))) for h in range(RET_HEADS))
VMEM_LIMIT = 56 * 1024 * 1024
NEG_INF = float("-inf")


def _const_spec(shape):
    nd = len(shape)
    return pl.BlockSpec(shape, lambda *_: (0,) * nd, pipeline_mode=pl.Buffered(1))


def _rms(x, g):
    return x * lax.rsqrt(jnp.mean(x * x, axis=-1, keepdims=True) + EPS) * g


def _proj_kernel(x_ref, c128_ref, s128_ref, c64_ref, s64a_ref, s64b_ref,
                 gmix_ref, win_ref, gq_ref, wuq_ref, gkv_ref, wuk_ref, *rest, tiles_per_seq):
    if tiles_per_seq:
        pre_ckv_ref, pre_kpe_ref = rest[:2]
        rest = rest[2:]
    qr_ref, kr_ref, vr_ref, gr_ref, qcat_ref, kvcat_ref, ckv_ref, kpe_ref = rest
    a = _rms(x_ref[...], gmix_ref[...])
    z = jnp.dot(a.astype(BF16), win_ref[...], preferred_element_type=F32)
    c128 = c128_ref[...]
    s128 = s128_ref[...]
    c64 = c64_ref[...]
    s64a = s64a_ref[...]
    s64b = s64b_ref[...]

    def rope128(v):
        return v * c128 + pltpu.roll(v, 64, 1) * s128

    def rope64(v):
        return v * c64 + pltpu.roll(v, 96, 1) * s64a + pltpu.roll(v, 32, 1) * s64b

    for h in range(RET_HEADS):
        sl = slice(h * RET_DK, (h + 1) * RET_DK)
        qr_ref[:, sl] = rope128(z[:, sl]).astype(BF16)
        ksl = slice(RET_W + h * RET_DK, RET_W + (h + 1) * RET_DK)
        kr_ref[:, sl] = (rope128(z[:, ksl]) * (RET_DK ** -0.5)).astype(BF16)
    vr_ref[...] = z[:, 2 * RET_W:3 * RET_W].astype(BF16)
    gr_ref[...] = z[:, 3 * RET_W:4 * RET_W].astype(BF16)

    o = 4 * RET_W
    cqn = _rms(z[:, o:o + Q_LORA], gq_ref[...])
    q2 = jnp.dot(cqn.astype(BF16), wuq_ref[...], preferred_element_type=F32)
    for h in range(MLA_HEADS):
        b0 = h * 2 * LANES
        nope = q2[:, b0:b0 + QK_NOPE]
        pe = rope64(q2[:, b0 + LANES:b0 + 2 * LANES])
        qlat = jnp.dot(nope.astype(BF16), wuk_ref[h], preferred_element_type=F32)
        qcat_ref[h, :, 0:KV_LORA] = (qlat * SOFTMAX_SCALE).astype(BF16)
        qcat_ref[h, :, KV_LORA:QCAT_W] = (pe * SOFTMAX_SCALE).astype(BF16)

    o += Q_LORA
    ckvn = _rms(z[:, o:o + KV_LORA], gkv_ref[...])
    o += KV_LORA
    kp = rope64(z[:, o:o + LANES])
    kvcat_ref[:, 0:KV_LORA] = ckvn.astype(BF16)
    kvcat_ref[:, KV_LORA:QCAT_W] = kp.astype(BF16)
    if tiles_per_seq:
        tm = x_ref.shape[0]
        n_pre = pre_ckv_ref.shape[0]
        t = pl.program_id(0) % tiles_per_seq

        @pl.when(t == 0)
        def _():
            ckv_ref[0, 0:n_pre, :] = pre_ckv_ref[...]
            kpe_ref[0, 0:n_pre, :] = pre_kpe_ref[...]

        rows = pl.ds(pl.multiple_of(n_pre + t * tm, 8), tm)
        ckv_ref[0, rows, :] = ckvn
        kpe_ref[0, rows, :] = kp[:, 0:QK_ROPE]
    else:
        ckv_ref[...] = ckvn
        kpe_ref[...] = kp[:, 0:QK_ROPE]


def _project(x, tabs, wts, tm, prefix=None):
    n, d = x.shape
    nt = tabs[0].shape[0] // tm
    gmix, win, gq, wuq, gkv, wuk = wts
    row = lambda w: pl.BlockSpec((tm, w), lambda i: (i, 0))
    tab = pl.BlockSpec((tm, LANES), lambda i: (i % nt, 0))
    if prefix is None:
        key_shapes = (jax.ShapeDtypeStruct((n, KV_LORA), F32), jax.ShapeDtypeStruct((n, QK_ROPE), F32))
        key_specs = (row(KV_LORA), row(QK_ROPE))
        extra, extra_specs = (), []
    else:
        t_all = prefix[0].shape[0] + nt * tm
        nseq = n // (nt * tm)
        seq_blk = lambda w: pl.BlockSpec((1, t_all, w), lambda i: (i // nt, 0, 0))
        key_shapes = (jax.ShapeDtypeStruct((nseq, t_all, KV_LORA), F32),
                      jax.ShapeDtypeStruct((nseq, t_all, QK_ROPE), F32))
        key_specs = (seq_blk(KV_LORA), seq_blk(QK_ROPE))
        extra, extra_specs = tuple(prefix), [_const_spec(p.shape) for p in prefix]
    out_shapes = (
        jax.ShapeDtypeStruct((n, RET_W), BF16), jax.ShapeDtypeStruct((n, RET_W), BF16),
        jax.ShapeDtypeStruct((n, RET_W), BF16), jax.ShapeDtypeStruct((n, RET_W), BF16),
        jax.ShapeDtypeStruct((MLA_HEADS, n, QCAT_W), BF16),
        jax.ShapeDtypeStruct((n, QCAT_W), BF16)) + key_shapes
    out_specs = (row(RET_W), row(RET_W), row(RET_W), row(RET_W),
                 pl.BlockSpec((MLA_HEADS, tm, QCAT_W), lambda i: (0, i, 0)),
                 row(QCAT_W)) + key_specs
    return pl.pallas_call(
        functools.partial(_proj_kernel, tiles_per_seq=0 if prefix is None else nt),
        grid=(n // tm,),
        in_specs=[row(d), tab, tab, tab, tab, tab,
                  _const_spec(gmix.shape), _const_spec(win.shape), _const_spec(gq.shape),
                  _const_spec(wuq.shape), _const_spec(gkv.shape), _const_spec(wuk.shape)]
        + extra_specs,
        out_specs=out_specs,
        out_shape=out_shapes,
        compiler_params=pltpu.CompilerParams(dimension_semantics=("arbitrary",),
                                             vmem_limit_bytes=VMEM_LIMIT),
        name="proj",
    )(x, *tabs, gmix, win, gq, wuq, gkv, wuk, *extra)


def _ret_chunk_steps(q_ref, k_ref, v_ref, g_ref, o_ref, st_ref, lv):
    L = q_ref.shape[0]
    li = lax.broadcasted_iota(jnp.int32, (L, L), 0)
    mi = lax.broadcasted_iota(jnp.int32, (L, L), 1)
    diff = (li - mi).astype(F32)
    n = lax.broadcasted_iota(jnp.int32, (L, 1), 0).astype(F32)
    for h in range(RET_HEADS):
        lg = LOG_GAMMA[h]
        sl = slice(h * RET_DK, (h + 1) * RET_DK)
        decay = jnp.where(diff >= 0, jnp.exp(lg * jnp.maximum(diff, 0.0)), 0.0)
        qdec = jnp.exp(lg * (n + 1.0))
        kdec = jnp.where(n < lv, jnp.exp(lg * jnp.maximum(lv - 1.0 - n, 0.0)), 0.0)
        q = q_ref[:, sl]
        k = k_ref[:, sl]
        v = v_ref[:, sl]
        s = lax.dot_general(q, k, (((1,), (1,)), ((), ())), preferred_element_type=F32) * decay
        state = st_ref[h]
        qd = (q.astype(F32) * qdec).astype(BF16)
        cross = jnp.dot(qd, state.astype(BF16), preferred_element_type=F32)
        yield
        inner = jnp.dot(s.astype(BF16), v, preferred_element_type=F32)
        o = inner + cross
        kd = (k.astype(F32) * kdec).astype(BF16)
        st_ref[h] = math.exp(lg * lv) * state + lax.dot_general(
            kd, v, (((0,), (0,)), ((), ())), preferred_element_type=F32)
        of = o * lax.rsqrt(jnp.mean(o * o, axis=-1, keepdims=True) + EPS)
        g = g_ref[:, sl].astype(F32)
        o_ref[:, sl] = (of * (g * jax.nn.sigmoid(g))).astype(BF16)
        yield


def _ret_chunk(*args):
    for _ in _ret_chunk_steps(*args):
        pass


def _ret_chunk_kernel(q_ref, k_ref, v_ref, g_ref, s0_ref, o_ref, sf_ref, st_ref, *, lv):
    c = pl.program_id(0)
    nb = q_ref.shape[0]

    @pl.when(c == 0)
    def _():
        for b in range(nb):
            st_ref[b] = s0_ref[0]

    for b in range(nb):
        _ret_chunk(q_ref.at[b], k_ref.at[b], v_ref.at[b], g_ref.at[b], o_ref.at[b],
                   st_ref.at[b], lv)

    @pl.when(c == pl.num_programs(0) - 1)
    def _():
        sf_ref[...] = st_ref[...]


def _retention_chunks(qr, kr, vr, gr, s0, nb, L, lv):
    n = qr.shape[0]
    t = n // nb
    as3 = lambda a: a.reshape(nb, t, RET_W)
    blk = pl.BlockSpec((nb, L, RET_W), lambda c: (0, c, 0))
    return pl.pallas_call(
        functools.partial(_ret_chunk_kernel, lv=float(lv)),
        grid=(t // L,),
        in_specs=[blk, blk, blk, blk, _const_spec(s0.shape)],
        out_specs=(blk, pl.BlockSpec((nb, RET_HEADS, RET_DK, RET_DV), lambda c: (0, 0, 0, 0))),
        out_shape=(jax.ShapeDtypeStruct((nb, t, RET_W), BF16),
                   jax.ShapeDtypeStruct((nb, RET_HEADS, RET_DK, RET_DV), F32)),
        scratch_shapes=[pltpu.VMEM((nb, RET_HEADS, RET_DK, RET_DV), F32)],
        compiler_params=pltpu.CompilerParams(dimension_semantics=("arbitrary",),
                                             vmem_limit_bytes=VMEM_LIMIT),
        name="ret_chunks",
    )(as3(qr), as3(kr), as3(vr), as3(gr), s0)


def _ret_decode_kernel(q_ref, k_ref, v_ref, g_ref, s_ref, o_ref, sn_ref, *, ls):
    R = q_ref.shape[0]
    per_tile = ROW_TILE // ls
    li = lax.broadcasted_iota(jnp.int32, (R, R), 0)
    mi = lax.broadcasted_iota(jnp.int32, (R, R), 1)
    same = (li // ls) == (mi // ls)
    diff = ((li % ls) - (mi % ls)).astype(F32)
    t_col = (lax.broadcasted_iota(jnp.int32, (R, 1), 0) % ls).astype(F32)
    seq_in_tile = lax.broadcasted_iota(jnp.int32, (ROW_TILE, 1), 0) // ls
    for h in range(RET_HEADS):
        lg = LOG_GAMMA[h]
        sl = slice(h * RET_DK, (h + 1) * RET_DK)
        q = q_ref[:, sl]
        k = k_ref[:, sl]
        v = v_ref[:, sl]
        decay = jnp.where(same & (diff >= 0), jnp.exp(lg * jnp.maximum(diff, 0.0)), 0.0)
        s = lax.dot_general(q, k, (((1,), (1,)), ((), ())), preferred_element_type=F32) * decay
        inner = jnp.dot(s.astype(BF16), v, preferred_element_type=F32)
        qd = q.astype(F32) * jnp.exp(lg * (t_col + 1.0))
        kd = k.astype(F32) * jnp.exp(lg * (ls - 1.0 - t_col))
        sdec = math.exp(lg * ls)
        cross_tiles = []
        for tt in range(R // ROW_TILE):
            rows = slice(tt * ROW_TILE, (tt + 1) * ROW_TILE)
            qd_t = qd[rows].astype(BF16)
            kd_t = kd[rows]
            v_t = v[rows]
            acc = jnp.zeros((ROW_TILE, RET_DV), F32)
            for j in range(per_tile):
                b = tt * per_tile + j
                state = s_ref[b, h]
                mine = seq_in_tile == j
                cr = jnp.dot(qd_t, state.astype(BF16), preferred_element_type=F32)
                acc = jnp.where(mine, cr, acc)
                kdm = jnp.where(mine, kd_t, 0.0).astype(BF16)
                sn_ref[b, h] = sdec * state + lax.dot_general(
                    kdm, v_t, (((0,), (0,)), ((), ())), preferred_element_type=F32)
            cross_tiles.append(acc)
        o = inner + jnp.concatenate(cross_tiles, axis=0)
        of = o * lax.rsqrt(jnp.mean(o * o, axis=-1, keepdims=True) + EPS)
        g = g_ref[:, sl].astype(F32)
        o_ref[:, sl] = (of * (g * jax.nn.sigmoid(g))).astype(BF16)


def _retention_decode(qr, kr, vr, gr, state, ls, rows):
    n = qr.shape[0]
    nseq = rows // ls
    row = pl.BlockSpec((rows, RET_W), lambda i: (i, 0))
    st = pl.BlockSpec((nseq, RET_HEADS, RET_DK, RET_DV), lambda i: (i, 0, 0, 0))
    return pl.pallas_call(
        functools.partial(_ret_decode_kernel, ls=ls),
        grid=(n // rows,),
        in_specs=[row, row, row, row, st],
        out_specs=(row, st),
        out_shape=(jax.ShapeDtypeStruct((n, RET_W), BF16),
                   jax.ShapeDtypeStruct(state.shape, F32)),
        compiler_params=pltpu.CompilerParams(dimension_semantics=("arbitrary",),
                                             vmem_limit_bytes=VMEM_LIMIT),
        name="ret_decode",
    )(qr, kr, vr, gr, state)


def _mla_prompt_kernel(q_ref, k_ref, km_ref, o_ref, vt_ref, vmt_ref, m_ref, l_ref, acc_ref, *, tq, tk):
    assert tk % tq == 0 and tk <= 2 * tq
    i = pl.program_id(1)
    nkm = km_ref.shape[0]
    nq = MLA_HEADS * tq

    @pl.when(i == 0)
    def _():
        vmt_ref[...] = km_ref[:, 0:KV_LORA].T
        for c in range(k_ref.shape[0] // tq):
            vt_ref[:, c * tq:(c + 1) * tq] = k_ref[c * tq:(c + 1) * tq, 0:KV_LORA].T

    q_all = q_ref[...].reshape(nq, QCAT_W)
    groups = [slice(g * GROUP_LANES, (g + 1) * GROUP_LANES) for g in range(nq // GROUP_LANES)]

    def scores(kc, grp):
        return lax.dot_general(kc, q_all[grp], (((1,), (1,)), ((), ())),
                               preferred_element_type=F32)

    s = scores(km_ref[...], slice(0, nq))
    s = jnp.where(lax.broadcasted_iota(jnp.int32, (nkm, nq), 0) < N_META, s, NEG_INF)
    m = jnp.max(s, axis=0, keepdims=True)
    p = jnp.exp(s - m)
    m_ref[...] = m
    l_ref[...] = jnp.sum(p, axis=0, keepdims=True)
    acc_ref[...] = jnp.dot(vmt_ref[...], p.astype(BF16), preferred_element_type=F32)

    def update(start, size, mask):
        kc = k_ref[pl.ds(start, size), :]
        vt = vt_ref[:, pl.ds(start, size)]
        ss = [scores(kc, grp) for grp in groups]
        for grp, s in zip(groups, ss):
            if mask is not None:
                s = jnp.where(mask[:, grp], s, NEG_INF)
            m_prev = m_ref[:, grp]
            m_new = jnp.maximum(m_prev, jnp.max(s, axis=0, keepdims=True))
            alpha = jnp.exp(m_prev - m_new)
            p = jnp.exp(s - m_new)
            m_ref[:, grp] = m_new
            l_ref[:, grp] = alpha * l_ref[:, grp] + jnp.sum(p, axis=0, keepdims=True)
            acc_ref[:, grp] = alpha * acc_ref[:, grp] + jnp.dot(
                vt, p.astype(BF16), preferred_element_type=F32)

    n_vis = i * tq

    def body(j, carry):
        update(pl.multiple_of(j * tk, tk), tk, None)
        return carry

    lax.fori_loop(0, n_vis // tk, body, 0)
    if tk > tq:
        @pl.when(n_vis % tk != 0)
        def _():
            update(pl.multiple_of(n_vis - tq, tq), tq, None)

    key = lax.broadcasted_iota(jnp.int32, (tq, nq), 0)
    qry = lax.broadcasted_iota(jnp.int32, (tq, nq), 1) % tq
    update(pl.multiple_of(n_vis, tq), tq, key <= qry)
    out = acc_ref[...] / l_ref[...]
    for h in range(MLA_HEADS):
        o_ref[:, h * KV_LORA:(h + 1) * KV_LORA] = out[:, h * tq:(h + 1) * tq].T.astype(BF16)


def _mla_prompt(qcat, kvcat, kmeta, nb, tq, tk):
    n = kvcat.shape[0]
    t = n // nb
    nq = t // tq
    return pl.pallas_call(
        functools.partial(_mla_prompt_kernel, tq=tq, tk=tk),
        grid=(nb, nq),
        in_specs=[pl.BlockSpec((MLA_HEADS, tq, QCAT_W), lambda b, i: (0, b * nq + i, 0)),
                  pl.BlockSpec((t, QCAT_W), lambda b, i: (b, 0)),
                  _const_spec(kmeta.shape)],
        out_specs=pl.BlockSpec((tq, MLA_HEADS * KV_LORA), lambda b, i: (b * nq + i, 0)),
        out_shape=jax.ShapeDtypeStruct((n, MLA_HEADS * KV_LORA), BF16),
        scratch_shapes=[pltpu.VMEM((KV_LORA, t), BF16), pltpu.VMEM((KV_LORA, kmeta.shape[0]), BF16),
                        pltpu.VMEM((1, MLA_HEADS * tq), F32), pltpu.VMEM((1, MLA_HEADS * tq), F32),
                        pltpu.VMEM((KV_LORA, MLA_HEADS * tq), F32)],
        compiler_params=pltpu.CompilerParams(dimension_semantics=("arbitrary", "arbitrary"),
                                             vmem_limit_bytes=VMEM_LIMIT),
        name="mla_prompt",
    )(qcat, kvcat, kmeta)


def _mla_meta_kernel(q_ref, km_ref, o_ref):
    km = km_ref[...]
    r = q_ref.shape[1]
    row = lax.broadcasted_iota(jnp.int32, (r, km.shape[0]), 0)
    col = lax.broadcasted_iota(jnp.int32, (r, km.shape[0]), 1)
    mask = (col <= row) & (col < N_META)
    for h in range(MLA_HEADS):
        s = lax.dot_general(q_ref[h], km, (((1,), (1,)), ((), ())), preferred_element_type=F32)
        s = jnp.where(mask, s, NEG_INF)
        p = jnp.exp(s - jnp.max(s, axis=-1, keepdims=True))
        acc = jnp.dot(p.astype(BF16), km[:, 0:KV_LORA], preferred_element_type=F32)
        o_ref[:, h * KV_LORA:(h + 1) * KV_LORA] = (
            acc / jnp.sum(p, axis=-1, keepdims=True)).astype(BF16)


def _mla_meta(qcat, kmeta):
    r = qcat.shape[1]
    return pl.pallas_call(
        _mla_meta_kernel,
        out_shape=jax.ShapeDtypeStruct((r, MLA_HEADS * KV_LORA), BF16),
        compiler_params=pltpu.CompilerParams(vmem_limit_bytes=VMEM_LIMIT),
        name="mla_meta",
    )(qcat, kmeta)


def _mla_decode_kernel(pt_ref, q_ref, kself_ref, peself_ref, kv_hbm, pe_hbm, *rest,
                       n_pages, page, ls, chunk, side_lv):
    if side_lv is None:
        o_ref, kvbuf, pebuf, kb16, pb16, s_ref, sem = rest
    else:
        (rq_ref, rk_ref, rv_ref, rg_ref, s0_ref, o_ref, ro_ref, sf_ref,
         kvbuf, pebuf, kb16, pb16, s_ref, sem, st_ref) = rest
    b = pl.program_id(0)
    nb = pl.num_programs(0)
    slot = b % 2
    n_keys = n_pages * page

    def page_copies(seq, sl):
        out = []
        for j in range(n_pages):
            pid = pt_ref[seq, j]
            out.append(pltpu.make_async_copy(kv_hbm.at[pid], kvbuf.at[sl, pl.ds(j * page, page)],
                                             sem.at[0, sl]))
            out.append(pltpu.make_async_copy(pe_hbm.at[pid], pebuf.at[sl, j], sem.at[1, sl]))
        return out

    @pl.when(b == 0)
    def _():
        for c in page_copies(0, 0):
            c.start()

    @pl.when(b + 1 < nb)
    def _():
        for c in page_copies(b + 1, 1 - slot):
            c.start()

    for c in page_copies(b, slot):
        c.wait()

    q = q_ref[0]
    rows = q.shape[0]
    ql = q[:, 0:KV_LORA]
    qp = q[:, KV_LORA:KV_LORA + QK_ROPE]
    ppc = chunk // page
    n_chunks = n_keys // chunk
    side = iter(())
    if side_lv is not None:
        n_seq = st_ref.shape[0]
        seq = b % n_seq

        @pl.when(b < n_seq)
        def _():
            st_ref[seq] = s0_ref[0]

        side = _ret_chunk_steps(rq_ref.at[0], rk_ref.at[0], rv_ref.at[0], rg_ref.at[0],
                                ro_ref.at[0], st_ref.at[seq], side_lv)
    for c in range(n_chunks):
        next(side, None)
        ksl = slice(c * chunk, (c + 1) * chunk)
        kb16[ksl, :] = kvbuf[slot, ksl, :].astype(BF16)
        for j in range(ppc):
            jj = c * ppc + j
            pb16[:, jj * page:(jj + 1) * page] = pebuf[slot, jj].astype(BF16)
        s_ref[:, ksl] = (
            lax.dot_general(ql, kb16[ksl, :], (((1,), (1,)), ((), ())), preferred_element_type=F32)
            + jnp.dot(qp, pb16[:, ksl], preferred_element_type=F32))

    parts = []
    n_split = 4 if n_chunks % 4 == 0 else 1
    for h in range(n_split):
        hsl = slice(h * n_keys // n_split, (h + 1) * n_keys // n_split)
        s_h = s_ref[:, hsl]
        m_h = jnp.max(s_h, axis=-1, keepdims=True)
        p_h = jnp.exp(s_h - m_h)
        parts.append((m_h, jnp.sum(p_h, axis=-1, keepdims=True),
                      jnp.dot(p_h.astype(BF16), kb16[hsl, :], preferred_element_type=F32)))

    qf = q.astype(F32)
    kvs = kself_ref[0]
    pes = peself_ref[0]
    t_row = lax.broadcasted_iota(jnp.int32, (rows, 1), 0) // MLA_HEADS
    ss = []
    for t2 in range(ls):
        s_t = (jnp.sum(qf[:, 0:KV_LORA] * kvs[t2:t2 + 1, :], axis=-1, keepdims=True)
               + jnp.sum(qf[:, KV_LORA:KV_LORA + QK_ROPE] * pes[t2:t2 + 1, :], axis=-1, keepdims=True))
        ss.append(jnp.where(t_row >= t2, s_t, NEG_INF))
    m_s = ss[0]
    for s_t in ss[1:]:
        m_s = jnp.maximum(m_s, s_t)
    l_s = jnp.zeros((rows, 1), F32)
    acc_s = jnp.zeros((rows, KV_LORA), F32)
    for t2 in range(ls):
        p_t = jnp.exp(ss[t2] - m_s)
        l_s = l_s + p_t
        acc_s = acc_s + p_t * kvs[t2:t2 + 1, :]
    parts.append((m_s, l_s, acc_s))

    m = parts[0][0]
    for m_h, _, _ in parts[1:]:
        m = jnp.maximum(m, m_h)
    l = jnp.zeros((rows, 1), F32)
    acc = jnp.zeros((rows, KV_LORA), F32)
    for m_h, l_h, acc_h in parts:
        w_h = jnp.exp(m_h - m)
        l = l + w_h * l_h
        acc = acc + w_h * acc_h
    o_ref[0] = (acc / l).astype(BF16)

    for _ in side:
        pass
    if side_lv is not None:
        @pl.when(b == nb - 1)
        def _():
            sf_ref[...] = st_ref[...]


def _mla_decode(page_table, q, kself, peself, cache_kv, cache_pe_t, chunk, side=None):
    db, rows, _ = q.shape
    ls = kself.shape[1]
    n_pages = page_table.shape[1]
    page = cache_kv.shape[1]
    n_keys = n_pages * page
    in_specs = [pl.BlockSpec((1, rows, QCAT_W), lambda b, pt: (b, 0, 0)),
                pl.BlockSpec((1, ls, KV_LORA), lambda b, pt: (b, 0, 0)),
                pl.BlockSpec((1, ls, QK_ROPE), lambda b, pt: (b, 0, 0)),
                pl.BlockSpec(memory_space=pl.ANY),
                pl.BlockSpec(memory_space=pl.ANY)]
    out_specs = [pl.BlockSpec((1, rows, KV_LORA), lambda b, pt: (b, 0, 0))]
    out_shape = [jax.ShapeDtypeStruct((db, rows, KV_LORA), BF16)]
    scratch = [pltpu.VMEM((2, n_keys, KV_LORA), F32),
               pltpu.VMEM((2, n_pages, QK_ROPE, page), F32),
               pltpu.VMEM((n_keys, KV_LORA), BF16),
               pltpu.VMEM((QK_ROPE, n_keys), BF16),
               pltpu.VMEM((rows, n_keys), F32),
               pltpu.SemaphoreType.DMA((2, 2))]
    operands = [page_table, q, kself, peself, cache_kv, cache_pe_t]
    side_lv = None
    if side is not None:
        qr, kr, vr, gr, s0, L = side
        n_seq, t, _ = qr.shape
        assert n_seq * (t // L) == db, "one retention chunk per decode step"
        blk = pl.BlockSpec((1, L, RET_W), lambda b, pt: (b % n_seq, b // n_seq, 0))
        st_shape = (n_seq, RET_HEADS, RET_DK, RET_DV)
        in_specs += [blk, blk, blk, blk, pl.BlockSpec(s0.shape, lambda b, pt: (0, 0, 0, 0))]
        out_specs += [blk, pl.BlockSpec(st_shape, lambda b, pt: (0, 0, 0, 0))]
        out_shape += [jax.ShapeDtypeStruct((n_seq, t, RET_W), BF16),
                      jax.ShapeDtypeStruct(st_shape, F32)]
        scratch += [pltpu.VMEM(st_shape, F32)]
        operands += [qr, kr, vr, gr, s0]
        side_lv = float(L)
    grid_spec = pltpu.PrefetchScalarGridSpec(
        num_scalar_prefetch=1, grid=(db,), in_specs=in_specs, out_specs=out_specs,
        scratch_shapes=scratch)
    out = pl.pallas_call(
        functools.partial(_mla_decode_kernel, n_pages=n_pages, page=page, ls=ls, chunk=chunk,
                          side_lv=side_lv),
        grid_spec=grid_spec,
        out_shape=out_shape,
        compiler_params=pltpu.CompilerParams(dimension_semantics=("arbitrary",),
                                             vmem_limit_bytes=VMEM_LIMIT),
        name="mla_decode",
    )(*operands)
    return out[0] if side is None else tuple(out)


def _mix_ffn_kernel(x_ref, ret_ref, olat_ref, cin_ref, wuv_ref, wout_ref, gffn_ref, wup_ref,
                    cw_ref, cb_ref, wdown_ref, gfin_ref, y_ref, cout_ref,
                    carry_ref, stage_ref, hmid_ref, *, stride, carry_end, cw):
    t = pl.program_id(1)
    tm = x_ref.shape[0]
    hal = carry_ref.shape[0]
    d_ff = wdown_ref.shape[0]

    @pl.when(t == 0)
    def _():
        carry_ref[...] = cin_ref[...]

    mla = jnp.concatenate(
        [jnp.dot(olat_ref[:, h * KV_LORA:(h + 1) * KV_LORA], wuv_ref[h],
                 preferred_element_type=F32).astype(BF16) for h in range(MLA_HEADS)], axis=1)
    mixed = jnp.concatenate([ret_ref[...], mla], axis=1)
    h1 = x_ref[...] + jnp.dot(mixed, wout_ref[...], preferred_element_type=F32)
    a2 = _rms(h1, gffn_ref[...]).astype(BF16)

    def conv_half(c0, slot):
        u = jnp.dot(a2, wup_ref[:, c0:c0 + cw], preferred_element_type=F32)
        stage_ref[slot, 0:hal, :] = carry_ref[:, c0:c0 + cw]
        stage_ref[slot, hal:hal + tm, :] = u
        carry_ref[:, c0:c0 + cw] = u[carry_end - hal:carry_end, :]
        um1 = stage_ref[slot, hal - stride:hal - stride + tm, :]
        um2 = stage_ref[slot, hal - 2 * stride:hal - 2 * stride + tm, :]
        return (cb_ref[:, c0:c0 + cw] + cw_ref[0:1, c0:c0 + cw] * um2
                + cw_ref[1:2, c0:c0 + cw] * um1 + cw_ref[2:3, c0:c0 + cw] * u)

    n_slots = stage_ref.shape[0]
    for jc in range(d_ff // cw):
        ca = conv_half(jc * cw, (2 * jc) % n_slots)
        cg = conv_half(d_ff + jc * cw, (2 * jc + 1) % n_slots)
        hmid_ref[:, jc * cw:(jc + 1) * cw] = (cg * jax.nn.sigmoid(cg) * ca).astype(BF16)
    h2 = h1 + jnp.dot(hmid_ref[...], wdown_ref[...], preferred_element_type=F32)
    y_ref[...] = _rms(h2, gfin_ref[...])

    @pl.when(t == pl.num_programs(1) - 1)
    def _():
        cout_ref[0] = carry_ref[...]


def _mix_ffn(x, ret, olat, carry_in, wts, nseq, tm, stride, carry_end, cw):
    n, d = x.shape
    hal = carry_in.shape[0]
    nt = n // (nseq * tm)
    wuv, wout, gffn, wup, convw, convb, wdown, gfin = wts
    row = lambda w: pl.BlockSpec((tm, w), lambda b, t: (b * nt + t, 0))
    return pl.pallas_call(
        functools.partial(_mix_ffn_kernel, stride=stride, carry_end=carry_end, cw=cw),
        grid=(nseq, nt),
        in_specs=[row(d), row(RET_W), row(MLA_HEADS * KV_LORA), _const_spec(carry_in.shape),
                  _const_spec(wuv.shape), _const_spec(wout.shape), _const_spec(gffn.shape),
                  _const_spec(wup.shape), _const_spec(convw.shape), _const_spec(convb.shape),
                  _const_spec(wdown.shape), _const_spec(gfin.shape)],
        out_specs=(row(d), pl.BlockSpec((1, hal, wup.shape[1]), lambda b, t: (b, 0, 0))),
        out_shape=(jax.ShapeDtypeStruct((n, d), F32),
                   jax.ShapeDtypeStruct((nseq, hal, wup.shape[1]), F32)),
        scratch_shapes=[pltpu.VMEM((hal, wup.shape[1]), F32),
                        pltpu.VMEM((4, hal + tm, cw), F32),
                        pltpu.VMEM((tm, wdown.shape[0]), BF16)],
        compiler_params=pltpu.CompilerParams(dimension_semantics=("arbitrary", "arbitrary"),
                                             vmem_limit_bytes=VMEM_LIMIT),
        name="mix_ffn",
    )(x, ret, olat, carry_in, wuv, wout, gffn, wup, convw, convb, wdown, gfin)


def _rope_tables(pos):
    pos = np.asarray(pos, np.float64)[:, None]

    def cs(dim):
        inv = ROPE_THETA ** (-np.arange(0, dim, 2, dtype=np.float64) / dim)
        ang = pos * inv[None, :]
        return np.cos(ang), np.sin(ang)

    c, s = cs(RET_DK)
    c128 = np.concatenate([c, c], axis=-1)
    s128 = np.concatenate([-s, s], axis=-1)
    c, s = cs(QK_ROPE)
    z32 = np.zeros_like(s)
    z64 = np.zeros((pos.shape[0], LANES - QK_ROPE))
    c64 = np.concatenate([c, c, z64], axis=-1)
    s64a = np.concatenate([-s, z32, z64], axis=-1)
    s64b = np.concatenate([z32, s, z64], axis=-1)
    return tuple(np.asarray(t, np.float32) for t in (c128, s128, c64, s64a, s64b))


def kernel(x_prompt, x_sample, cache_kv_latent, cache_k_rope, state_retention, state_ffn_conv,
           page_table, meta_tokens, g_mix, w_in, g_q, w_uq, g_kv, w_uk, w_uv, w_out,
           g_ffn, w_up, conv_w, conv_b, w_down, g_final):
    nb, seq, d = x_prompt.shape
    db, ls, _ = x_sample.shape
    depth = w_in.shape[0]
    assert depth == 1, "single-layer step"
    n_pages = page_table.shape[1]
    page = cache_kv_latent.shape[2]
    past_len = n_pages * page
    d_ff = w_down.shape[1]
    l = 0

    w = w_in[l]
    win = jnp.concatenate([w, jnp.zeros((d, LANES - QK_ROPE), w.dtype)], axis=1).astype(BF16)
    wq = w_uq[l].reshape(Q_LORA, MLA_HEADS, QK_NOPE + QK_ROPE)
    wuq = jnp.concatenate([wq, jnp.zeros((Q_LORA, MLA_HEADS, LANES - QK_ROPE), wq.dtype)],
                          axis=-1).reshape(Q_LORA, MLA_HEADS * 2 * LANES).astype(BF16)
    wuk = jnp.transpose(w_uk[l], (1, 2, 0)).astype(BF16)
    wuv = jnp.transpose(w_uv[l], (1, 0, 2)).astype(BF16)
    proj_w = (g_mix[l][None, :], win, g_q[l][None, :], wuq, g_kv[l][None, :], wuk)
    ffn_w = (wuv, w_out[l].astype(BF16), g_ffn[l][None, :], w_up[l].astype(BF16),
             conv_w[l], conv_b[l][None, :], w_down[l].astype(BF16), g_final[None, :])

    tile = 128
    xm = jnp.concatenate([meta_tokens.astype(F32), jnp.zeros((tile - N_META, d), F32)], axis=0)
    tabs_m = _rope_tables(np.arange(tile))
    qr, kr, vr, gr, qcat_m, kvcat_m, ckv_m, kpe_m = _project(xm, tabs_m, proj_w, tile)
    zero_state = jnp.zeros((1, RET_HEADS, RET_DK, RET_DV), F32)
    ret_m, state_m = _retention_chunks(qr, kr, vr, gr, zero_state, 1, tile, N_META)
    ret_m = ret_m.reshape(tile, RET_W)
    row_valid = (jnp.arange(tile) < N_META)[:, None]
    kmeta = jnp.where(row_valid, kvcat_m, jnp.zeros_like(kvcat_m))
    olat_m = _mla_meta(qcat_m, kmeta)
    _, carry_m = _mix_ffn(xm, ret_m, olat_m, jnp.zeros((8, 2 * d_ff), F32), ffn_w,
                          1, tile, 1, N_META, 256)

    tm = 512
    xp = x_prompt.reshape(nb * seq, d)
    tabs_p = _rope_tables(N_META + np.arange(seq))
    qr_p, kr_p, vr_p, gr_p, qcat, kvcat, kv_p, pe_p = _project(
        xp, tabs_p, proj_w, tm, prefix=(ckv_m[:N_META], kpe_m[:N_META]))
    ns = db * ls
    xs = x_sample.reshape(ns, d)
    pos_s = past_len + np.arange(ls)
    tabs_s = tuple(np.tile(t, (db, 1)) for t in _rope_tables(pos_s))
    qr, kr, vr, gr, qcat_s, _, ckv_s, kpe_s = _project(xs, tabs_s, proj_w, min(ns, 256))

    ret_s, state_s = _retention_decode(qr, kr, vr, gr, state_retention[l], ls, min(ns, 64))
    q_s = jnp.transpose(qcat_s.reshape(MLA_HEADS, db, ls, QCAT_W), (1, 2, 0, 3)).reshape(
        db, ls * MLA_HEADS, QCAT_W)
    cache_pe_t = jnp.swapaxes(cache_k_rope[l], 1, 2)
    decode_args = (page_table, q_s, ckv_s.reshape(db, ls, KV_LORA), kpe_s.reshape(db, ls, QK_ROPE),
                   cache_kv_latent[l], cache_pe_t, min(1024, past_len))
    if nb * (seq // RET_CHUNK) == db:
        as3 = lambda a: a.reshape(nb, seq, RET_W)
        olat_s, ret_p, state_p = _mla_decode(
            *decode_args, side=(as3(qr_p), as3(kr_p), as3(vr_p), as3(gr_p), state_m, RET_CHUNK))
    else:
        olat_s = _mla_decode(*decode_args)
        ret_p, state_p = _retention_chunks(qr_p, kr_p, vr_p, gr_p, state_m, nb, RET_CHUNK, RET_CHUNK)
    ret_p = ret_p.reshape(nb * seq, RET_W)

    olat_p = _mla_prompt(qcat, kvcat, kmeta, nb, 512, 1024)
    y_p, carry_p = _mix_ffn(xp, ret_p, olat_p, carry_m[0], ffn_w, nb, 512, 1, 512, 256)

    olat_s = olat_s.reshape(ns, MLA_HEADS * KV_LORA)
    tmaj = lambda a: jnp.transpose(a.reshape(db, ls, a.shape[-1]), (1, 0, 2)).reshape(ns, a.shape[-1])
    carry_s_in = jnp.transpose(state_ffn_conv[l], (1, 0, 2)).reshape((CONV_W - 1) * db, 2 * d_ff)
    y_s, carry_s = _mix_ffn(tmaj(xs), tmaj(ret_s), tmaj(olat_s), carry_s_in, ffn_w,
                            1, ns, db, ns, 256)
    y_s = jnp.transpose(y_s.reshape(ls, db, d), (1, 0, 2))
    conv_s = jnp.transpose(carry_s.reshape(CONV_W - 1, db, 2 * d_ff), (1, 0, 2))

    return (y_p.reshape(nb, seq, d), y_s,
            kv_p[None], pe_p[None],
            state_p[None],
            carry_p[:, 8 - (CONV_W - 1):, :][None],
            ckv_s.reshape(db, ls, KV_LORA)[None], kpe_s.reshape(db, ls, QK_ROPE)[None],
            state_s[None], conv_s[None])
```

```python
import functools
import math

import jax
import jax.numpy as jnp
import numpy as np
from jax import lax
from jax.experimental import pallas as pl
from jax.experimental.pallas import tpu as pltpu

F32 = jnp.float32
BF16 = jnp.bfloat16

N_META = 16
RET_HEADS = 4
RET_DK = 128
RET_DV = 128
MLA_HEADS = 4
Q_LORA = 384
KV_LORA = 256
QK_NOPE = 128
QK_ROPE = 64
CONV_W = 3
ROPE_THETA = 10000.0
EPS = 1e-6
SOFTMAX_SCALE = (QK_NOPE + QK_ROPE) ** -0.5
RET_W = RET_HEADS * RET_DV
LANES = 128
ROW_TILE = 16
GROUP_LANES = 512
RET_CHUNK = 128
MXU_TILE = 256
META_ROWS = 128
PROJ_ROWS = 512
FFN_ROWS = 512
ATTN_Q_ROWS = 512
ATTN_K_ROWS = 1024
DECODE_KEY_CHUNK = 1024
DECODE_RET_ROWS = 64
QCAT_W = KV_LORA + LANES
LOG_GAMMA = tuple(math.log1p(-(2.0 ** (-5.0 - h))) for h in range(RET_HEADS))
VMEM_LIMIT = 56 * 1024 * 1024
NEG_INF = float("-inf")


def _const_spec(shape):
    nd = len(shape)
    return pl.BlockSpec(shape, lambda *_: (0,) * nd, pipeline_mode=pl.Buffered(1))


def _rms(x, g):
    return x * lax.rsqrt(jnp.mean(x * x, axis=-1, keepdims=True) + EPS) * g


def _proj_kernel(x_ref, c128_ref, s128_ref, c64_ref, s64a_ref, s64b_ref,
                 gmix_ref, win_ref, gq_ref, wuq_ref, gkv_ref, wuk_ref, *rest, tiles_per_seq):
    if tiles_per_seq:
        pre_ckv_ref, pre_kpe_ref = rest[:2]
        rest = rest[2:]
    qr_ref, kr_ref, vr_ref, gr_ref, qcat_ref, kvcat_ref, ckv_ref, kpe_ref = rest
    a = _rms(x_ref[...], gmix_ref[...]).astype(BF16)
    o = 4 * RET_W
    z_lat = jnp.dot(a, win_ref[:, o:], preferred_element_type=F32)
    z = jnp.dot(a, win_ref[:, 0:o], preferred_element_type=F32)
    c128 = c128_ref[...]
    s128 = s128_ref[...]
    c64 = c64_ref[...]
    s64a = s64a_ref[...]
    s64b = s64b_ref[...]

    def rope128(v):
        return v * c128 + pltpu.roll(v, 64, 1) * s128

    def rope64(v):
        return v * c64 + pltpu.roll(v, 96, 1) * s64a + pltpu.roll(v, 32, 1) * s64b

    cqn = _rms(z_lat[:, 0:Q_LORA], gq_ref[...])
    q2 = jnp.dot(cqn.astype(BF16), wuq_ref[...], preferred_element_type=F32)
    for h in range(MLA_HEADS):
        b0 = h * 2 * LANES
        nope = q2[:, b0:b0 + QK_NOPE]
        pe = rope64(q2[:, b0 + LANES:b0 + 2 * LANES])
        qlat = jnp.dot(nope.astype(BF16), wuk_ref[h], preferred_element_type=F32)
        qcat_ref[h, :, 0:KV_LORA] = (qlat * SOFTMAX_SCALE).astype(BF16)
        qcat_ref[h, :, KV_LORA:QCAT_W] = (pe * SOFTMAX_SCALE).astype(BF16)

    ckvn = _rms(z_lat[:, Q_LORA:Q_LORA + KV_LORA], gkv_ref[...])
    kp = rope64(z_lat[:, Q_LORA + KV_LORA:Q_LORA + KV_LORA + LANES])
    kvcat_ref[:, 0:KV_LORA] = ckvn.astype(BF16)
    kvcat_ref[:, KV_LORA:QCAT_W] = kp.astype(BF16)

    for h in range(RET_HEADS):
        sl = slice(h * RET_DK, (h + 1) * RET_DK)
        qr_ref[:, sl] = rope128(z[:, sl]).astype(BF16)
        ksl = slice(RET_W + h * RET_DK, RET_W + (h + 1) * RET_DK)
        kr_ref[:, sl] = (rope128(z[:, ksl]) * (RET_DK ** -0.5)).astype(BF16)
    vr_ref[...] = z[:, 2 * RET_W:3 * RET_W].astype(BF16)
    gr_ref[...] = z[:, 3 * RET_W:4 * RET_W].astype(BF16)

    if tiles_per_seq:
        tm = x_ref.shape[0]
        n_pre = pre_ckv_ref.shape[0]
        t = pl.program_id(0) % tiles_per_seq

        @pl.when(t == 0)
        def _():
            ckv_ref[0, 0:n_pre, :] = pre_ckv_ref[...]
            kpe_ref[0, 0:n_pre, :] = pre_kpe_ref[...]

        rows = pl.ds(pl.multiple_of(n_pre + t * tm, 8), tm)
        ckv_ref[0, rows, :] = ckvn
        kpe_ref[0, rows, :] = kp[:, 0:QK_ROPE]
    else:
        ckv_ref[...] = ckvn
        kpe_ref[...] = kp[:, 0:QK_ROPE]


def _project(x, tabs, wts, tm, prefix=None):
    n, d = x.shape
    nt = tabs[0].shape[0] // tm
    gmix, win, gq, wuq, gkv, wuk = wts
    row = lambda w: pl.BlockSpec((tm, w), lambda i: (i, 0))
    tab = pl.BlockSpec((tm, LANES), lambda i: (i % nt, 0))
    if prefix is None:
        key_shapes = (jax.ShapeDtypeStruct((n, KV_LORA), F32), jax.ShapeDtypeStruct((n, QK_ROPE), F32))
        key_specs = (row(KV_LORA), row(QK_ROPE))
        extra, extra_specs = (), []
    else:
        t_all = prefix[0].shape[0] + nt * tm
        nseq = n // (nt * tm)
        seq_blk = lambda w: pl.BlockSpec((1, t_all, w), lambda i: (i // nt, 0, 0))
        key_shapes = (jax.ShapeDtypeStruct((nseq, t_all, KV_LORA), F32),
                      jax.ShapeDtypeStruct((nseq, t_all, QK_ROPE), F32))
        key_specs = (seq_blk(KV_LORA), seq_blk(QK_ROPE))
        extra, extra_specs = tuple(prefix), [_const_spec(p.shape) for p in prefix]
    out_shapes = (
        jax.ShapeDtypeStruct((n, RET_W), BF16), jax.ShapeDtypeStruct((n, RET_W), BF16),
        jax.ShapeDtypeStruct((n, RET_W), BF16), jax.ShapeDtypeStruct((n, RET_W), BF16),
        jax.ShapeDtypeStruct((MLA_HEADS, n, QCAT_W), BF16),
        jax.ShapeDtypeStruct((n, QCAT_W), BF16)) + key_shapes
    out_specs = (row(RET_W), row(RET_W), row(RET_W), row(RET_W),
                 pl.BlockSpec((MLA_HEADS, tm, QCAT_W), lambda i: (0, i, 0)),
                 row(QCAT_W)) + key_specs
    return pl.pallas_call(
        functools.partial(_proj_kernel, tiles_per_seq=0 if prefix is None else nt),
        grid=(n // tm,),
        in_specs=[row(d), tab, tab, tab, tab, tab,
                  _const_spec(gmix.shape), _const_spec(win.shape), _const_spec(gq.shape),
                  _const_spec(wuq.shape), _const_spec(gkv.shape), _const_spec(wuk.shape)]
        + extra_specs,
        out_specs=out_specs,
        out_shape=out_shapes,
        compiler_params=pltpu.CompilerParams(dimension_semantics=("arbitrary",),
                                             vmem_limit_bytes=VMEM_LIMIT),
        name="proj",
    )(x, *tabs, gmix, win, gq, wuq, gkv, wuk, *extra)


def _ret_chunk_steps(q_ref, k_ref, v_ref, g_ref, o_ref, st_ref, lv):
    L = q_ref.shape[0]
    li = lax.broadcasted_iota(jnp.int32, (L, L), 0)
    mi = lax.broadcasted_iota(jnp.int32, (L, L), 1)
    diff = (li - mi).astype(F32)
    n = lax.broadcasted_iota(jnp.int32, (L, 1), 0).astype(F32)
    for h in range(RET_HEADS):
        lg = LOG_GAMMA[h]
        sl = slice(h * RET_DK, (h + 1) * RET_DK)
        decay = jnp.where(diff >= 0, jnp.exp(lg * jnp.maximum(diff, 0.0)), 0.0)
        qdec = jnp.exp(lg * (n + 1.0))
        kdec = jnp.where(n < lv, jnp.exp(lg * jnp.maximum(lv - 1.0 - n, 0.0)), 0.0)
        q = q_ref[:, sl]
        k = k_ref[:, sl]
        v = v_ref[:, sl]
        s = lax.dot_general(q, k, (((1,), (1,)), ((), ())), preferred_element_type=F32) * decay
        state = st_ref[h]
        qd = (q.astype(F32) * qdec).astype(BF16)
        cross = jnp.dot(qd, state.astype(BF16), preferred_element_type=F32)
        yield
        inner = jnp.dot(s.astype(BF16), v, preferred_element_type=F32)
        o = inner + cross
        kd = (k.astype(F32) * kdec).astype(BF16)
        st_ref[h] = math.exp(lg * lv) * state + lax.dot_general(
            kd, v, (((0,), (0,)), ((), ())), preferred_element_type=F32)
        of = o * lax.rsqrt(jnp.mean(o * o, axis=-1, keepdims=True) + EPS)
        g = g_ref[:, sl].astype(F32)
        o_ref[:, sl] = (of * (g * jax.nn.sigmoid(g))).astype(BF16)
        yield


def _ret_chunk(*args):
    for _ in _ret_chunk_steps(*args):
        pass


def _ret_chunk_kernel(q_ref, k_ref, v_ref, g_ref, s0_ref, o_ref, sf_ref, st_ref, *, lv):
    c = pl.program_id(0)
    nb = q_ref.shape[0]

    @pl.when(c == 0)
    def _():
        for b in range(nb):
            st_ref[b] = s0_ref[0]

    for b in range(nb):
        _ret_chunk(q_ref.at[b], k_ref.at[b], v_ref.at[b], g_ref.at[b], o_ref.at[b],
                   st_ref.at[b], lv)

    @pl.when(c == pl.num_programs(0) - 1)
    def _():
        sf_ref[...] = st_ref[...]


def _retention_chunks(qr, kr, vr, gr, s0, nb, L, lv):
    n = qr.shape[0]
    t = n // nb
    as3 = lambda a: a.reshape(nb, t, RET_W)
    blk = pl.BlockSpec((nb, L, RET_W), lambda c: (0, c, 0))
    return pl.pallas_call(
        functools.partial(_ret_chunk_kernel, lv=float(lv)),
        grid=(t // L,),
        in_specs=[blk, blk, blk, blk, _const_spec(s0.shape)],
        out_specs=(blk, pl.BlockSpec((nb, RET_HEADS, RET_DK, RET_DV), lambda c: (0, 0, 0, 0))),
        out_shape=(jax.ShapeDtypeStruct((nb, t, RET_W), BF16),
                   jax.ShapeDtypeStruct((nb, RET_HEADS, RET_DK, RET_DV), F32)),
        scratch_shapes=[pltpu.VMEM((nb, RET_HEADS, RET_DK, RET_DV), F32)],
        compiler_params=pltpu.CompilerParams(dimension_semantics=("arbitrary",),
                                             vmem_limit_bytes=VMEM_LIMIT),
        name="ret_chunks",
    )(as3(qr), as3(kr), as3(vr), as3(gr), s0)


def _ret_decode_kernel(q_ref, k_ref, v_ref, g_ref, s_ref, o_ref, sn_ref, *, ls):
    R = q_ref.shape[0]
    per_tile = ROW_TILE // ls
    li = lax.broadcasted_iota(jnp.int32, (R, R), 0)
    mi = lax.broadcasted_iota(jnp.int32, (R, R), 1)
    same = (li // ls) == (mi // ls)
    diff = ((li % ls) - (mi % ls)).astype(F32)
    t_col = (lax.broadcasted_iota(jnp.int32, (R, 1), 0) % ls).astype(F32)
    seq_in_tile = lax.broadcasted_iota(jnp.int32, (ROW_TILE, 1), 0) // ls
    for h in range(RET_HEADS):
        lg = LOG_GAMMA[h]
        sl = slice(h * RET_DK, (h + 1) * RET_DK)
        q = q_ref[:, sl]
        k = k_ref[:, sl]
        v = v_ref[:, sl]
        decay = jnp.where(same & (diff >= 0), jnp.exp(lg * jnp.maximum(diff, 0.0)), 0.0)
        s = lax.dot_general(q, k, (((1,), (1,)), ((), ())), preferred_element_type=F32) * decay
        inner = jnp.dot(s.astype(BF16), v, preferred_element_type=F32)
        qd = q.astype(F32) * jnp.exp(lg * (t_col + 1.0))
        kd = k.astype(F32) * jnp.exp(lg * (ls - 1.0 - t_col))
        sdec = math.exp(lg * ls)
        cross_tiles = []
        for tt in range(R // ROW_TILE):
            rows = slice(tt * ROW_TILE, (tt + 1) * ROW_TILE)
            qd_t = qd[rows].astype(BF16)
            kd_t = kd[rows]
            v_t = v[rows]
            acc = jnp.zeros((ROW_TILE, RET_DV), F32)
            for j in range(per_tile):
                b = tt * per_tile + j
                state = s_ref[b, h]
                mine = seq_in_tile == j
                cr = jnp.dot(qd_t, state.astype(BF16), preferred_element_type=F32)
                acc = jnp.where(mine, cr, acc)
                kdm = jnp.where(mine, kd_t, 0.0).astype(BF16)
                sn_ref[b, h] = sdec * state + lax.dot_general(
                    kdm, v_t, (((0,), (0,)), ((), ())), preferred_element_type=F32)
            cross_tiles.append(acc)
        o = inner + jnp.concatenate(cross_tiles, axis=0)
        of = o * lax.rsqrt(jnp.mean(o * o, axis=-1, keepdims=True) + EPS)
        g = g_ref[:, sl].astype(F32)
        o_ref[:, sl] = (of * (g * jax.nn.sigmoid(g))).astype(BF16)


def _retention_decode(qr, kr, vr, gr, state, ls, rows):
    n = qr.shape[0]
    nseq = rows // ls
    row = pl.BlockSpec((rows, RET_W), lambda i: (i, 0))
    st = pl.BlockSpec((nseq, RET_HEADS, RET_DK, RET_DV), lambda i: (i, 0, 0, 0))
    return pl.pallas_call(
        functools.partial(_ret_decode_kernel, ls=ls),
        grid=(n // rows,),
        in_specs=[row, row, row, row, st],
        out_specs=(row, st),
        out_shape=(jax.ShapeDtypeStruct((n, RET_W), BF16),
                   jax.ShapeDtypeStruct(state.shape, F32)),
        compiler_params=pltpu.CompilerParams(dimension_semantics=("arbitrary",),
                                             vmem_limit_bytes=VMEM_LIMIT),
        name="ret_decode",
    )(qr, kr, vr, gr, state)


def _mla_prompt_kernel(q_ref, k_ref, km_ref, o_ref, vt_ref, vmt_ref, m_ref, l_ref, acc_ref, *, tq, tk):
    assert tk % tq == 0 and tk <= 2 * tq
    i = pl.program_id(1)
    nkm = km_ref.shape[0]
    nq = MLA_HEADS * tq

    @pl.when(i == 0)
    def _():
        vmt_ref[...] = km_ref[:, 0:KV_LORA].T
        for c in range(k_ref.shape[0] // tq):
            vt_ref[:, c * tq:(c + 1) * tq] = k_ref[c * tq:(c + 1) * tq, 0:KV_LORA].T

    q_all = q_ref[...].reshape(nq, QCAT_W)
    groups = [slice(g * GROUP_LANES, (g + 1) * GROUP_LANES) for g in range(nq // GROUP_LANES)]

    def scores(kc, grp):
        return lax.dot_general(kc, q_all[grp], (((1,), (1,)), ((), ())),
                               preferred_element_type=F32)

    s = scores(km_ref[...], slice(0, nq))
    s = jnp.where(lax.broadcasted_iota(jnp.int32, (nkm, nq), 0) < N_META, s, NEG_INF)
    m = jnp.max(s, axis=0, keepdims=True)
    p = jnp.exp(s - m)
    m_ref[...] = m
    l_ref[...] = jnp.sum(p, axis=0, keepdims=True)
    acc_ref[...] = jnp.dot(vmt_ref[...], p.astype(BF16), preferred_element_type=F32)

    def update(start, size, mask):
        kc = k_ref[pl.ds(start, size), :]
        vt = vt_ref[:, pl.ds(start, size)]
        ss = [scores(kc, grp) for grp in groups]
        for grp, s in zip(groups, ss):
            if mask is not None:
                s = jnp.where(mask[:, grp], s, NEG_INF)
            m_prev = m_ref[:, grp]
            m_new = jnp.maximum(m_prev, jnp.max(s, axis=0, keepdims=True))
            alpha = jnp.exp(m_prev - m_new)
            p = jnp.exp(s - m_new)
            m_ref[:, grp] = m_new
            l_ref[:, grp] = alpha * l_ref[:, grp] + jnp.sum(p, axis=0, keepdims=True)
            acc_ref[:, grp] = alpha * acc_ref[:, grp] + jnp.dot(
                vt, p.astype(BF16), preferred_element_type=F32)

    n_vis = i * tq

    def body(j, carry):
        update(pl.multiple_of(j * tk, tk), tk, None)
        return carry

    lax.fori_loop(0, n_vis // tk, body, 0)
    if tk > tq:
        @pl.when(n_vis % tk != 0)
        def _():
            update(pl.multiple_of(n_vis - tq, tq), tq, None)

    key = lax.broadcasted_iota(jnp.int32, (tq, nq), 0)
    qry = lax.broadcasted_iota(jnp.int32, (tq, nq), 1) % tq
    update(pl.multiple_of(n_vis, tq), tq, key <= qry)
    out = acc_ref[...] / l_ref[...]
    for h in range(MLA_HEADS):
        o_ref[:, h * KV_LORA:(h + 1) * KV_LORA] = out[:, h * tq:(h + 1) * tq].T.astype(BF16)


def _mla_prompt(qcat, kvcat, kmeta, nb, tq, tk):
    n = kvcat.shape[0]
    t = n // nb
    nq = t // tq
    return pl.pallas_call(
        functools.partial(_mla_prompt_kernel, tq=tq, tk=tk),
        grid=(nb, nq),
        in_specs=[pl.BlockSpec((MLA_HEADS, tq, QCAT_W), lambda b, i: (0, b * nq + i, 0)),
                  pl.BlockSpec((t, QCAT_W), lambda b, i: (b, 0)),
                  _const_spec(kmeta.shape)],
        out_specs=pl.BlockSpec((tq, MLA_HEADS * KV_LORA), lambda b, i: (b * nq + i, 0)),
        out_shape=jax.ShapeDtypeStruct((n, MLA_HEADS * KV_LORA), BF16),
        scratch_shapes=[pltpu.VMEM((KV_LORA, t), BF16), pltpu.VMEM((KV_LORA, kmeta.shape[0]), BF16),
                        pltpu.VMEM((1, MLA_HEADS * tq), F32), pltpu.VMEM((1, MLA_HEADS * tq), F32),
                        pltpu.VMEM((KV_LORA, MLA_HEADS * tq), F32)],
        compiler_params=pltpu.CompilerParams(dimension_semantics=("arbitrary", "arbitrary"),
                                             vmem_limit_bytes=VMEM_LIMIT),
        name="mla_prompt",
    )(qcat, kvcat, kmeta)


def _mla_meta_kernel(q_ref, km_ref, o_ref):
    km = km_ref[...]
    r = q_ref.shape[1]
    row = lax.broadcasted_iota(jnp.int32, (r, km.shape[0]), 0)
    col = lax.broadcasted_iota(jnp.int32, (r, km.shape[0]), 1)
    mask = (col <= row) & (col < N_META)
    for h in range(MLA_HEADS):
        s = lax.dot_general(q_ref[h], km, (((1,), (1,)), ((), ())), preferred_element_type=F32)
        s = jnp.where(mask, s, NEG_INF)
        p = jnp.exp(s - jnp.max(s, axis=-1, keepdims=True))
        acc = jnp.dot(p.astype(BF16), km[:, 0:KV_LORA], preferred_element_type=F32)
        o_ref[:, h * KV_LORA:(h + 1) * KV_LORA] = (
            acc / jnp.sum(p, axis=-1, keepdims=True)).astype(BF16)


def _mla_meta(qcat, kmeta):
    r = qcat.shape[1]
    return pl.pallas_call(
        _mla_meta_kernel,
        out_shape=jax.ShapeDtypeStruct((r, MLA_HEADS * KV_LORA), BF16),
        compiler_params=pltpu.CompilerParams(vmem_limit_bytes=VMEM_LIMIT),
        name="mla_meta",
    )(qcat, kmeta)


def _mla_decode_kernel(pt_ref, q_ref, kself_ref, peself_ref, kv_hbm, pe_hbm, *rest,
                       n_pages, page, ls, chunk, side_lv):
    if side_lv is None:
        o_ref, kvbuf, pebuf, kb16, pb16, s_ref, sem = rest
    else:
        (rq_ref, rk_ref, rv_ref, rg_ref, s0_ref, o_ref, ro_ref, sf_ref,
         kvbuf, pebuf, kb16, pb16, s_ref, sem, st_ref) = rest
    b = pl.program_id(0)
    nb = pl.num_programs(0)
    slot = b % 2
    n_keys = n_pages * page

    def page_copies(seq, sl):
        out = []
        for j in range(n_pages):
            pid = pt_ref[seq, j]
            out.append(pltpu.make_async_copy(kv_hbm.at[pid], kvbuf.at[sl, pl.ds(j * page, page)],
                                             sem.at[0, sl]))
            out.append(pltpu.make_async_copy(pe_hbm.at[pid], pebuf.at[sl, j], sem.at[1, sl]))
        return out

    @pl.when(b == 0)
    def _():
        for c in page_copies(0, 0):
            c.start()

    @pl.when(b + 1 < nb)
    def _():
        for c in page_copies(b + 1, 1 - slot):
            c.start()

    for c in page_copies(b, slot):
        c.wait()

    q = q_ref[0]
    rows = q.shape[0]
    ql = q[:, 0:KV_LORA]
    qp = q[:, KV_LORA:KV_LORA + QK_ROPE]
    ppc = chunk // page
    n_chunks = n_keys // chunk
    side = iter(())
    if side_lv is not None:
        n_seq = st_ref.shape[0]
        seq = b % n_seq

        @pl.when(b < n_seq)
        def _():
            st_ref[seq] = s0_ref[0]

        side = _ret_chunk_steps(rq_ref.at[0], rk_ref.at[0], rv_ref.at[0], rg_ref.at[0],
                                ro_ref.at[0], st_ref.at[seq], side_lv)
    for c in range(n_chunks):
        next(side, None)
        ksl = slice(c * chunk, (c + 1) * chunk)
        kb16[ksl, :] = kvbuf[slot, ksl, :].astype(BF16)
        for j in range(ppc):
            jj = c * ppc + j
            pb16[:, jj * page:(jj + 1) * page] = pebuf[slot, jj].astype(BF16)
        s_ref[:, ksl] = (
            lax.dot_general(ql, kb16[ksl, :], (((1,), (1,)), ((), ())), preferred_element_type=F32)
            + jnp.dot(qp, pb16[:, ksl], preferred_element_type=F32))

    parts = []
    n_split = 4 if n_chunks % 4 == 0 else 1
    for h in range(n_split):
        hsl = slice(h * n_keys // n_split, (h + 1) * n_keys // n_split)
        s_h = s_ref[:, hsl]
        m_h = jnp.max(s_h, axis=-1, keepdims=True)
        p_h = jnp.exp(s_h - m_h)
        parts.append((m_h, jnp.sum(p_h, axis=-1, keepdims=True),
                      jnp.dot(p_h.astype(BF16), kb16[hsl, :], preferred_element_type=F32)))

    qf = q.astype(F32)
    kvs = kself_ref[0]
    pes = peself_ref[0]
    t_row = lax.broadcasted_iota(jnp.int32, (rows, 1), 0) // MLA_HEADS
    ss = []
    for t2 in range(ls):
        s_t = (jnp.sum(qf[:, 0:KV_LORA] * kvs[t2:t2 + 1, :], axis=-1, keepdims=True)
               + jnp.sum(qf[:, KV_LORA:KV_LORA + QK_ROPE] * pes[t2:t2 + 1, :], axis=-1, keepdims=True))
        ss.append(jnp.where(t_row >= t2, s_t, NEG_INF))
    m_s = ss[0]
    for s_t in ss[1:]:
        m_s = jnp.maximum(m_s, s_t)
    l_s = jnp.zeros((rows, 1), F32)
    acc_s = jnp.zeros((rows, KV_LORA), F32)
    for t2 in range(ls):
        p_t = jnp.exp(ss[t2] - m_s)
        l_s = l_s + p_t
        acc_s = acc_s + p_t * kvs[t2:t2 + 1, :]
    parts.append((m_s, l_s, acc_s))

    m = parts[0][0]
    for m_h, _, _ in parts[1:]:
        m = jnp.maximum(m, m_h)
    l = jnp.zeros((rows, 1), F32)
    acc = jnp.zeros((rows, KV_LORA), F32)
    for m_h, l_h, acc_h in parts:
        w_h = jnp.exp(m_h - m)
        l = l + w_h * l_h
        acc = acc + w_h * acc_h
    o_ref[0] = (acc / l).astype(BF16)

    for _ in side:
        pass
    if side_lv is not None:
        @pl.when(b == nb - 1)
        def _():
            sf_ref[...] = st_ref[...]


def _mla_decode(page_table, q, kself, peself, cache_kv, cache_pe_t, chunk, side=None):
    db, rows, _ = q.shape
    ls = kself.shape[1]
    n_pages = page_table.shape[1]
    page = cache_kv.shape[1]
    n_keys = n_pages * page
    in_specs = [pl.BlockSpec((1, rows, QCAT_W), lambda b, pt: (b, 0, 0)),
                pl.BlockSpec((1, ls, KV_LORA), lambda b, pt: (b, 0, 0)),
                pl.BlockSpec((1, ls, QK_ROPE), lambda b, pt: (b, 0, 0)),
                pl.BlockSpec(memory_space=pl.ANY),
                pl.BlockSpec(memory_space=pl.ANY)]
    out_specs = [pl.BlockSpec((1, rows, KV_LORA), lambda b, pt: (b, 0, 0))]
    out_shape = [jax.ShapeDtypeStruct((db, rows, KV_LORA), BF16)]
    scratch = [pltpu.VMEM((2, n_keys, KV_LORA), F32),
               pltpu.VMEM((2, n_pages, QK_ROPE, page), F32),
               pltpu.VMEM((n_keys, KV_LORA), BF16),
               pltpu.VMEM((QK_ROPE, n_keys), BF16),
               pltpu.VMEM((rows, n_keys), F32),
               pltpu.SemaphoreType.DMA((2, 2))]
    operands = [page_table, q, kself, peself, cache_kv, cache_pe_t]
    side_lv = None
    if side is not None:
        qr, kr, vr, gr, s0, L = side
        n_seq, t, _ = qr.shape
        assert n_seq * (t // L) == db, "one retention chunk per decode step"
        blk = pl.BlockSpec((1, L, RET_W), lambda b, pt: (b % n_seq, b // n_seq, 0))
        st_shape = (n_seq, RET_HEADS, RET_DK, RET_DV)
        in_specs += [blk, blk, blk, blk, pl.BlockSpec(s0.shape, lambda b, pt: (0, 0, 0, 0))]
        out_specs += [blk, pl.BlockSpec(st_shape, lambda b, pt: (0, 0, 0, 0))]
        out_shape += [jax.ShapeDtypeStruct((n_seq, t, RET_W), BF16),
                      jax.ShapeDtypeStruct(st_shape, F32)]
        scratch += [pltpu.VMEM(st_shape, F32)]
        operands += [qr, kr, vr, gr, s0]
        side_lv = float(L)
    grid_spec = pltpu.PrefetchScalarGridSpec(
        num_scalar_prefetch=1, grid=(db,), in_specs=in_specs, out_specs=out_specs,
        scratch_shapes=scratch)
    out = pl.pallas_call(
        functools.partial(_mla_decode_kernel, n_pages=n_pages, page=page, ls=ls, chunk=chunk,
                          side_lv=side_lv),
        grid_spec=grid_spec,
        out_shape=out_shape,
        compiler_params=pltpu.CompilerParams(dimension_semantics=("arbitrary",),
                                             vmem_limit_bytes=VMEM_LIMIT),
        name="mla_decode",
    )(*operands)
    return out[0] if side is None else tuple(out)


def _mix_ffn_kernel(x_ref, ret_ref, olat_ref, cin_ref, wuv_ref, wout_ref, gffn_ref, wup_ref,
                    cw_ref, cb_ref, wdown_ref, gfin_ref, y_ref, cout_ref,
                    carry_ref, stage_ref, hmid_ref, *, stride, carry_end, cw):
    t = pl.program_id(1)
    tm = x_ref.shape[0]
    hal = carry_ref.shape[0]
    d_ff = wdown_ref.shape[0]

    @pl.when(t == 0)
    def _():
        carry_ref[...] = cin_ref[...]

    mla = jnp.concatenate(
        [jnp.dot(olat_ref[:, h * KV_LORA:(h + 1) * KV_LORA], wuv_ref[h],
                 preferred_element_type=F32).astype(BF16) for h in range(MLA_HEADS)], axis=1)
    mixed = jnp.concatenate([ret_ref[...], mla], axis=1)
    h1 = x_ref[...] + jnp.dot(mixed, wout_ref[...], preferred_element_type=F32)
    a2 = _rms(h1, gffn_ref[...]).astype(BF16)

    def conv_half(c0, slot):
        u = jnp.dot(a2, wup_ref[:, c0:c0 + cw], preferred_element_type=F32)
        stage_ref[slot, 0:hal, :] = carry_ref[:, c0:c0 + cw]
        stage_ref[slot, hal:hal + tm, :] = u
        carry_ref[:, c0:c0 + cw] = u[carry_end - hal:carry_end, :]
        um1 = stage_ref[slot, hal - stride:hal - stride + tm, :]
        um2 = stage_ref[slot, hal - 2 * stride:hal - 2 * stride + tm, :]
        return (cb_ref[:, c0:c0 + cw] + cw_ref[0:1, c0:c0 + cw] * um2
                + cw_ref[1:2, c0:c0 + cw] * um1 + cw_ref[2:3, c0:c0 + cw] * u)

    n_slots = stage_ref.shape[0]
    for jc in range(d_ff // cw):
        ca = conv_half(jc * cw, (2 * jc) % n_slots)
        cg = conv_half(d_ff + jc * cw, (2 * jc + 1) % n_slots)
        hmid_ref[:, jc * cw:(jc + 1) * cw] = (cg * jax.nn.sigmoid(cg) * ca).astype(BF16)
    h2 = h1 + jnp.dot(hmid_ref[...], wdown_ref[...], preferred_element_type=F32)
    y_ref[...] = _rms(h2, gfin_ref[...])

    @pl.when(t == pl.num_programs(1) - 1)
    def _():
        cout_ref[0] = carry_ref[...]


def _mix_ffn(x, ret, olat, carry_in, wts, nseq, tm, stride, carry_end, cw):
    n, d = x.shape
    hal = carry_in.shape[0]
    nt = n // (nseq * tm)
    wuv, wout, gffn, wup, convw, convb, wdown, gfin = wts
    row = lambda w: pl.BlockSpec((tm, w), lambda b, t: (b * nt + t, 0))
    return pl.pallas_call(
        functools.partial(_mix_ffn_kernel, stride=stride, carry_end=carry_end, cw=cw),
        grid=(nseq, nt),
        in_specs=[row(d), row(RET_W), row(MLA_HEADS * KV_LORA), _const_spec(carry_in.shape),
                  _const_spec(wuv.shape), _const_spec(wout.shape), _const_spec(gffn.shape),
                  _const_spec(wup.shape), _const_spec(convw.shape), _const_spec(convb.shape),
                  _const_spec(wdown.shape), _const_spec(gfin.shape)],
        out_specs=(row(d), pl.BlockSpec((1, hal, wup.shape[1]), lambda b, t: (b, 0, 0))),
        out_shape=(jax.ShapeDtypeStruct((n, d), F32),
                   jax.ShapeDtypeStruct((nseq, hal, wup.shape[1]), F32)),
        scratch_shapes=[pltpu.VMEM((hal, wup.shape[1]), F32),
                        pltpu.VMEM((4, hal + tm, cw), F32),
                        pltpu.VMEM((tm, wdown.shape[0]), BF16)],
        compiler_params=pltpu.CompilerParams(dimension_semantics=("arbitrary", "arbitrary"),
                                             vmem_limit_bytes=VMEM_LIMIT),
        name="mix_ffn",
    )(x, ret, olat, carry_in, wuv, wout, gffn, wup, convw, convb, wdown, gfin)


def _rope_tables(pos):
    pos = np.asarray(pos, np.float64)[:, None]

    def cs(dim):
        inv = ROPE_THETA ** (-np.arange(0, dim, 2, dtype=np.float64) / dim)
        ang = pos * inv[None, :]
        return np.cos(ang), np.sin(ang)

    c, s = cs(RET_DK)
    c128 = np.concatenate([c, c], axis=-1)
    s128 = np.concatenate([-s, s], axis=-1)
    c, s = cs(QK_ROPE)
    z32 = np.zeros_like(s)
    z64 = np.zeros((pos.shape[0], LANES - QK_ROPE))
    c64 = np.concatenate([c, c, z64], axis=-1)
    s64a = np.concatenate([-s, z32, z64], axis=-1)
    s64b = np.concatenate([z32, s, z64], axis=-1)
    return tuple(np.asarray(t, np.float32) for t in (c128, s128, c64, s64a, s64b))


def kernel(x_prompt, x_sample, cache_kv_latent, cache_k_rope, state_retention, state_ffn_conv,
           page_table, meta_tokens, g_mix, w_in, g_q, w_uq, g_kv, w_uk, w_uv, w_out,
           g_ffn, w_up, conv_w, conv_b, w_down, g_final):
    nb, seq, d = x_prompt.shape
    db, ls, _ = x_sample.shape
    depth = w_in.shape[0]
    assert depth == 1, "single-layer step"
    n_pages = page_table.shape[1]
    page = cache_kv_latent.shape[2]
    past_len = n_pages * page
    d_ff = w_down.shape[1]
    l = 0

    w = w_in[l]
    win = jnp.concatenate([w, jnp.zeros((d, LANES - QK_ROPE), w.dtype)], axis=1).astype(BF16)
    wq = w_uq[l].reshape(Q_LORA, MLA_HEADS, QK_NOPE + QK_ROPE)
    wuq = jnp.concatenate([wq, jnp.zeros((Q_LORA, MLA_HEADS, LANES - QK_ROPE), wq.dtype)],
                          axis=-1).reshape(Q_LORA, MLA_HEADS * 2 * LANES).astype(BF16)
    wuk = jnp.transpose(w_uk[l], (1, 2, 0)).astype(BF16)
    wuv = jnp.transpose(w_uv[l], (1, 0, 2)).astype(BF16)
    proj_w = (g_mix[l][None, :], win, g_q[l][None, :], wuq, g_kv[l][None, :], wuk)
    ffn_w = (wuv, w_out[l].astype(BF16), g_ffn[l][None, :], w_up[l].astype(BF16),
             conv_w[l], conv_b[l][None, :], w_down[l].astype(BF16), g_final[None, :])

    tile = META_ROWS
    xm = jnp.concatenate([meta_tokens.astype(F32), jnp.zeros((tile - N_META, d), F32)], axis=0)
    tabs_m = _rope_tables(np.arange(tile))
    qr, kr, vr, gr, qcat_m, kvcat_m, ckv_m, kpe_m = _project(xm, tabs_m, proj_w, tile)
    zero_state = jnp.zeros((1, RET_HEADS, RET_DK, RET_DV), F32)
    ret_m, state_m = _retention_chunks(qr, kr, vr, gr, zero_state, 1, tile, N_META)
    ret_m = ret_m.reshape(tile, RET_W)
    row_valid = (jnp.arange(tile) < N_META)[:, None]
    kmeta = jnp.where(row_valid, kvcat_m, jnp.zeros_like(kvcat_m))
    olat_m = _mla_meta(qcat_m, kmeta)
    _, carry_m = _mix_ffn(xm, ret_m, olat_m, jnp.zeros((8, 2 * d_ff), F32), ffn_w,
                          1, tile, 1, N_META, MXU_TILE)

    tm = min(PROJ_ROWS, seq)
    xp = x_prompt.reshape(nb * seq, d)
    tabs_p = _rope_tables(N_META + np.arange(seq))
    qr_p, kr_p, vr_p, gr_p, qcat, kvcat, kv_p, pe_p = _project(
        xp, tabs_p, proj_w, tm, prefix=(ckv_m[:N_META], kpe_m[:N_META]))
    ns = db * ls
    xs = x_sample.reshape(ns, d)
    pos_s = past_len + np.arange(ls)
    tabs_s = tuple(np.tile(t, (db, 1)) for t in _rope_tables(pos_s))
    qr, kr, vr, gr, qcat_s, _, ckv_s, kpe_s = _project(xs, tabs_s, proj_w, min(ns, PROJ_ROWS // 2))

    ret_s, state_s = _retention_decode(qr, kr, vr, gr, state_retention[l], ls, min(ns, DECODE_RET_ROWS))
    q_s = jnp.transpose(qcat_s.reshape(MLA_HEADS, db, ls, QCAT_W), (1, 2, 0, 3)).reshape(
        db, ls * MLA_HEADS, QCAT_W)
    cache_pe_t = jnp.swapaxes(cache_k_rope[l], 1, 2)
    decode_args = (page_table, q_s, ckv_s.reshape(db, ls, KV_LORA), kpe_s.reshape(db, ls, QK_ROPE),
                   cache_kv_latent[l], cache_pe_t, min(DECODE_KEY_CHUNK, past_len))
    if nb * (seq // RET_CHUNK) == db:
        as3 = lambda a: a.reshape(nb, seq, RET_W)
        olat_s, ret_p, state_p = _mla_decode(
            *decode_args, side=(as3(qr_p), as3(kr_p), as3(vr_p), as3(gr_p), state_m, RET_CHUNK))
    else:
        olat_s = _mla_decode(*decode_args)
        ret_p, state_p = _retention_chunks(qr_p, kr_p, vr_p, gr_p, state_m, nb, RET_CHUNK, RET_CHUNK)
    ret_p = ret_p.reshape(nb * seq, RET_W)

    tq = min(ATTN_Q_ROWS, seq)
    olat_p = _mla_prompt(qcat, kvcat, kmeta, nb, tq, min(ATTN_K_ROWS, 2 * tq))
    tf = min(FFN_ROWS, seq)
    y_p, carry_p = _mix_ffn(xp, ret_p, olat_p, carry_m[0], ffn_w, nb, tf, 1, tf, MXU_TILE)

    olat_s = olat_s.reshape(ns, MLA_HEADS * KV_LORA)
    tmaj = lambda a: jnp.transpose(a.reshape(db, ls, a.shape[-1]), (1, 0, 2)).reshape(ns, a.shape[-1])
    carry_s_in = jnp.transpose(state_ffn_conv[l], (1, 0, 2)).reshape((CONV_W - 1) * db, 2 * d_ff)
    y_s, carry_s = _mix_ffn(tmaj(xs), tmaj(ret_s), tmaj(olat_s), carry_s_in, ffn_w,
                            1, ns, db, ns, MXU_TILE)
    y_s = jnp.transpose(y_s.reshape(ls, db, d), (1, 0, 2))
    conv_s = jnp.transpose(carry_s.reshape(CONV_W - 1, db, 2 * d_ff), (1, 0, 2))

    return (y_p.reshape(nb, seq, d), y_s,
            kv_p[None], pe_p[None],
            state_p[None],
            carry_p[:, 8 - (CONV_W - 1):, :][None],
            ckv_s.reshape(db, ls, KV_LORA)[None], kpe_s.reshape(db, ls, QK_ROPE)[None],
            state_s[None], conv_s[None])
```

```python
import functools
import math

import jax
import jax.numpy as jnp
import numpy as np
from jax import lax
from jax.experimental import pallas as pl
from jax.experimental.pallas import tpu as pltpu

F32 = jnp.float32
BF16 = jnp.bfloat16

N_META = 16
RET_HEADS = 4
RET_DK = 128
RET_DV = 128
MLA_HEADS = 4
Q_LORA = 384
KV_LORA = 256
QK_NOPE = 128
QK_ROPE = 64
CONV_W = 3
ROPE_THETA = 10000.0
EPS = 1e-6
SOFTMAX_SCALE = (QK_NOPE + QK_ROPE) ** -0.5
RET_W = RET_HEADS * RET_DV
LANES = 128
ROW_TILE = 16
GROUP_LANES = 512
RET_CHUNK = 128
MXU_TILE = 256
META_ROWS = 128
PROJ_ROWS = 512
FFN_ROWS = 512
ATTN_Q_ROWS = 512
ATTN_K_ROWS = 1024
DECODE_KEY_CHUNK = 1024
DECODE_RET_ROWS = 64
QCAT_W = KV_LORA + LANES
LOG_GAMMA = tuple(math.log1p(-(2.0 ** (-5.0 - h))) for h in range(RET_HEADS))
VMEM_LIMIT = 56 * 1024 * 1024
NEG_INF = float("-inf")


def _const_spec(shape):
    nd = len(shape)
    return pl.BlockSpec(shape, lambda *_: (0,) * nd, pipeline_mode=pl.Buffered(1))


def _rms(x, g):
    return x * lax.rsqrt(jnp.mean(x * x, axis=-1, keepdims=True) + EPS) * g


def _proj_kernel(x_ref, c128_ref, s128_ref, c64_ref, s64a_ref, s64b_ref,
                 gmix_ref, win_ref, gq_ref, wuq_ref, gkv_ref, wuk_ref, *rest, tiles_per_seq):
    if tiles_per_seq:
        pre_ckv_ref, pre_kpe_ref = rest[:2]
        rest = rest[2:]
    qr_ref, kr_ref, vr_ref, gr_ref, qcat_ref, kvcat_ref, ckv_ref, kpe_ref = rest
    a = _rms(x_ref[...], gmix_ref[...]).astype(BF16)
    o = 4 * RET_W
    z_lat = jnp.dot(a, win_ref[:, o:], preferred_element_type=F32)
    z = jnp.dot(a, win_ref[:, 0:o], preferred_element_type=F32)
    c128 = c128_ref[...]
    s128 = s128_ref[...]
    c64 = c64_ref[...]
    s64a = s64a_ref[...]
    s64b = s64b_ref[...]

    def rope128(v):
        return v * c128 + pltpu.roll(v, 64, 1) * s128

    def rope64(v):
        return v * c64 + pltpu.roll(v, 96, 1) * s64a + pltpu.roll(v, 32, 1) * s64b

    cqn = _rms(z_lat[:, 0:Q_LORA], gq_ref[...])
    q2 = jnp.dot(cqn.astype(BF16), wuq_ref[...], preferred_element_type=F32)
    for h in range(MLA_HEADS):
        b0 = h * 2 * LANES
        nope = q2[:, b0:b0 + QK_NOPE]
        pe = rope64(q2[:, b0 + LANES:b0 + 2 * LANES])
        qlat = jnp.dot(nope.astype(BF16), wuk_ref[h], preferred_element_type=F32)
        qcat_ref[h, :, 0:KV_LORA] = (qlat * SOFTMAX_SCALE).astype(BF16)
        qcat_ref[h, :, KV_LORA:QCAT_W] = (pe * SOFTMAX_SCALE).astype(BF16)

    ckvn = _rms(z_lat[:, Q_LORA:Q_LORA + KV_LORA], gkv_ref[...])
    kp = rope64(z_lat[:, Q_LORA + KV_LORA:Q_LORA + KV_LORA + LANES])
    kvcat_ref[:, 0:KV_LORA] = ckvn.astype(BF16)
    kvcat_ref[:, KV_LORA:QCAT_W] = kp.astype(BF16)

    for h in range(RET_HEADS):
        sl = slice(h * RET_DK, (h + 1) * RET_DK)
        qr_ref[:, sl] = rope128(z[:, sl]).astype(BF16)
        ksl = slice(RET_W + h * RET_DK, RET_W + (h + 1) * RET_DK)
        kr_ref[:, sl] = (rope128(z[:, ksl]) * (RET_DK ** -0.5)).astype(BF16)
    vr_ref[...] = z[:, 2 * RET_W:3 * RET_W].astype(BF16)
    gr_ref[...] = z[:, 3 * RET_W:4 * RET_W].astype(BF16)

    if tiles_per_seq:
        tm = x_ref.shape[0]
        n_pre = pre_ckv_ref.shape[0]
        t = pl.program_id(0) % tiles_per_seq

        @pl.when(t == 0)
        def _():
            ckv_ref[0, 0:n_pre, :] = pre_ckv_ref[...]
            kpe_ref[0, 0:n_pre, :] = pre_kpe_ref[...]

        rows = pl.ds(pl.multiple_of(n_pre + t * tm, 8), tm)
        ckv_ref[0, rows, :] = ckvn
        kpe_ref[0, rows, :] = kp[:, 0:QK_ROPE]
    else:
        ckv_ref[...] = ckvn
        kpe_ref[...] = kp[:, 0:QK_ROPE]


def _project(x, tabs, wts, tm, prefix=None):
    n, d = x.shape
    nt = tabs[0].shape[0] // tm
    gmix, win, gq, wuq, gkv, wuk = wts
    row = lambda w: pl.BlockSpec((tm, w), lambda i: (i, 0))
    tab = pl.BlockSpec((tm, LANES), lambda i: (i % nt, 0))
    if prefix is None:
        key_shapes = (jax.ShapeDtypeStruct((n, KV_LORA), F32), jax.ShapeDtypeStruct((n, QK_ROPE), F32))
        key_specs = (row(KV_LORA), row(QK_ROPE))
        extra, extra_specs = (), []
    else:
        t_all = prefix[0].shape[0] + nt * tm
        nseq = n // (nt * tm)
        seq_blk = lambda w: pl.BlockSpec((1, t_all, w), lambda i: (i // nt, 0, 0))
        key_shapes = (jax.ShapeDtypeStruct((nseq, t_all, KV_LORA), F32),
                      jax.ShapeDtypeStruct((nseq, t_all, QK_ROPE), F32))
        key_specs = (seq_blk(KV_LORA), seq_blk(QK_ROPE))
        extra, extra_specs = tuple(prefix), [_const_spec(p.shape) for p in prefix]
    out_shapes = (
        jax.ShapeDtypeStruct((n, RET_W), BF16), jax.ShapeDtypeStruct((n, RET_W), BF16),
        jax.ShapeDtypeStruct((n, RET_W), BF16), jax.ShapeDtypeStruct((n, RET_W), BF16),
        jax.ShapeDtypeStruct((MLA_HEADS, n, QCAT_W), BF16),
        jax.ShapeDtypeStruct((n, QCAT_W), BF16)) + key_shapes
    out_specs = (row(RET_W), row(RET_W), row(RET_W), row(RET_W),
                 pl.BlockSpec((MLA_HEADS, tm, QCAT_W), lambda i: (0, i, 0)),
                 row(QCAT_W)) + key_specs
    return pl.pallas_call(
        functools.partial(_proj_kernel, tiles_per_seq=0 if prefix is None else nt),
        grid=(n // tm,),
        in_specs=[row(d), tab, tab, tab, tab, tab,
                  _const_spec(gmix.shape), _const_spec(win.shape), _const_spec(gq.shape),
                  _const_spec(wuq.shape), _const_spec(gkv.shape), _const_spec(wuk.shape)]
        + extra_specs,
        out_specs=out_specs,
        out_shape=out_shapes,
        compiler_params=pltpu.CompilerParams(dimension_semantics=("arbitrary",),
                                             vmem_limit_bytes=VMEM_LIMIT),
        name="proj",
    )(x, *tabs, gmix, win, gq, wuq, gkv, wuk, *extra)


def _ret_chunk_steps(q_ref, k_ref, v_ref, g_ref, o_ref, st_ref, lv):
    L = q_ref.shape[0]
    li = lax.broadcasted_iota(jnp.int32, (L, L), 0)
    mi = lax.broadcasted_iota(jnp.int32, (L, L), 1)
    diff = (li - mi).astype(F32)
    n = lax.broadcasted_iota(jnp.int32, (L, 1), 0).astype(F32)
    for h in range(RET_HEADS):
        lg = LOG_GAMMA[h]
        sl = slice(h * RET_DK, (h + 1) * RET_DK)
        decay = jnp.where(diff >= 0, jnp.exp(lg * jnp.maximum(diff, 0.0)), 0.0)
        qdec = jnp.exp(lg * (n + 1.0))
        kdec = jnp.where(n < lv, jnp.exp(lg * jnp.maximum(lv - 1.0 - n, 0.0)), 0.0)
        q = q_ref[:, sl]
        k = k_ref[:, sl]
        v = v_ref[:, sl]
        s = lax.dot_general(q, k, (((1,), (1,)), ((), ())), preferred_element_type=F32) * decay
        state = st_ref[h]
        qd = (q.astype(F32) * qdec).astype(BF16)
        cross = jnp.dot(qd, state.astype(BF16), preferred_element_type=F32)
        yield
        inner = jnp.dot(s.astype(BF16), v, preferred_element_type=F32)
        o = inner + cross
        kd = (k.astype(F32) * kdec).astype(BF16)
        st_ref[h] = math.exp(lg * lv) * state + lax.dot_general(
            kd, v, (((0,), (0,)), ((), ())), preferred_element_type=F32)
        of = o * lax.rsqrt(jnp.mean(o * o, axis=-1, keepdims=True) + EPS)
        g = g_ref[:, sl].astype(F32)
        o_ref[:, sl] = (of * (g * jax.nn.sigmoid(g))).astype(BF16)
        yield


def _ret_chunk(*args):
    for _ in _ret_chunk_steps(*args):
        pass


def _ret_chunk_kernel(q_ref, k_ref, v_ref, g_ref, s0_ref, o_ref, sf_ref, st_ref, *, lv):
    c = pl.program_id(0)
    nb = q_ref.shape[0]

    @pl.when(c == 0)
    def _():
        for b in range(nb):
            st_ref[b] = s0_ref[0]

    for b in range(nb):
        _ret_chunk(q_ref.at[b], k_ref.at[b], v_ref.at[b], g_ref.at[b], o_ref.at[b],
                   st_ref.at[b], lv)

    @pl.when(c == pl.num_programs(0) - 1)
    def _():
        sf_ref[...] = st_ref[...]


def _retention_chunks(qr, kr, vr, gr, s0, nb, L, lv):
    n = qr.shape[0]
    t = n // nb
    as3 = lambda a: a.reshape(nb, t, RET_W)
    blk = pl.BlockSpec((nb, L, RET_W), lambda c: (0, c, 0))
    return pl.pallas_call(
        functools.partial(_ret_chunk_kernel, lv=float(lv)),
        grid=(t // L,),
        in_specs=[blk, blk, blk, blk, _const_spec(s0.shape)],
        out_specs=(blk, pl.BlockSpec((nb, RET_HEADS, RET_DK, RET_DV), lambda c: (0, 0, 0, 0))),
        out_shape=(jax.ShapeDtypeStruct((nb, t, RET_W), BF16),
                   jax.ShapeDtypeStruct((nb, RET_HEADS, RET_DK, RET_DV), F32)),
        scratch_shapes=[pltpu.VMEM((nb, RET_HEADS, RET_DK, RET_DV), F32)],
        compiler_params=pltpu.CompilerParams(dimension_semantics=("arbitrary",),
                                             vmem_limit_bytes=VMEM_LIMIT),
        name="ret_chunks",
    )(as3(qr), as3(kr), as3(vr), as3(gr), s0)


def _ret_decode_kernel(q_ref, k_ref, v_ref, g_ref, s_ref, o_ref, sn_ref, *, ls):
    R = q_ref.shape[0]
    per_tile = ROW_TILE // ls
    li = lax.broadcasted_iota(jnp.int32, (R, R), 0)
    mi = lax.broadcasted_iota(jnp.int32, (R, R), 1)
    same = (li // ls) == (mi // ls)
    diff = ((li % ls) - (mi % ls)).astype(F32)
    t_col = (lax.broadcasted_iota(jnp.int32, (R, 1), 0) % ls).astype(F32)
    seq_in_tile = lax.broadcasted_iota(jnp.int32, (ROW_TILE, 1), 0) // ls
    for h in range(RET_HEADS):
        lg = LOG_GAMMA[h]
        sl = slice(h * RET_DK, (h + 1) * RET_DK)
        q = q_ref[:, sl]
        k = k_ref[:, sl]
        v = v_ref[:, sl]
        decay = jnp.where(same & (diff >= 0), jnp.exp(lg * jnp.maximum(diff, 0.0)), 0.0)
        s = lax.dot_general(q, k, (((1,), (1,)), ((), ())), preferred_element_type=F32) * decay
        inner = jnp.dot(s.astype(BF16), v, preferred_element_type=F32)
        qd = q.astype(F32) * jnp.exp(lg * (t_col + 1.0))
        kd = k.astype(F32) * jnp.exp(lg * (ls - 1.0 - t_col))
        sdec = math.exp(lg * ls)
        cross_tiles = []
        for tt in range(R // ROW_TILE):
            rows = slice(tt * ROW_TILE, (tt + 1) * ROW_TILE)
            qd_t = qd[rows].astype(BF16)
            kd_t = kd[rows]
            v_t = v[rows]
            acc = jnp.zeros((ROW_TILE, RET_DV), F32)
            for j in range(per_tile):
                b = tt * per_tile + j
                state = s_ref[b, h]
                mine = seq_in_tile == j
                cr = jnp.dot(qd_t, state.astype(BF16), preferred_element_type=F32)
                acc = jnp.where(mine, cr, acc)
                kdm = jnp.where(mine, kd_t, 0.0).astype(BF16)
                sn_ref[b, h] = sdec * state + lax.dot_general(
                    kdm, v_t, (((0,), (0,)), ((), ())), preferred_element_type=F32)
            cross_tiles.append(acc)
        o = inner + jnp.concatenate(cross_tiles, axis=0)
        of = o * lax.rsqrt(jnp.mean(o * o, axis=-1, keepdims=True) + EPS)
        g = g_ref[:, sl].astype(F32)
        o_ref[:, sl] = (of * (g * jax.nn.sigmoid(g))).astype(BF16)


def _retention_decode(qr, kr, vr, gr, state, ls, rows):
    n = qr.shape[0]
    nseq = rows // ls
    row = pl.BlockSpec((rows, RET_W), lambda i: (i, 0))
    st = pl.BlockSpec((nseq, RET_HEADS, RET_DK, RET_DV), lambda i: (i, 0, 0, 0))
    return pl.pallas_call(
        functools.partial(_ret_decode_kernel, ls=ls),
        grid=(n // rows,),
        in_specs=[row, row, row, row, st],
        out_specs=(row, st),
        out_shape=(jax.ShapeDtypeStruct((n, RET_W), BF16),
                   jax.ShapeDtypeStruct(state.shape, F32)),
        compiler_params=pltpu.CompilerParams(dimension_semantics=("arbitrary",),
                                             vmem_limit_bytes=VMEM_LIMIT),
        name="ret_decode",
    )(qr, kr, vr, gr, state)


def _mla_prompt_kernel(q_ref, k_ref, km_ref, o_ref, vt_ref, vmt_ref, m_ref, l_ref, acc_ref, *, tq, tk):
    assert tk % tq == 0 and tk <= 2 * tq
    i = pl.program_id(1)
    nkm = km_ref.shape[0]
    nq = MLA_HEADS * tq

    @pl.when(i == 0)
    def _():
        vmt_ref[...] = km_ref[:, 0:KV_LORA].T
        for c in range(k_ref.shape[0] // tq):
            vt_ref[:, c * tq:(c + 1) * tq] = k_ref[c * tq:(c + 1) * tq, 0:KV_LORA].T

    q_all = q_ref[...].reshape(nq, QCAT_W)
    groups = [slice(g * GROUP_LANES, (g + 1) * GROUP_LANES) for g in range(nq // GROUP_LANES)]

    def scores(kc, grp):
        return lax.dot_general(kc, q_all[grp], (((1,), (1,)), ((), ())),
                               preferred_element_type=F32)

    s = scores(km_ref[...], slice(0, nq))
    s = jnp.where(lax.broadcasted_iota(jnp.int32, (nkm, nq), 0) < N_META, s, NEG_INF)
    m = jnp.max(s, axis=0, keepdims=True)
    p = jnp.exp(s - m)
    m_ref[...] = m
    l_ref[...] = jnp.sum(p, axis=0, keepdims=True)
    acc_ref[...] = jnp.dot(vmt_ref[...], p.astype(BF16), preferred_element_type=F32)

    def update(start, size, mask):
        kc = k_ref[pl.ds(start, size), :]
        vt = vt_ref[:, pl.ds(start, size)]
        ss = [scores(kc, grp) for grp in groups]
        for grp, s in zip(groups, ss):
            if mask is not None:
                s = jnp.where(mask[:, grp], s, NEG_INF)
            m_prev = m_ref[:, grp]
            m_new = jnp.maximum(m_prev, jnp.max(s, axis=0, keepdims=True))
            alpha = jnp.exp(m_prev - m_new)
            p = jnp.exp(s - m_new)
            m_ref[:, grp] = m_new
            l_ref[:, grp] = alpha * l_ref[:, grp] + jnp.sum(p, axis=0, keepdims=True)
            acc_ref[:, grp] = alpha * acc_ref[:, grp] + jnp.dot(
                vt, p.astype(BF16), preferred_element_type=F32)

    n_vis = i * tq

    def body(j, carry):
        update(pl.multiple_of(j * tk, tk), tk, None)
        return carry

    lax.fori_loop(0, n_vis // tk, body, 0)
    if tk > tq:
        @pl.when(n_vis % tk != 0)
        def _():
            update(pl.multiple_of(n_vis - tq, tq), tq, None)

    key = lax.broadcasted_iota(jnp.int32, (tq, nq), 0)
    qry = lax.broadcasted_iota(jnp.int32, (tq, nq), 1) % tq
    update(pl.multiple_of(n_vis, tq), tq, key <= qry)
    out = acc_ref[...] / l_ref[...]
    for h in range(MLA_HEADS):
        o_ref[:, h * KV_LORA:(h + 1) * KV_LORA] = out[:, h * tq:(h + 1) * tq].T.astype(BF16)


def _mla_prompt(qcat, kvcat, kmeta, nb, tq, tk):
    n = kvcat.shape[0]
    t = n // nb
    nq = t // tq
    return pl.pallas_call(
        functools.partial(_mla_prompt_kernel, tq=tq, tk=tk),
        grid=(nb, nq),
        in_specs=[pl.BlockSpec((MLA_HEADS, tq, QCAT_W), lambda b, i: (0, b * nq + i, 0)),
                  pl.BlockSpec((t, QCAT_W), lambda b, i: (b, 0)),
                  _const_spec(kmeta.shape)],
        out_specs=pl.BlockSpec((tq, MLA_HEADS * KV_LORA), lambda b, i: (b * nq + i, 0)),
        out_shape=jax.ShapeDtypeStruct((n, MLA_HEADS * KV_LORA), BF16),
        scratch_shapes=[pltpu.VMEM((KV_LORA, t), BF16), pltpu.VMEM((KV_LORA, kmeta.shape[0]), BF16),
                        pltpu.VMEM((1, MLA_HEADS * tq), F32), pltpu.VMEM((1, MLA_HEADS * tq), F32),
                        pltpu.VMEM((KV_LORA, MLA_HEADS * tq), F32)],
        compiler_params=pltpu.CompilerParams(dimension_semantics=("arbitrary", "arbitrary"),
                                             vmem_limit_bytes=VMEM_LIMIT),
        name="mla_prompt",
    )(qcat, kvcat, kmeta)


def _mla_meta_kernel(q_ref, km_ref, o_ref):
    km = km_ref[...]
    r = q_ref.shape[1]
    row = lax.broadcasted_iota(jnp.int32, (r, km.shape[0]), 0)
    col = lax.broadcasted_iota(jnp.int32, (r, km.shape[0]), 1)
    mask = (col <= row) & (col < N_META)
    for h in range(MLA_HEADS):
        s = lax.dot_general(q_ref[h], km, (((1,), (1,)), ((), ())), preferred_element_type=F32)
        s = jnp.where(mask, s, NEG_INF)
        p = jnp.exp(s - jnp.max(s, axis=-1, keepdims=True))
        acc = jnp.dot(p.astype(BF16), km[:, 0:KV_LORA], preferred_element_type=F32)
        o_ref[:, h * KV_LORA:(h + 1) * KV_LORA] = (
            acc / jnp.sum(p, axis=-1, keepdims=True)).astype(BF16)


def _mla_meta(qcat, kmeta):
    r = qcat.shape[1]
    return pl.pallas_call(
        _mla_meta_kernel,
        out_shape=jax.ShapeDtypeStruct((r, MLA_HEADS * KV_LORA), BF16),
        compiler_params=pltpu.CompilerParams(vmem_limit_bytes=VMEM_LIMIT),
        name="mla_meta",
    )(qcat, kmeta)


def _mla_decode_kernel(pt_ref, q_ref, kself_ref, peself_ref, kv_hbm, pe_hbm, *rest,
                       n_pages, page, ls, chunk, side_lv):
    if side_lv is None:
        o_ref, kvbuf, pebuf, kb16, pb16, s_ref, sem = rest
    else:
        (rq_ref, rk_ref, rv_ref, rg_ref, s0_ref, o_ref, ro_ref, sf_ref,
         kvbuf, pebuf, kb16, pb16, s_ref, sem, st_ref) = rest
    b = pl.program_id(0)
    nb = pl.num_programs(0)
    slot = b % 2
    n_keys = n_pages * page

    def page_copies(seq, sl):
        out = []
        for j in range(n_pages):
            pid = pt_ref[seq, j]
            out.append(pltpu.make_async_copy(kv_hbm.at[pid], kvbuf.at[sl, pl.ds(j * page, page)],
                                             sem.at[0, sl]))
            out.append(pltpu.make_async_copy(pe_hbm.at[pid], pebuf.at[sl, j], sem.at[1, sl]))
        return out

    def start_all(copies):
        for n, c in enumerate(copies):
            c.start(priority=(n // 2) % 2)

    @pl.when(b == 0)
    def _():
        start_all(page_copies(0, 0))

    @pl.when(b + 1 < nb)
    def _():
        start_all(page_copies(b + 1, 1 - slot))

    for c in page_copies(b, slot):
        c.wait()

    q = q_ref[0]
    rows = q.shape[0]
    ql = q[:, 0:KV_LORA]
    qp = q[:, KV_LORA:KV_LORA + QK_ROPE]
    ppc = chunk // page
    n_chunks = n_keys // chunk
    side = iter(())
    if side_lv is not None:
        n_seq = st_ref.shape[0]
        seq = b % n_seq

        @pl.when(b < n_seq)
        def _():
            st_ref[seq] = s0_ref[0]

        side = _ret_chunk_steps(rq_ref.at[0], rk_ref.at[0], rv_ref.at[0], rg_ref.at[0],
                                ro_ref.at[0], st_ref.at[seq], side_lv)
    for c in range(n_chunks):
        next(side, None)
        ksl = slice(c * chunk, (c + 1) * chunk)
        kb16[ksl, :] = kvbuf[slot, ksl, :].astype(BF16)
        for j in range(ppc):
            jj = c * ppc + j
            pb16[:, jj * page:(jj + 1) * page] = pebuf[slot, jj].astype(BF16)
        s_ref[:, ksl] = (
            lax.dot_general(ql, kb16[ksl, :], (((1,), (1,)), ((), ())), preferred_element_type=F32)
            + jnp.dot(qp, pb16[:, ksl], preferred_element_type=F32))

    parts = []
    n_split = 4 if n_chunks % 4 == 0 else 1
    for h in range(n_split):
        hsl = slice(h * n_keys // n_split, (h + 1) * n_keys // n_split)
        s_h = s_ref[:, hsl]
        m_h = jnp.max(s_h, axis=-1, keepdims=True)
        p_h = jnp.exp(s_h - m_h)
        parts.append((m_h, jnp.sum(p_h, axis=-1, keepdims=True),
                      jnp.dot(p_h.astype(BF16), kb16[hsl, :], preferred_element_type=F32)))

    qf = q.astype(F32)
    kvs = kself_ref[0]
    pes = peself_ref[0]
    t_row = lax.broadcasted_iota(jnp.int32, (rows, 1), 0) // MLA_HEADS
    ss = []
    for t2 in range(ls):
        s_t = (jnp.sum(qf[:, 0:KV_LORA] * kvs[t2:t2 + 1, :], axis=-1, keepdims=True)
               + jnp.sum(qf[:, KV_LORA:KV_LORA + QK_ROPE] * pes[t2:t2 + 1, :], axis=-1, keepdims=True))
        ss.append(jnp.where(t_row >= t2, s_t, NEG_INF))
    m_s = ss[0]
    for s_t in ss[1:]:
        m_s = jnp.maximum(m_s, s_t)
    l_s = jnp.zeros((rows, 1), F32)
    acc_s = jnp.zeros((rows, KV_LORA), F32)
    for t2 in range(ls):
        p_t = jnp.exp(ss[t2] - m_s)
        l_s = l_s + p_t
        acc_s = acc_s + p_t * kvs[t2:t2 + 1, :]
    parts.append((m_s, l_s, acc_s))

    m = parts[0][0]
    for m_h, _, _ in parts[1:]:
        m = jnp.maximum(m, m_h)
    l = jnp.zeros((rows, 1), F32)
    acc = jnp.zeros((rows, KV_LORA), F32)
    for m_h, l_h, acc_h in parts:
        w_h = jnp.exp(m_h - m)
        l = l + w_h * l_h
        acc = acc + w_h * acc_h
    o_ref[0] = (acc / l).astype(BF16)

    for _ in side:
        pass
    if side_lv is not None:
        @pl.when(b == nb - 1)
        def _():
            sf_ref[...] = st_ref[...]


def _mla_decode(page_table, q, kself, peself, cache_kv, cache_pe_t, chunk, side=None):
    db, rows, _ = q.shape
    ls = kself.shape[1]
    n_pages = page_table.shape[1]
    page = cache_kv.shape[1]
    n_keys = n_pages * page
    in_specs = [pl.BlockSpec((1, rows, QCAT_W), lambda b, pt: (b, 0, 0)),
                pl.BlockSpec((1, ls, KV_LORA), lambda b, pt: (b, 0, 0)),
                pl.BlockSpec((1, ls, QK_ROPE), lambda b, pt: (b, 0, 0)),
                pl.BlockSpec(memory_space=pl.ANY),
                pl.BlockSpec(memory_space=pl.ANY)]
    out_specs = [pl.BlockSpec((1, rows, KV_LORA), lambda b, pt: (b, 0, 0))]
    out_shape = [jax.ShapeDtypeStruct((db, rows, KV_LORA), BF16)]
    scratch = [pltpu.VMEM((2, n_keys, KV_LORA), F32),
               pltpu.VMEM((2, n_pages, QK_ROPE, page), F32),
               pltpu.VMEM((n_keys, KV_LORA), BF16),
               pltpu.VMEM((QK_ROPE, n_keys), BF16),
               pltpu.VMEM((rows, n_keys), F32),
               pltpu.SemaphoreType.DMA((2, 2))]
    operands = [page_table, q, kself, peself, cache_kv, cache_pe_t]
    side_lv = None
    if side is not None:
        qr, kr, vr, gr, s0, L = side
        n_seq, t, _ = qr.shape
        assert n_seq * (t // L) == db, "one retention chunk per decode step"
        blk = pl.BlockSpec((1, L, RET_W), lambda b, pt: (b % n_seq, b // n_seq, 0))
        st_shape = (n_seq, RET_HEADS, RET_DK, RET_DV)
        in_specs += [blk, blk, blk, blk, pl.BlockSpec(s0.shape, lambda b, pt: (0, 0, 0, 0))]
        out_specs += [blk, pl.BlockSpec(st_shape, lambda b, pt: (0, 0, 0, 0))]
        out_shape += [jax.ShapeDtypeStruct((n_seq, t, RET_W), BF16),
                      jax.ShapeDtypeStruct(st_shape, F32)]
        scratch += [pltpu.VMEM(st_shape, F32)]
        operands += [qr, kr, vr, gr, s0]
        side_lv = float(L)
    grid_spec = pltpu.PrefetchScalarGridSpec(
        num_scalar_prefetch=1, grid=(db,), in_specs=in_specs, out_specs=out_specs,
        scratch_shapes=scratch)
    out = pl.pallas_call(
        functools.partial(_mla_decode_kernel, n_pages=n_pages, page=page, ls=ls, chunk=chunk,
                          side_lv=side_lv),
        grid_spec=grid_spec,
        out_shape=out_shape,
        compiler_params=pltpu.CompilerParams(dimension_semantics=("arbitrary",),
                                             vmem_limit_bytes=VMEM_LIMIT),
        name="mla_decode",
    )(*operands)
    return out[0] if side is None else tuple(out)


def _mix_ffn_kernel(x_ref, ret_ref, olat_ref, cin_ref, wuv_ref, wout_ref, gffn_ref, wup_ref,
                    cw_ref, cb_ref, wdown_ref, gfin_ref, y_ref, cout_ref,
                    carry_ref, stage_ref, hmid_ref, *, stride, carry_end, cw):
    t = pl.program_id(1)
    tm = x_ref.shape[0]
    hal = carry_ref.shape[0]
    d_ff = wdown_ref.shape[0]

    @pl.when(t == 0)
    def _():
        carry_ref[...] = cin_ref[...]

    mla = jnp.concatenate(
        [jnp.dot(olat_ref[:, h * KV_LORA:(h + 1) * KV_LORA], wuv_ref[h],
                 preferred_element_type=F32).astype(BF16) for h in range(MLA_HEADS)], axis=1)
    mixed = jnp.concatenate([ret_ref[...], mla], axis=1)
    h1 = x_ref[...] + jnp.dot(mixed, wout_ref[...], preferred_element_type=F32)
    a2 = _rms(h1, gffn_ref[...]).astype(BF16)

    def conv_half(c0, slot):
        u = jnp.dot(a2, wup_ref[:, c0:c0 + cw], preferred_element_type=F32)
        stage_ref[slot, 0:hal, :] = carry_ref[:, c0:c0 + cw]
        stage_ref[slot, hal:hal + tm, :] = u
        carry_ref[:, c0:c0 + cw] = u[carry_end - hal:carry_end, :]
        um1 = stage_ref[slot, hal - stride:hal - stride + tm, :]
        um2 = stage_ref[slot, hal - 2 * stride:hal - 2 * stride + tm, :]
        return (cb_ref[:, c0:c0 + cw] + cw_ref[0:1, c0:c0 + cw] * um2
                + cw_ref[1:2, c0:c0 + cw] * um1 + cw_ref[2:3, c0:c0 + cw] * u)

    n_slots = stage_ref.shape[0]
    for jc in range(d_ff // cw):
        ca = conv_half(jc * cw, (2 * jc) % n_slots)
        cg = conv_half(d_ff + jc * cw, (2 * jc + 1) % n_slots)
        hmid_ref[:, jc * cw:(jc + 1) * cw] = (cg * jax.nn.sigmoid(cg) * ca).astype(BF16)
    h2 = h1 + jnp.dot(hmid_ref[...], wdown_ref[...], preferred_element_type=F32)
    y_ref[...] = _rms(h2, gfin_ref[...])

    @pl.when(t == pl.num_programs(1) - 1)
    def _():
        cout_ref[0] = carry_ref[...]


def _mix_ffn(x, ret, olat, carry_in, wts, nseq, tm, stride, carry_end, cw):
    n, d = x.shape
    hal = carry_in.shape[0]
    nt = n // (nseq * tm)
    wuv, wout, gffn, wup, convw, convb, wdown, gfin = wts
    row = lambda w: pl.BlockSpec((tm, w), lambda b, t: (b * nt + t, 0))
    return pl.pallas_call(
        functools.partial(_mix_ffn_kernel, stride=stride, carry_end=carry_end, cw=cw),
        grid=(nseq, nt),
        in_specs=[row(d), row(RET_W), row(MLA_HEADS * KV_LORA), _const_spec(carry_in.shape),
                  _const_spec(wuv.shape), _const_spec(wout.shape), _const_spec(gffn.shape),
                  _const_spec(wup.shape), _const_spec(convw.shape), _const_spec(convb.shape),
                  _const_spec(wdown.shape), _const_spec(gfin.shape)],
        out_specs=(row(d), pl.BlockSpec((1, hal, wup.shape[1]), lambda b, t: (b, 0, 0))),
        out_shape=(jax.ShapeDtypeStruct((n, d), F32),
                   jax.ShapeDtypeStruct((nseq, hal, wup.shape[1]), F32)),
        scratch_shapes=[pltpu.VMEM((hal, wup.shape[1]), F32),
                        pltpu.VMEM((4, hal + tm, cw), F32),
                        pltpu.VMEM((tm, wdown.shape[0]), BF16)],
        compiler_params=pltpu.CompilerParams(dimension_semantics=("arbitrary", "arbitrary"),
                                             vmem_limit_bytes=VMEM_LIMIT),
        name="mix_ffn",
    )(x, ret, olat, carry_in, wuv, wout, gffn, wup, convw, convb, wdown, gfin)


def _rope_tables(pos):
    pos = np.asarray(pos, np.float64)[:, None]

    def cs(dim):
        inv = ROPE_THETA ** (-np.arange(0, dim, 2, dtype=np.float64) / dim)
        ang = pos * inv[None, :]
        return np.cos(ang), np.sin(ang)

    c, s = cs(RET_DK)
    c128 = np.concatenate([c, c], axis=-1)
    s128 = np.concatenate([-s, s], axis=-1)
    c, s = cs(QK_ROPE)
    z32 = np.zeros_like(s)
    z64 = np.zeros((pos.shape[0], LANES - QK_ROPE))
    c64 = np.concatenate([c, c, z64], axis=-1)
    s64a = np.concatenate([-s, z32, z64], axis=-1)
    s64b = np.concatenate([z32, s, z64], axis=-1)
    return tuple(np.asarray(t, np.float32) for t in (c128, s128, c64, s64a, s64b))


def kernel(x_prompt, x_sample, cache_kv_latent, cache_k_rope, state_retention, state_ffn_conv,
           page_table, meta_tokens, g_mix, w_in, g_q, w_uq, g_kv, w_uk, w_uv, w_out,
           g_ffn, w_up, conv_w, conv_b, w_down, g_final):
    nb, seq, d = x_prompt.shape
    db, ls, _ = x_sample.shape
    depth = w_in.shape[0]
    assert depth == 1, "single-layer step"
    n_pages = page_table.shape[1]
    page = cache_kv_latent.shape[2]
    past_len = n_pages * page
    d_ff = w_down.shape[1]
    l = 0

    w = w_in[l]
    win = jnp.concatenate([w, jnp.zeros((d, LANES - QK_ROPE), w.dtype)], axis=1).astype(BF16)
    wq = w_uq[l].reshape(Q_LORA, MLA_HEADS, QK_NOPE + QK_ROPE)
    wuq = jnp.concatenate([wq, jnp.zeros((Q_LORA, MLA_HEADS, LANES - QK_ROPE), wq.dtype)],
                          axis=-1).reshape(Q_LORA, MLA_HEADS * 2 * LANES).astype(BF16)
    wuk = jnp.transpose(w_uk[l], (1, 2, 0)).astype(BF16)
    wuv = jnp.transpose(w_uv[l], (1, 0, 2)).astype(BF16)
    proj_w = (g_mix[l][None, :], win, g_q[l][None, :], wuq, g_kv[l][None, :], wuk)
    ffn_w = (wuv, w_out[l].astype(BF16), g_ffn[l][None, :], w_up[l].astype(BF16),
             conv_w[l], conv_b[l][None, :], w_down[l].astype(BF16), g_final[None, :])

    tile = META_ROWS
    xm = jnp.concatenate([meta_tokens.astype(F32), jnp.zeros((tile - N_META, d), F32)], axis=0)
    tabs_m = _rope_tables(np.arange(tile))
    qr, kr, vr, gr, qcat_m, kvcat_m, ckv_m, kpe_m = _project(xm, tabs_m, proj_w, tile)
    zero_state = jnp.zeros((1, RET_HEADS, RET_DK, RET_DV), F32)
    ret_m, state_m = _retention_chunks(qr, kr, vr, gr, zero_state, 1, tile, N_META)
    ret_m = ret_m.reshape(tile, RET_W)
    row_valid = (jnp.arange(tile) < N_META)[:, None]
    kmeta = jnp.where(row_valid, kvcat_m, jnp.zeros_like(kvcat_m))
    olat_m = _mla_meta(qcat_m, kmeta)
    _, carry_m = _mix_ffn(xm, ret_m, olat_m, jnp.zeros((8, 2 * d_ff), F32), ffn_w,
                          1, tile, 1, N_META, MXU_TILE)

    tm = min(PROJ_ROWS, seq)
    xp = x_prompt.reshape(nb * seq, d)
    tabs_p = _rope_tables(N_META + np.arange(seq))
    qr_p, kr_p, vr_p, gr_p, qcat, kvcat, kv_p, pe_p = _project(
        xp, tabs_p, proj_w, tm, prefix=(ckv_m[:N_META], kpe_m[:N_META]))
    ns = db * ls
    xs = x_sample.reshape(ns, d)
    pos_s = past_len + np.arange(ls)
    tabs_s = tuple(np.tile(t, (db, 1)) for t in _rope_tables(pos_s))
    qr, kr, vr, gr, qcat_s, _, ckv_s, kpe_s = _project(xs, tabs_s, proj_w, min(ns, PROJ_ROWS // 2))

    ret_s, state_s = _retention_decode(qr, kr, vr, gr, state_retention[l], ls, min(ns, DECODE_RET_ROWS))
    q_s = jnp.transpose(qcat_s.reshape(MLA_HEADS, db, ls, QCAT_W), (1, 2, 0, 3)).reshape(
        db, ls * MLA_HEADS, QCAT_W)
    cache_pe_t = jnp.swapaxes(cache_k_rope[l], 1, 2)
    decode_args = (page_table, q_s, ckv_s.reshape(db, ls, KV_LORA), kpe_s.reshape(db, ls, QK_ROPE),
                   cache_kv_latent[l], cache_pe_t, min(DECODE_KEY_CHUNK, past_len))
    if nb * (seq // RET_CHUNK) == db:
        as3 = lambda a: a.reshape(nb, seq, RET_W)
        olat_s, ret_p, state_p = _mla_decode(
            *decode_args, side=(as3(qr_p), as3(kr_p), as3(vr_p), as3(gr_p), state_m, RET_CHUNK))
    else:
        olat_s = _mla_decode(*decode_args)
        ret_p, state_p = _retention_chunks(qr_p, kr_p, vr_p, gr_p, state_m, nb, RET_CHUNK, RET_CHUNK)
    ret_p = ret_p.reshape(nb * seq, RET_W)

    tq = min(ATTN_Q_ROWS, seq)
    olat_p = _mla_prompt(qcat, kvcat, kmeta, nb, tq, min(ATTN_K_ROWS, 2 * tq))
    tf = min(FFN_ROWS, seq)
    y_p, carry_p = _mix_ffn(xp, ret_p, olat_p, carry_m[0], ffn_w, nb, tf, 1, tf, MXU_TILE)

    olat_s = olat_s.reshape(ns, MLA_HEADS * KV_LORA)
    tmaj = lambda a: jnp.transpose(a.reshape(db, ls, a.shape[-1]), (1, 0, 2)).reshape(ns, a.shape[-1])
    carry_s_in = jnp.transpose(state_ffn_conv[l], (1, 0, 2)).reshape((CONV_W - 1) * db, 2 * d_ff)
    y_s, carry_s = _mix_ffn(tmaj(xs), tmaj(ret_s), tmaj(olat_s), carry_s_in, ffn_w,
                            1, ns, db, ns, MXU_TILE)
    y_s = jnp.transpose(y_s.reshape(ls, db, d), (1, 0, 2))
    conv_s = jnp.transpose(carry_s.reshape(CONV_W - 1, db, 2 * d_ff), (1, 0, 2))

    return (y_p.reshape(nb, seq, d), y_s,
            kv_p[None], pe_p[None],
            state_p[None],
            carry_p[:, 8 - (CONV_W - 1):, :][None],
            ckv_s.reshape(db, ls, KV_LORA)[None], kpe_s.reshape(db, ls, QK_ROPE)[None],
            state_s[None], conv_s[None])
```

```python
import functools
import math

import jax
import jax.numpy as jnp
import numpy as np
from jax import lax
from jax.experimental import pallas as pl
from jax.experimental.pallas import tpu as pltpu

F32 = jnp.float32
BF16 = jnp.bfloat16

N_META = 16
RET_HEADS = 4
RET_DK = 128
RET_DV = 128
MLA_HEADS = 4
Q_LORA = 384
KV_LORA = 256
QK_NOPE = 128
QK_ROPE = 64
CONV_W = 3
ROPE_THETA = 10000.0
EPS = 1e-6
SOFTMAX_SCALE = (QK_NOPE + QK_ROPE) ** -0.5
RET_W = RET_HEADS * RET_DV
LANES = 128
ROW_TILE = 16
GROUP_LANES = 512
RET_CHUNK = 128
MXU_TILE = 256
META_ROWS = 128
PROJ_ROWS = 512
FFN_ROWS = 512
ATTN_Q_ROWS = 512
ATTN_K_ROWS = 1024
DECODE_KEY_CHUNK = 1024
DECODE_RET_ROWS = 64
DECODE_SLOTS = 3
QCAT_W = KV_LORA + LANES
LOG_GAMMA = tuple(math.log1p(-(2.0 ** (-5.0 - h))) for h in range(RET_HEADS))
VMEM_LIMIT = 56 * 1024 * 1024
NEG_INF = float("-inf")


def _const_spec(shape):
    nd = len(shape)
    return pl.BlockSpec(shape, lambda *_: (0,) * nd, pipeline_mode=pl.Buffered(1))


def _rms(x, g):
    return x * lax.rsqrt(jnp.mean(x * x, axis=-1, keepdims=True) + EPS) * g


def _proj_kernel(x_ref, c128_ref, s128_ref, c64_ref, s64a_ref, s64b_ref,
                 gmix_ref, win_ref, gq_ref, wuq_ref, gkv_ref, wuk_ref, *rest, tiles_per_seq):
    if tiles_per_seq:
        pre_ckv_ref, pre_kpe_ref = rest[:2]
        rest = rest[2:]
    qr_ref, kr_ref, vr_ref, gr_ref, qcat_ref, kvcat_ref, ckv_ref, kpe_ref = rest
    a = _rms(x_ref[...], gmix_ref[...]).astype(BF16)
    o = 4 * RET_W
    z_lat = jnp.dot(a, win_ref[:, o:], preferred_element_type=F32)
    z = jnp.dot(a, win_ref[:, 0:o], preferred_element_type=F32)
    c128 = c128_ref[...]
    s128 = s128_ref[...]
    c64 = c64_ref[...]
    s64a = s64a_ref[...]
    s64b = s64b_ref[...]

    def rope128(v):
        return v * c128 + pltpu.roll(v, 64, 1) * s128

    def rope64(v):
        return v * c64 + pltpu.roll(v, 96, 1) * s64a + pltpu.roll(v, 32, 1) * s64b

    cqn = _rms(z_lat[:, 0:Q_LORA], gq_ref[...])
    q2 = jnp.dot(cqn.astype(BF16), wuq_ref[...], preferred_element_type=F32)
    for h in range(MLA_HEADS):
        b0 = h * 2 * LANES
        nope = q2[:, b0:b0 + QK_NOPE]
        pe = rope64(q2[:, b0 + LANES:b0 + 2 * LANES])
        qlat = jnp.dot(nope.astype(BF16), wuk_ref[h], preferred_element_type=F32)
        qcat_ref[h, :, 0:KV_LORA] = (qlat * SOFTMAX_SCALE).astype(BF16)
        qcat_ref[h, :, KV_LORA:QCAT_W] = (pe * SOFTMAX_SCALE).astype(BF16)

    ckvn = _rms(z_lat[:, Q_LORA:Q_LORA + KV_LORA], gkv_ref[...])
    kp = rope64(z_lat[:, Q_LORA + KV_LORA:Q_LORA + KV_LORA + LANES])
    kvcat_ref[:, 0:KV_LORA] = ckvn.astype(BF16)
    kvcat_ref[:, KV_LORA:QCAT_W] = kp.astype(BF16)

    for h in range(RET_HEADS):
        sl = slice(h * RET_DK, (h + 1) * RET_DK)
        qr_ref[:, sl] = rope128(z[:, sl]).astype(BF16)
        ksl = slice(RET_W + h * RET_DK, RET_W + (h + 1) * RET_DK)
        kr_ref[:, sl] = (rope128(z[:, ksl]) * (RET_DK ** -0.5)).astype(BF16)
    vr_ref[...] = z[:, 2 * RET_W:3 * RET_W].astype(BF16)
    gr_ref[...] = z[:, 3 * RET_W:4 * RET_W].astype(BF16)

    if tiles_per_seq:
        tm = x_ref.shape[0]
        n_pre = pre_ckv_ref.shape[0]
        t = pl.program_id(0) % tiles_per_seq

        @pl.when(t == 0)
        def _():
            ckv_ref[0, 0:n_pre, :] = pre_ckv_ref[...]
            kpe_ref[0, 0:n_pre, :] = pre_kpe_ref[...]

        rows = pl.ds(pl.multiple_of(n_pre + t * tm, 8), tm)
        ckv_ref[0, rows, :] = ckvn
        kpe_ref[0, rows, :] = kp[:, 0:QK_ROPE]
    else:
        ckv_ref[...] = ckvn
        kpe_ref[...] = kp[:, 0:QK_ROPE]


def _project(x, tabs, wts, tm, prefix=None):
    n, d = x.shape
    nt = tabs[0].shape[0] // tm
    gmix, win, gq, wuq, gkv, wuk = wts
    row = lambda w: pl.BlockSpec((tm, w), lambda i: (i, 0))
    tab = pl.BlockSpec((tm, LANES), lambda i: (i % nt, 0))
    if prefix is None:
        key_shapes = (jax.ShapeDtypeStruct((n, KV_LORA), F32), jax.ShapeDtypeStruct((n, QK_ROPE), F32))
        key_specs = (row(KV_LORA), row(QK_ROPE))
        extra, extra_specs = (), []
    else:
        t_all = prefix[0].shape[0] + nt * tm
        nseq = n // (nt * tm)
        seq_blk = lambda w: pl.BlockSpec((1, t_all, w), lambda i: (i // nt, 0, 0))
        key_shapes = (jax.ShapeDtypeStruct((nseq, t_all, KV_LORA), F32),
                      jax.ShapeDtypeStruct((nseq, t_all, QK_ROPE), F32))
        key_specs = (seq_blk(KV_LORA), seq_blk(QK_ROPE))
        extra, extra_specs = tuple(prefix), [_const_spec(p.shape) for p in prefix]
    out_shapes = (
        jax.ShapeDtypeStruct((n, RET_W), BF16), jax.ShapeDtypeStruct((n, RET_W), BF16),
        jax.ShapeDtypeStruct((n, RET_W), BF16), jax.ShapeDtypeStruct((n, RET_W), BF16),
        jax.ShapeDtypeStruct((MLA_HEADS, n, QCAT_W), BF16),
        jax.ShapeDtypeStruct((n, QCAT_W), BF16)) + key_shapes
    out_specs = (row(RET_W), row(RET_W), row(RET_W), row(RET_W),
                 pl.BlockSpec((MLA_HEADS, tm, QCAT_W), lambda i: (0, i, 0)),
                 row(QCAT_W)) + key_specs
    return pl.pallas_call(
        functools.partial(_proj_kernel, tiles_per_seq=0 if prefix is None else nt),
        grid=(n // tm,),
        in_specs=[row(d), tab, tab, tab, tab, tab,
                  _const_spec(gmix.shape), _const_spec(win.shape), _const_spec(gq.shape),
                  _const_spec(wuq.shape), _const_spec(gkv.shape), _const_spec(wuk.shape)]
        + extra_specs,
        out_specs=out_specs,
        out_shape=out_shapes,
        compiler_params=pltpu.CompilerParams(dimension_semantics=("arbitrary",),
                                             vmem_limit_bytes=VMEM_LIMIT),
        name="proj",
    )(x, *tabs, gmix, win, gq, wuq, gkv, wuk, *extra)


def _ret_chunk_steps(q_ref, k_ref, v_ref, g_ref, o_ref, st_ref, lv):
    L = q_ref.shape[0]
    li = lax.broadcasted_iota(jnp.int32, (L, L), 0)
    mi = lax.broadcasted_iota(jnp.int32, (L, L), 1)
    diff = (li - mi).astype(F32)
    n = lax.broadcasted_iota(jnp.int32, (L, 1), 0).astype(F32)
    for h in range(RET_HEADS):
        lg = LOG_GAMMA[h]
        sl = slice(h * RET_DK, (h + 1) * RET_DK)
        decay = jnp.where(diff >= 0, jnp.exp(lg * jnp.maximum(diff, 0.0)), 0.0)
        qdec = jnp.exp(lg * (n + 1.0))
        kdec = jnp.where(n < lv, jnp.exp(lg * jnp.maximum(lv - 1.0 - n, 0.0)), 0.0)
        q = q_ref[:, sl]
        k = k_ref[:, sl]
        v = v_ref[:, sl]
        s = lax.dot_general(q, k, (((1,), (1,)), ((), ())), preferred_element_type=F32) * decay
        state = st_ref[h]
        qd = (q.astype(F32) * qdec).astype(BF16)
        cross = jnp.dot(qd, state.astype(BF16), preferred_element_type=F32)
        yield
        inner = jnp.dot(s.astype(BF16), v, preferred_element_type=F32)
        o = inner + cross
        kd = (k.astype(F32) * kdec).astype(BF16)
        st_ref[h] = math.exp(lg * lv) * state + lax.dot_general(
            kd, v, (((0,), (0,)), ((), ())), preferred_element_type=F32)
        of = o * lax.rsqrt(jnp.mean(o * o, axis=-1, keepdims=True) + EPS)
        g = g_ref[:, sl].astype(F32)
        o_ref[:, sl] = (of * (g * jax.nn.sigmoid(g))).astype(BF16)
        yield


def _ret_chunk(*args):
    for _ in _ret_chunk_steps(*args):
        pass


def _ret_chunk_kernel(q_ref, k_ref, v_ref, g_ref, s0_ref, o_ref, sf_ref, st_ref, *, lv):
    c = pl.program_id(0)
    nb = q_ref.shape[0]

    @pl.when(c == 0)
    def _():
        for b in range(nb):
            st_ref[b] = s0_ref[0]

    for b in range(nb):
        _ret_chunk(q_ref.at[b], k_ref.at[b], v_ref.at[b], g_ref.at[b], o_ref.at[b],
                   st_ref.at[b], lv)

    @pl.when(c == pl.num_programs(0) - 1)
    def _():
        sf_ref[...] = st_ref[...]


def _retention_chunks(qr, kr, vr, gr, s0, nb, L, lv):
    n = qr.shape[0]
    t = n // nb
    as3 = lambda a: a.reshape(nb, t, RET_W)
    blk = pl.BlockSpec((nb, L, RET_W), lambda c: (0, c, 0))
    return pl.pallas_call(
        functools.partial(_ret_chunk_kernel, lv=float(lv)),
        grid=(t // L,),
        in_specs=[blk, blk, blk, blk, _const_spec(s0.shape)],
        out_specs=(blk, pl.BlockSpec((nb, RET_HEADS, RET_DK, RET_DV), lambda c: (0, 0, 0, 0))),
        out_shape=(jax.ShapeDtypeStruct((nb, t, RET_W), BF16),
                   jax.ShapeDtypeStruct((nb, RET_HEADS, RET_DK, RET_DV), F32)),
        scratch_shapes=[pltpu.VMEM((nb, RET_HEADS, RET_DK, RET_DV), F32)],
        compiler_params=pltpu.CompilerParams(dimension_semantics=("arbitrary",),
                                             vmem_limit_bytes=VMEM_LIMIT),
        name="ret_chunks",
    )(as3(qr), as3(kr), as3(vr), as3(gr), s0)


def _ret_decode_kernel(q_ref, k_ref, v_ref, g_ref, s_ref, o_ref, sn_ref, *, ls):
    R = q_ref.shape[0]
    per_tile = ROW_TILE // ls
    li = lax.broadcasted_iota(jnp.int32, (R, R), 0)
    mi = lax.broadcasted_iota(jnp.int32, (R, R), 1)
    same = (li // ls) == (mi // ls)
    diff = ((li % ls) - (mi % ls)).astype(F32)
    t_col = (lax.broadcasted_iota(jnp.int32, (R, 1), 0) % ls).astype(F32)
    seq_in_tile = lax.broadcasted_iota(jnp.int32, (ROW_TILE, 1), 0) // ls
    for h in range(RET_HEADS):
        lg = LOG_GAMMA[h]
        sl = slice(h * RET_DK, (h + 1) * RET_DK)
        q = q_ref[:, sl]
        k = k_ref[:, sl]
        v = v_ref[:, sl]
        decay = jnp.where(same & (diff >= 0), jnp.exp(lg * jnp.maximum(diff, 0.0)), 0.0)
        s = lax.dot_general(q, k, (((1,), (1,)), ((), ())), preferred_element_type=F32) * decay
        inner = jnp.dot(s.astype(BF16), v, preferred_element_type=F32)
        qd = q.astype(F32) * jnp.exp(lg * (t_col + 1.0))
        kd = k.astype(F32) * jnp.exp(lg * (ls - 1.0 - t_col))
        sdec = math.exp(lg * ls)
        cross_tiles = []
        for tt in range(R // ROW_TILE):
            rows = slice(tt * ROW_TILE, (tt + 1) * ROW_TILE)
            qd_t = qd[rows].astype(BF16)
            kd_t = kd[rows]
            v_t = v[rows]
            acc = jnp.zeros((ROW_TILE, RET_DV), F32)
            for j in range(per_tile):
                b = tt * per_tile + j
                state = s_ref[b, h]
                mine = seq_in_tile == j
                cr = jnp.dot(qd_t, state.astype(BF16), preferred_element_type=F32)
                acc = jnp.where(mine, cr, acc)
                kdm = jnp.where(mine, kd_t, 0.0).astype(BF16)
                sn_ref[b, h] = sdec * state + lax.dot_general(
                    kdm, v_t, (((0,), (0,)), ((), ())), preferred_element_type=F32)
            cross_tiles.append(acc)
        o = inner + jnp.concatenate(cross_tiles, axis=0)
        of = o * lax.rsqrt(jnp.mean(o * o, axis=-1, keepdims=True) + EPS)
        g = g_ref[:, sl].astype(F32)
        o_ref[:, sl] = (of * (g * jax.nn.sigmoid(g))).astype(BF16)


def _retention_decode(qr, kr, vr, gr, state, ls, rows):
    n = qr.shape[0]
    nseq = rows // ls
    row = pl.BlockSpec((rows, RET_W), lambda i: (i, 0))
    st = pl.BlockSpec((nseq, RET_HEADS, RET_DK, RET_DV), lambda i: (i, 0, 0, 0))
    return pl.pallas_call(
        functools.partial(_ret_decode_kernel, ls=ls),
        grid=(n // rows,),
        in_specs=[row, row, row, row, st],
        out_specs=(row, st),
        out_shape=(jax.ShapeDtypeStruct((n, RET_W), BF16),
                   jax.ShapeDtypeStruct(state.shape, F32)),
        compiler_params=pltpu.CompilerParams(dimension_semantics=("arbitrary",),
                                             vmem_limit_bytes=VMEM_LIMIT),
        name="ret_decode",
    )(qr, kr, vr, gr, state)


def _mla_prompt_kernel(q_ref, k_ref, km_ref, o_ref, vt_ref, vmt_ref, m_ref, l_ref, acc_ref, *, tq, tk):
    assert tk % tq == 0 and tk <= 2 * tq
    i = pl.program_id(1)
    nkm = km_ref.shape[0]
    nq = MLA_HEADS * tq

    @pl.when(i == 0)
    def _():
        vmt_ref[...] = km_ref[:, 0:KV_LORA].T
        for c in range(k_ref.shape[0] // tq):
            vt_ref[:, c * tq:(c + 1) * tq] = k_ref[c * tq:(c + 1) * tq, 0:KV_LORA].T

    q_all = q_ref[...].reshape(nq, QCAT_W)
    groups = [slice(g * GROUP_LANES, (g + 1) * GROUP_LANES) for g in range(nq // GROUP_LANES)]

    def scores(kc, grp):
        return lax.dot_general(kc, q_all[grp], (((1,), (1,)), ((), ())),
                               preferred_element_type=F32)

    s = scores(km_ref[...], slice(0, nq))
    s = jnp.where(lax.broadcasted_iota(jnp.int32, (nkm, nq), 0) < N_META, s, NEG_INF)
    m = jnp.max(s, axis=0, keepdims=True)
    p = jnp.exp(s - m)
    m_ref[...] = m
    l_ref[...] = jnp.sum(p, axis=0, keepdims=True)
    acc_ref[...] = jnp.dot(vmt_ref[...], p.astype(BF16), preferred_element_type=F32)

    def update(start, size, mask):
        kc = k_ref[pl.ds(start, size), :]
        vt = vt_ref[:, pl.ds(start, size)]
        ss = [scores(kc, grp) for grp in groups]
        for grp, s in zip(groups, ss):
            if mask is not None:
                s = jnp.where(mask[:, grp], s, NEG_INF)
            m_prev = m_ref[:, grp]
            m_new = jnp.maximum(m_prev, jnp.max(s, axis=0, keepdims=True))
            alpha = jnp.exp(m_prev - m_new)
            p = jnp.exp(s - m_new)
            m_ref[:, grp] = m_new
            l_ref[:, grp] = alpha * l_ref[:, grp] + jnp.sum(p, axis=0, keepdims=True)
            acc_ref[:, grp] = alpha * acc_ref[:, grp] + jnp.dot(
                vt, p.astype(BF16), preferred_element_type=F32)

    n_vis = i * tq

    def body(j, carry):
        update(pl.multiple_of(j * tk, tk), tk, None)
        return carry

    lax.fori_loop(0, n_vis // tk, body, 0)
    if tk > tq:
        @pl.when(n_vis % tk != 0)
        def _():
            update(pl.multiple_of(n_vis - tq, tq), tq, None)

    key = lax.broadcasted_iota(jnp.int32, (tq, nq), 0)
    qry = lax.broadcasted_iota(jnp.int32, (tq, nq), 1) % tq
    update(pl.multiple_of(n_vis, tq), tq, key <= qry)
    out = acc_ref[...] / l_ref[...]
    for h in range(MLA_HEADS):
        o_ref[:, h * KV_LORA:(h + 1) * KV_LORA] = out[:, h * tq:(h + 1) * tq].T.astype(BF16)


def _mla_prompt(qcat, kvcat, kmeta, nb, tq, tk):
    n = kvcat.shape[0]
    t = n // nb
    nq = t // tq
    return pl.pallas_call(
        functools.partial(_mla_prompt_kernel, tq=tq, tk=tk),
        grid=(nb, nq),
        in_specs=[pl.BlockSpec((MLA_HEADS, tq, QCAT_W), lambda b, i: (0, b * nq + i, 0)),
                  pl.BlockSpec((t, QCAT_W), lambda b, i: (b, 0)),
                  _const_spec(kmeta.shape)],
        out_specs=pl.BlockSpec((tq, MLA_HEADS * KV_LORA), lambda b, i: (b * nq + i, 0)),
        out_shape=jax.ShapeDtypeStruct((n, MLA_HEADS * KV_LORA), BF16),
        scratch_shapes=[pltpu.VMEM((KV_LORA, t), BF16), pltpu.VMEM((KV_LORA, kmeta.shape[0]), BF16),
                        pltpu.VMEM((1, MLA_HEADS * tq), F32), pltpu.VMEM((1, MLA_HEADS * tq), F32),
                        pltpu.VMEM((KV_LORA, MLA_HEADS * tq), F32)],
        compiler_params=pltpu.CompilerParams(dimension_semantics=("arbitrary", "arbitrary"),
                                             vmem_limit_bytes=VMEM_LIMIT),
        name="mla_prompt",
    )(qcat, kvcat, kmeta)


def _mla_meta_kernel(q_ref, km_ref, o_ref):
    km = km_ref[...]
    r = q_ref.shape[1]
    row = lax.broadcasted_iota(jnp.int32, (r, km.shape[0]), 0)
    col = lax.broadcasted_iota(jnp.int32, (r, km.shape[0]), 1)
    mask = (col <= row) & (col < N_META)
    for h in range(MLA_HEADS):
        s = lax.dot_general(q_ref[h], km, (((1,), (1,)), ((), ())), preferred_element_type=F32)
        s = jnp.where(mask, s, NEG_INF)
        p = jnp.exp(s - jnp.max(s, axis=-1, keepdims=True))
        acc = jnp.dot(p.astype(BF16), km[:, 0:KV_LORA], preferred_element_type=F32)
        o_ref[:, h * KV_LORA:(h + 1) * KV_LORA] = (
            acc / jnp.sum(p, axis=-1, keepdims=True)).astype(BF16)


def _mla_meta(qcat, kmeta):
    r = qcat.shape[1]
    return pl.pallas_call(
        _mla_meta_kernel,
        out_shape=jax.ShapeDtypeStruct((r, MLA_HEADS * KV_LORA), BF16),
        compiler_params=pltpu.CompilerParams(vmem_limit_bytes=VMEM_LIMIT),
        name="mla_meta",
    )(qcat, kmeta)


def _mla_decode_kernel(pt_ref, q_ref, kself_ref, peself_ref, kv_hbm, pe_hbm, *rest,
                       n_pages, page, ls, chunk, side_lv):
    if side_lv is None:
        o_ref, kvbuf, pebuf, kb16, pb16, s_ref, sem = rest
    else:
        (rq_ref, rk_ref, rv_ref, rg_ref, s0_ref, o_ref, ro_ref, sf_ref,
         kvbuf, pebuf, kb16, pb16, s_ref, sem, st_ref) = rest
    b = pl.program_id(0)
    nb = pl.num_programs(0)
    n_slots = kvbuf.shape[0]
    slot = b % n_slots
    n_keys = n_pages * page

    def page_copies(seq, sl):
        out = []
        for j in range(n_pages):
            pid = pt_ref[seq, j]
            out.append(pltpu.make_async_copy(kv_hbm.at[pid], kvbuf.at[sl, pl.ds(j * page, page)],
                                             sem.at[0, sl]))
            out.append(pltpu.make_async_copy(pe_hbm.at[pid], pebuf.at[sl, j], sem.at[1, sl]))
        return out

    ahead = n_slots - 1

    @pl.when(b == 0)
    def _():
        for first in range(ahead):
            @pl.when(first < nb)
            def _():
                for c in page_copies(first, first):
                    c.start()

    @pl.when(b + ahead < nb)
    def _():
        for c in page_copies(b + ahead, (b + ahead) % n_slots):
            c.start()

    for c in page_copies(b, slot):
        c.wait()

    q = q_ref[0]
    rows = q.shape[0]
    ql = q[:, 0:KV_LORA]
    qp = q[:, KV_LORA:KV_LORA + QK_ROPE]
    ppc = chunk // page
    n_chunks = n_keys // chunk
    side = iter(())
    if side_lv is not None:
        n_seq = st_ref.shape[0]
        seq = b % n_seq

        @pl.when(b < n_seq)
        def _():
            st_ref[seq] = s0_ref[0]

        side = _ret_chunk_steps(rq_ref.at[0], rk_ref.at[0], rv_ref.at[0], rg_ref.at[0],
                                ro_ref.at[0], st_ref.at[seq], side_lv)
    for c in range(n_chunks):
        next(side, None)
        ksl = slice(c * chunk, (c + 1) * chunk)
        kb16[ksl, :] = kvbuf[slot, ksl, :].astype(BF16)
        for j in range(ppc):
            jj = c * ppc + j
            pb16[:, jj * page:(jj + 1) * page] = pebuf[slot, jj].astype(BF16)
        s_ref[:, ksl] = (
            lax.dot_general(ql, kb16[ksl, :], (((1,), (1,)), ((), ())), preferred_element_type=F32)
            + jnp.dot(qp, pb16[:, ksl], preferred_element_type=F32))

    parts = []
    n_split = 4 if n_chunks % 4 == 0 else 1
    for h in range(n_split):
        hsl = slice(h * n_keys // n_split, (h + 1) * n_keys // n_split)
        s_h = s_ref[:, hsl]
        m_h = jnp.max(s_h, axis=-1, keepdims=True)
        p_h = jnp.exp(s_h - m_h)
        parts.append((m_h, jnp.sum(p_h, axis=-1, keepdims=True),
                      jnp.dot(p_h.astype(BF16), kb16[hsl, :], preferred_element_type=F32)))

    qf = q.astype(F32)
    kvs = kself_ref[0]
    pes = peself_ref[0]
    t_row = lax.broadcasted_iota(jnp.int32, (rows, 1), 0) // MLA_HEADS
    ss = []
    for t2 in range(ls):
        s_t = (jnp.sum(qf[:, 0:KV_LORA] * kvs[t2:t2 + 1, :], axis=-1, keepdims=True)
               + jnp.sum(qf[:, KV_LORA:KV_LORA + QK_ROPE] * pes[t2:t2 + 1, :], axis=-1, keepdims=True))
        ss.append(jnp.where(t_row >= t2, s_t, NEG_INF))
    m_s = ss[0]
    for s_t in ss[1:]:
        m_s = jnp.maximum(m_s, s_t)
    l_s = jnp.zeros((rows, 1), F32)
    acc_s = jnp.zeros((rows, KV_LORA), F32)
    for t2 in range(ls):
        p_t = jnp.exp(ss[t2] - m_s)
        l_s = l_s + p_t
        acc_s = acc_s + p_t * kvs[t2:t2 + 1, :]
    parts.append((m_s, l_s, acc_s))

    m = parts[0][0]
    for m_h, _, _ in parts[1:]:
        m = jnp.maximum(m, m_h)
    l = jnp.zeros((rows, 1), F32)
    acc = jnp.zeros((rows, KV_LORA), F32)
    for m_h, l_h, acc_h in parts:
        w_h = jnp.exp(m_h - m)
        l = l + w_h * l_h
        acc = acc + w_h * acc_h
    o_ref[0] = (acc / l).astype(BF16)

    for _ in side:
        pass
    if side_lv is not None:
        @pl.when(b == nb - 1)
        def _():
            sf_ref[...] = st_ref[...]


def _mla_decode(page_table, q, kself, peself, cache_kv, cache_pe_t, chunk, side=None):
    db, rows, _ = q.shape
    ls = kself.shape[1]
    n_pages = page_table.shape[1]
    page = cache_kv.shape[1]
    n_keys = n_pages * page
    in_specs = [pl.BlockSpec((1, rows, QCAT_W), lambda b, pt: (b, 0, 0)),
                pl.BlockSpec((1, ls, KV_LORA), lambda b, pt: (b, 0, 0)),
                pl.BlockSpec((1, ls, QK_ROPE), lambda b, pt: (b, 0, 0)),
                pl.BlockSpec(memory_space=pl.ANY),
                pl.BlockSpec(memory_space=pl.ANY)]
    out_specs = [pl.BlockSpec((1, rows, KV_LORA), lambda b, pt: (b, 0, 0))]
    out_shape = [jax.ShapeDtypeStruct((db, rows, KV_LORA), BF16)]
    scratch = [pltpu.VMEM((DECODE_SLOTS, n_keys, KV_LORA), F32),
               pltpu.VMEM((DECODE_SLOTS, n_pages, QK_ROPE, page), F32),
               pltpu.VMEM((n_keys, KV_LORA), BF16),
               pltpu.VMEM((QK_ROPE, n_keys), BF16),
               pltpu.VMEM((rows, n_keys), F32),
               pltpu.SemaphoreType.DMA((2, DECODE_SLOTS))]
    operands = [page_table, q, kself, peself, cache_kv, cache_pe_t]
    side_lv = None
    if side is not None:
        qr, kr, vr, gr, s0, L = side
        n_seq, t, _ = qr.shape
        assert n_seq * (t // L) == db, "one retention chunk per decode step"
        blk = pl.BlockSpec((1, L, RET_W), lambda b, pt: (b % n_seq, b // n_seq, 0))
        st_shape = (n_seq, RET_HEADS, RET_DK, RET_DV)
        in_specs += [blk, blk, blk, blk, pl.BlockSpec(s0.shape, lambda b, pt: (0, 0, 0, 0))]
        out_specs += [blk, pl.BlockSpec(st_shape, lambda b, pt: (0, 0, 0, 0))]
        out_shape += [jax.ShapeDtypeStruct((n_seq, t, RET_W), BF16),
                      jax.ShapeDtypeStruct(st_shape, F32)]
        scratch += [pltpu.VMEM(st_shape, F32)]
        operands += [qr, kr, vr, gr, s0]
        side_lv = float(L)
    grid_spec = pltpu.PrefetchScalarGridSpec(
        num_scalar_prefetch=1, grid=(db,), in_specs=in_specs, out_specs=out_specs,
        scratch_shapes=scratch)
    out = pl.pallas_call(
        functools.partial(_mla_decode_kernel, n_pages=n_pages, page=page, ls=ls, chunk=chunk,
                          side_lv=side_lv),
        grid_spec=grid_spec,
        out_shape=out_shape,
        compiler_params=pltpu.CompilerParams(dimension_semantics=("arbitrary",),
                                             vmem_limit_bytes=VMEM_LIMIT),
        name="mla_decode",
    )(*operands)
    return out[0] if side is None else tuple(out)


def _mix_ffn_kernel(x_ref, ret_ref, olat_ref, cin_ref, wuv_ref, wout_ref, gffn_ref, wup_ref,
                    cw_ref, cb_ref, wdown_ref, gfin_ref, y_ref, cout_ref,
                    carry_ref, stage_ref, hmid_ref, *, stride, carry_end, cw):
    t = pl.program_id(1)
    tm = x_ref.shape[0]
    hal = carry_ref.shape[0]
    d_ff = wdown_ref.shape[0]

    @pl.when(t == 0)
    def _():
        carry_ref[...] = cin_ref[...]

    mla = jnp.concatenate(
        [jnp.dot(olat_ref[:, h * KV_LORA:(h + 1) * KV_LORA], wuv_ref[h],
                 preferred_element_type=F32).astype(BF16) for h in range(MLA_HEADS)], axis=1)
    mixed = jnp.concatenate([ret_ref[...], mla], axis=1)
    h1 = x_ref[...] + jnp.dot(mixed, wout_ref[...], preferred_element_type=F32)
    a2 = _rms(h1, gffn_ref[...]).astype(BF16)

    def conv_half(c0, slot):
        u = jnp.dot(a2, wup_ref[:, c0:c0 + cw], preferred_element_type=F32)
        stage_ref[slot, 0:hal, :] = carry_ref[:, c0:c0 + cw]
        stage_ref[slot, hal:hal + tm, :] = u
        carry_ref[:, c0:c0 + cw] = u[carry_end - hal:carry_end, :]
        um1 = stage_ref[slot, hal - stride:hal - stride + tm, :]
        um2 = stage_ref[slot, hal - 2 * stride:hal - 2 * stride + tm, :]
        return (cb_ref[:, c0:c0 + cw] + cw_ref[0:1, c0:c0 + cw] * um2
                + cw_ref[1:2, c0:c0 + cw] * um1 + cw_ref[2:3, c0:c0 + cw] * u)

    n_slots = stage_ref.shape[0]
    for jc in range(d_ff // cw):
        ca = conv_half(jc * cw, (2 * jc) % n_slots)
        cg = conv_half(d_ff + jc * cw, (2 * jc + 1) % n_slots)
        hmid_ref[:, jc * cw:(jc + 1) * cw] = (cg * jax.nn.sigmoid(cg) * ca).astype(BF16)
    h2 = h1 + jnp.dot(hmid_ref[...], wdown_ref[...], preferred_element_type=F32)
    y_ref[...] = _rms(h2, gfin_ref[...])

    @pl.when(t == pl.num_programs(1) - 1)
    def _():
        cout_ref[0] = carry_ref[...]


def _mix_ffn(x, ret, olat, carry_in, wts, nseq, tm, stride, carry_end, cw):
    n, d = x.shape
    hal = carry_in.shape[0]
    nt = n // (nseq * tm)
    wuv, wout, gffn, wup, convw, convb, wdown, gfin = wts
    row = lambda w: pl.BlockSpec((tm, w), lambda b, t: (b * nt + t, 0))
    return pl.pallas_call(
        functools.partial(_mix_ffn_kernel, stride=stride, carry_end=carry_end, cw=cw),
        grid=(nseq, nt),
        in_specs=[row(d), row(RET_W), row(MLA_HEADS * KV_LORA), _const_spec(carry_in.shape),
                  _const_spec(wuv.shape), _const_spec(wout.shape), _const_spec(gffn.shape),
                  _const_spec(wup.shape), _const_spec(convw.shape), _const_spec(convb.shape),
                  _const_spec(wdown.shape), _const_spec(gfin.shape)],
        out_specs=(row(d), pl.BlockSpec((1, hal, wup.shape[1]), lambda b, t: (b, 0, 0))),
        out_shape=(jax.ShapeDtypeStruct((n, d), F32),
                   jax.ShapeDtypeStruct((nseq, hal, wup.shape[1]), F32)),
        scratch_shapes=[pltpu.VMEM((hal, wup.shape[1]), F32),
                        pltpu.VMEM((4, hal + tm, cw), F32),
                        pltpu.VMEM((tm, wdown.shape[0]), BF16)],
        compiler_params=pltpu.CompilerParams(dimension_semantics=("arbitrary", "arbitrary"),
                                             vmem_limit_bytes=VMEM_LIMIT),
        name="mix_ffn",
    )(x, ret, olat, carry_in, wuv, wout, gffn, wup, convw, convb, wdown, gfin)


def _rope_tables(pos):
    pos = np.asarray(pos, np.float64)[:, None]

    def cs(dim):
        inv = ROPE_THETA ** (-np.arange(0, dim, 2, dtype=np.float64) / dim)
        ang = pos * inv[None, :]
        return np.cos(ang), np.sin(ang)

    c, s = cs(RET_DK)
    c128 = np.concatenate([c, c], axis=-1)
    s128 = np.concatenate([-s, s], axis=-1)
    c, s = cs(QK_ROPE)
    z32 = np.zeros_like(s)
    z64 = np.zeros((pos.shape[0], LANES - QK_ROPE))
    c64 = np.concatenate([c, c, z64], axis=-1)
    s64a = np.concatenate([-s, z32, z64], axis=-1)
    s64b = np.concatenate([z32, s, z64], axis=-1)
    return tuple(np.asarray(t, np.float32) for t in (c128, s128, c64, s64a, s64b))


def kernel(x_prompt, x_sample, cache_kv_latent, cache_k_rope, state_retention, state_ffn_conv,
           page_table, meta_tokens, g_mix, w_in, g_q, w_uq, g_kv, w_uk, w_uv, w_out,
           g_ffn, w_up, conv_w, conv_b, w_down, g_final):
    nb, seq, d = x_prompt.shape
    db, ls, _ = x_sample.shape
    depth = w_in.shape[0]
    assert depth == 1, "single-layer step"
    n_pages = page_table.shape[1]
    page = cache_kv_latent.shape[2]
    past_len = n_pages * page
    d_ff = w_down.shape[1]
    l = 0

    w = w_in[l]
    win = jnp.concatenate([w, jnp.zeros((d, LANES - QK_ROPE), w.dtype)], axis=1).astype(BF16)
    wq = w_uq[l].reshape(Q_LORA, MLA_HEADS, QK_NOPE + QK_ROPE)
    wuq = jnp.concatenate([wq, jnp.zeros((Q_LORA, MLA_HEADS, LANES - QK_ROPE), wq.dtype)],
                          axis=-1).reshape(Q_LORA, MLA_HEADS * 2 * LANES).astype(BF16)
    wuk = jnp.transpose(w_uk[l], (1, 2, 0)).astype(BF16)
    wuv = jnp.transpose(w_uv[l], (1, 0, 2)).astype(BF16)
    proj_w = (g_mix[l][None, :], win, g_q[l][None, :], wuq, g_kv[l][None, :], wuk)
    ffn_w = (wuv, w_out[l].astype(BF16), g_ffn[l][None, :], w_up[l].astype(BF16),
             conv_w[l], conv_b[l][None, :], w_down[l].astype(BF16), g_final[None, :])

    tile = META_ROWS
    xm = jnp.concatenate([meta_tokens.astype(F32), jnp.zeros((tile - N_META, d), F32)], axis=0)
    tabs_m = _rope_tables(np.arange(tile))
    qr, kr, vr, gr, qcat_m, kvcat_m, ckv_m, kpe_m = _project(xm, tabs_m, proj_w, tile)
    zero_state = jnp.zeros((1, RET_HEADS, RET_DK, RET_DV), F32)
    ret_m, state_m = _retention_chunks(qr, kr, vr, gr, zero_state, 1, tile, N_META)
    ret_m = ret_m.reshape(tile, RET_W)
    row_valid = (jnp.arange(tile) < N_META)[:, None]
    kmeta = jnp.where(row_valid, kvcat_m, jnp.zeros_like(kvcat_m))
    olat_m = _mla_meta(qcat_m, kmeta)
    _, carry_m = _mix_ffn(xm, ret_m, olat_m, jnp.zeros((8, 2 * d_ff), F32), ffn_w,
                          1, tile, 1, N_META, MXU_TILE)

    tm = min(PROJ_ROWS, seq)
    xp = x_prompt.reshape(nb * seq, d)
    tabs_p = _rope_tables(N_META + np.arange(seq))
    qr_p, kr_p, vr_p, gr_p, qcat, kvcat, kv_p, pe_p = _project(
        xp, tabs_p, proj_w, tm, prefix=(ckv_m[:N_META], kpe_m[:N_META]))
    ns = db * ls
    xs = x_sample.reshape(ns, d)
    pos_s = past_len + np.arange(ls)
    tabs_s = tuple(np.tile(t, (db, 1)) for t in _rope_tables(pos_s))
    qr, kr, vr, gr, qcat_s, _, ckv_s, kpe_s = _project(xs, tabs_s, proj_w, min(ns, PROJ_ROWS // 2))

    ret_s, state_s = _retention_decode(qr, kr, vr, gr, state_retention[l], ls, min(ns, DECODE_RET_ROWS))
    q_s = jnp.transpose(qcat_s.reshape(MLA_HEADS, db, ls, QCAT_W), (1, 2, 0, 3)).reshape(
        db, ls * MLA_HEADS, QCAT_W)
    cache_pe_t = jnp.swapaxes(cache_k_rope[l], 1, 2)
    decode_args = (page_table, q_s, ckv_s.reshape(db, ls, KV_LORA), kpe_s.reshape(db, ls, QK_ROPE),
                   cache_kv_latent[l], cache_pe_t, min(DECODE_KEY_CHUNK, past_len))
    if nb * (seq // RET_CHUNK) == db:
        as3 = lambda a: a.reshape(nb, seq, RET_W)
        olat_s, ret_p, state_p = _mla_decode(
            *decode_args, side=(as3(qr_p), as3(kr_p), as3(vr_p), as3(gr_p), state_m, RET_CHUNK))
    else:
        olat_s = _mla_decode(*decode_args)
        ret_p, state_p = _retention_chunks(qr_p, kr_p, vr_p, gr_p, state_m, nb, RET_CHUNK, RET_CHUNK)
    ret_p = ret_p.reshape(nb * seq, RET_W)

    tq = min(ATTN_Q_ROWS, seq)
    olat_p = _mla_prompt(qcat, kvcat, kmeta, nb, tq, min(ATTN_K_ROWS, 2 * tq))
    tf = min(FFN_ROWS, seq)
    y_p, carry_p = _mix_ffn(xp, ret_p, olat_p, carry_m[0], ffn_w, nb, tf, 1, tf, MXU_TILE)

    olat_s = olat_s.reshape(ns, MLA_HEADS * KV_LORA)
    tmaj = lambda a: jnp.transpose(a.reshape(db, ls, a.shape[-1]), (1, 0, 2)).reshape(ns, a.shape[-1])
    carry_s_in = jnp.transpose(state_ffn_conv[l], (1, 0, 2)).reshape((CONV_W - 1) * db, 2 * d_ff)
    y_s, carry_s = _mix_ffn(tmaj(xs), tmaj(ret_s), tmaj(olat_s), carry_s_in, ffn_w,
                            1, ns, db, ns, MXU_TILE)
    y_s = jnp.transpose(y_s.reshape(ls, db, d), (1, 0, 2))
    conv_s = jnp.transpose(carry_s.reshape(CONV_W - 1, db, 2 * d_ff), (1, 0, 2))

    return (y_p.reshape(nb, seq, d), y_s,
            kv_p[None], pe_p[None],
            state_p[None],
            carry_p[:, 8 - (CONV_W - 1):, :][None],
            ckv_s.reshape(db, ls, KV_LORA)[None], kpe_s.reshape(db, ls, QK_ROPE)[None],
            state_s[None], conv_s[None])
```

```python
import functools
import math

import jax
import jax.numpy as jnp
import numpy as np
from jax import lax
from jax.experimental import pallas as pl
from jax.experimental.pallas import tpu as pltpu

F32 = jnp.float32
BF16 = jnp.bfloat16

N_META = 16
RET_HEADS = 4
RET_DK = 128
RET_DV = 128
MLA_HEADS = 4
Q_LORA = 384
KV_LORA = 256
QK_NOPE = 128
QK_ROPE = 64
CONV_W = 3
ROPE_THETA = 10000.0
EPS = 1e-6
SOFTMAX_SCALE = (QK_NOPE + QK_ROPE) ** -0.5
RET_W = RET_HEADS * RET_DV
LANES = 128
ROW_TILE = 16
GROUP_LANES = 512
RET_CHUNK = 128
MXU_TILE = 256
META_ROWS = 128
PROJ_ROWS = 512
FFN_ROWS = 512
ATTN_Q_ROWS = 512
ATTN_K_ROWS = 1024
DECODE_KEY_CHUNK = 1024
DECODE_SOFTMAX_PARTS = 4
DECODE_RET_ROWS = 64
DECODE_SLOTS = 3
QCAT_W = KV_LORA + LANES
LOG_GAMMA = tuple(math.log1p(-(2.0 ** (-5.0 - h))) for h in range(RET_HEADS))
VMEM_LIMIT = 56 * 1024 * 1024
NEG_INF = float("-inf")


def _const_spec(shape):
    nd = len(shape)
    return pl.BlockSpec(shape, lambda *_: (0,) * nd, pipeline_mode=pl.Buffered(1))


def _rms(x, g):
    return x * lax.rsqrt(jnp.mean(x * x, axis=-1, keepdims=True) + EPS) * g


def _proj_kernel(x_ref, c128_ref, s128_ref, c64_ref, s64a_ref, s64b_ref,
                 gmix_ref, win_ref, gq_ref, wuq_ref, gkv_ref, wuk_ref, *rest, tiles_per_seq):
    if tiles_per_seq:
        pre_ckv_ref, pre_kpe_ref = rest[:2]
        rest = rest[2:]
    qr_ref, kr_ref, vr_ref, gr_ref, qcat_ref, kvcat_ref, ckv_ref, kpe_ref = rest
    a = _rms(x_ref[...], gmix_ref[...]).astype(BF16)
    o = 4 * RET_W
    z_lat = jnp.dot(a, win_ref[:, o:], preferred_element_type=F32)
    z = jnp.dot(a, win_ref[:, 0:o], preferred_element_type=F32)
    c128 = c128_ref[...]
    s128 = s128_ref[...]
    c64 = c64_ref[...]
    s64a = s64a_ref[...]
    s64b = s64b_ref[...]

    def rope128(v):
        return v * c128 + pltpu.roll(v, 64, 1) * s128

    def rope64(v):
        return v * c64 + pltpu.roll(v, 96, 1) * s64a + pltpu.roll(v, 32, 1) * s64b

    cqn = _rms(z_lat[:, 0:Q_LORA], gq_ref[...])
    q2 = jnp.dot(cqn.astype(BF16), wuq_ref[...], preferred_element_type=F32)
    for h in range(MLA_HEADS):
        b0 = h * 2 * LANES
        nope = q2[:, b0:b0 + QK_NOPE]
        pe = rope64(q2[:, b0 + LANES:b0 + 2 * LANES])
        qlat = jnp.dot(nope.astype(BF16), wuk_ref[h], preferred_element_type=F32)
        qcat_ref[h, :, 0:KV_LORA] = (qlat * SOFTMAX_SCALE).astype(BF16)
        qcat_ref[h, :, KV_LORA:QCAT_W] = (pe * SOFTMAX_SCALE).astype(BF16)

    ckvn = _rms(z_lat[:, Q_LORA:Q_LORA + KV_LORA], gkv_ref[...])
    kp = rope64(z_lat[:, Q_LORA + KV_LORA:Q_LORA + KV_LORA + LANES])
    kvcat_ref[:, 0:KV_LORA] = ckvn.astype(BF16)
    kvcat_ref[:, KV_LORA:QCAT_W] = kp.astype(BF16)

    for h in range(RET_HEADS):
        sl = slice(h * RET_DK, (h + 1) * RET_DK)
        qr_ref[:, sl] = rope128(z[:, sl]).astype(BF16)
        ksl = slice(RET_W + h * RET_DK, RET_W + (h + 1) * RET_DK)
        kr_ref[:, sl] = (rope128(z[:, ksl]) * (RET_DK ** -0.5)).astype(BF16)
    vr_ref[...] = z[:, 2 * RET_W:3 * RET_W].astype(BF16)
    gr_ref[...] = z[:, 3 * RET_W:4 * RET_W].astype(BF16)

    if tiles_per_seq:
        tm = x_ref.shape[0]
        n_pre = pre_ckv_ref.shape[0]
        t = pl.program_id(0) % tiles_per_seq

        @pl.when(t == 0)
        def _():
            ckv_ref[0, 0:n_pre, :] = pre_ckv_ref[...]
            kpe_ref[0, 0:n_pre, :] = pre_kpe_ref[...]

        rows = pl.ds(pl.multiple_of(n_pre + t * tm, 8), tm)
        ckv_ref[0, rows, :] = ckvn
        kpe_ref[0, rows, :] = kp[:, 0:QK_ROPE]
    else:
        ckv_ref[...] = ckvn
        kpe_ref[...] = kp[:, 0:QK_ROPE]


def _project(x, tabs, wts, tm, prefix=None):
    n, d = x.shape
    nt = tabs[0].shape[0] // tm
    gmix, win, gq, wuq, gkv, wuk = wts
    row = lambda w: pl.BlockSpec((tm, w), lambda i: (i, 0))
    tab = pl.BlockSpec((tm, LANES), lambda i: (i % nt, 0))
    if prefix is None:
        key_shapes = (jax.ShapeDtypeStruct((n, KV_LORA), F32), jax.ShapeDtypeStruct((n, QK_ROPE), F32))
        key_specs = (row(KV_LORA), row(QK_ROPE))
        extra, extra_specs = (), []
    else:
        t_all = prefix[0].shape[0] + nt * tm
        nseq = n // (nt * tm)
        seq_blk = lambda w: pl.BlockSpec((1, t_all, w), lambda i: (i // nt, 0, 0))
        key_shapes = (jax.ShapeDtypeStruct((nseq, t_all, KV_LORA), F32),
                      jax.ShapeDtypeStruct((nseq, t_all, QK_ROPE), F32))
        key_specs = (seq_blk(KV_LORA), seq_blk(QK_ROPE))
        extra, extra_specs = tuple(prefix), [_const_spec(p.shape) for p in prefix]
    out_shapes = (
        jax.ShapeDtypeStruct((n, RET_W), BF16), jax.ShapeDtypeStruct((n, RET_W), BF16),
        jax.ShapeDtypeStruct((n, RET_W), BF16), jax.ShapeDtypeStruct((n, RET_W), BF16),
        jax.ShapeDtypeStruct((MLA_HEADS, n, QCAT_W), BF16),
        jax.ShapeDtypeStruct((n, QCAT_W), BF16)) + key_shapes
    out_specs = (row(RET_W), row(RET_W), row(RET_W), row(RET_W),
                 pl.BlockSpec((MLA_HEADS, tm, QCAT_W), lambda i: (0, i, 0)),
                 row(QCAT_W)) + key_specs
    return pl.pallas_call(
        functools.partial(_proj_kernel, tiles_per_seq=0 if prefix is None else nt),
        grid=(n // tm,),
        in_specs=[row(d), tab, tab, tab, tab, tab,
                  _const_spec(gmix.shape), _const_spec(win.shape), _const_spec(gq.shape),
                  _const_spec(wuq.shape), _const_spec(gkv.shape), _const_spec(wuk.shape)]
        + extra_specs,
        out_specs=out_specs,
        out_shape=out_shapes,
        compiler_params=pltpu.CompilerParams(dimension_semantics=("arbitrary",),
                                             vmem_limit_bytes=VMEM_LIMIT),
        name="proj",
    )(x, *tabs, gmix, win, gq, wuq, gkv, wuk, *extra)


def _ret_chunk_steps(q_ref, k_ref, v_ref, g_ref, o_ref, st_ref, lv):
    L = q_ref.shape[0]
    li = lax.broadcasted_iota(jnp.int32, (L, L), 0)
    mi = lax.broadcasted_iota(jnp.int32, (L, L), 1)
    diff = (li - mi).astype(F32)
    n = lax.broadcasted_iota(jnp.int32, (L, 1), 0).astype(F32)
    for h in range(RET_HEADS):
        lg = LOG_GAMMA[h]
        sl = slice(h * RET_DK, (h + 1) * RET_DK)
        decay = jnp.where(diff >= 0, jnp.exp(lg * jnp.maximum(diff, 0.0)), 0.0)
        qdec = jnp.exp(lg * (n + 1.0))
        kdec = jnp.where(n < lv, jnp.exp(lg * jnp.maximum(lv - 1.0 - n, 0.0)), 0.0)
        q = q_ref[:, sl]
        k = k_ref[:, sl]
        v = v_ref[:, sl]
        s = lax.dot_general(q, k, (((1,), (1,)), ((), ())), preferred_element_type=F32) * decay
        state = st_ref[h]
        qd = (q.astype(F32) * qdec).astype(BF16)
        cross = jnp.dot(qd, state.astype(BF16), preferred_element_type=F32)
        yield
        inner = jnp.dot(s.astype(BF16), v, preferred_element_type=F32)
        o = inner + cross
        kd = (k.astype(F32) * kdec).astype(BF16)
        st_ref[h] = math.exp(lg * lv) * state + lax.dot_general(
            kd, v, (((0,), (0,)), ((), ())), preferred_element_type=F32)
        of = o * lax.rsqrt(jnp.mean(o * o, axis=-1, keepdims=True) + EPS)
        g = g_ref[:, sl].astype(F32)
        o_ref[:, sl] = (of * (g * jax.nn.sigmoid(g))).astype(BF16)
        yield


def _ret_chunk(*args):
    for _ in _ret_chunk_steps(*args):
        pass


def _ret_chunk_kernel(q_ref, k_ref, v_ref, g_ref, s0_ref, o_ref, sf_ref, st_ref, *, lv):
    c = pl.program_id(0)
    nb = q_ref.shape[0]

    @pl.when(c == 0)
    def _():
        for b in range(nb):
            st_ref[b] = s0_ref[0]

    for b in range(nb):
        _ret_chunk(q_ref.at[b], k_ref.at[b], v_ref.at[b], g_ref.at[b], o_ref.at[b],
                   st_ref.at[b], lv)

    @pl.when(c == pl.num_programs(0) - 1)
    def _():
        sf_ref[...] = st_ref[...]


def _retention_chunks(qr, kr, vr, gr, s0, nb, L, lv):
    n = qr.shape[0]
    t = n // nb
    as3 = lambda a: a.reshape(nb, t, RET_W)
    blk = pl.BlockSpec((nb, L, RET_W), lambda c: (0, c, 0))
    return pl.pallas_call(
        functools.partial(_ret_chunk_kernel, lv=float(lv)),
        grid=(t // L,),
        in_specs=[blk, blk, blk, blk, _const_spec(s0.shape)],
        out_specs=(blk, pl.BlockSpec((nb, RET_HEADS, RET_DK, RET_DV), lambda c: (0, 0, 0, 0))),
        out_shape=(jax.ShapeDtypeStruct((nb, t, RET_W), BF16),
                   jax.ShapeDtypeStruct((nb, RET_HEADS, RET_DK, RET_DV), F32)),
        scratch_shapes=[pltpu.VMEM((nb, RET_HEADS, RET_DK, RET_DV), F32)],
        compiler_params=pltpu.CompilerParams(dimension_semantics=("arbitrary",),
                                             vmem_limit_bytes=VMEM_LIMIT),
        name="ret_chunks",
    )(as3(qr), as3(kr), as3(vr), as3(gr), s0)


def _ret_decode_kernel(q_ref, k_ref, v_ref, g_ref, s_ref, o_ref, sn_ref, *, ls):
    R = q_ref.shape[0]
    per_tile = ROW_TILE // ls
    li = lax.broadcasted_iota(jnp.int32, (R, R), 0)
    mi = lax.broadcasted_iota(jnp.int32, (R, R), 1)
    same = (li // ls) == (mi // ls)
    diff = ((li % ls) - (mi % ls)).astype(F32)
    t_col = (lax.broadcasted_iota(jnp.int32, (R, 1), 0) % ls).astype(F32)
    seq_in_tile = lax.broadcasted_iota(jnp.int32, (ROW_TILE, 1), 0) // ls
    for h in range(RET_HEADS):
        lg = LOG_GAMMA[h]
        sl = slice(h * RET_DK, (h + 1) * RET_DK)
        q = q_ref[:, sl]
        k = k_ref[:, sl]
        v = v_ref[:, sl]
        decay = jnp.where(same & (diff >= 0), jnp.exp(lg * jnp.maximum(diff, 0.0)), 0.0)
        s = lax.dot_general(q, k, (((1,), (1,)), ((), ())), preferred_element_type=F32) * decay
        inner = jnp.dot(s.astype(BF16), v, preferred_element_type=F32)
        qd = q.astype(F32) * jnp.exp(lg * (t_col + 1.0))
        kd = k.astype(F32) * jnp.exp(lg * (ls - 1.0 - t_col))
        sdec = math.exp(lg * ls)
        cross_tiles = []
        for tt in range(R // ROW_TILE):
            rows = slice(tt * ROW_TILE, (tt + 1) * ROW_TILE)
            qd_t = qd[rows].astype(BF16)
            kd_t = kd[rows]
            v_t = v[rows]
            acc = jnp.zeros((ROW_TILE, RET_DV), F32)
            for j in range(per_tile):
                b = tt * per_tile + j
                state = s_ref[b, h]
                mine = seq_in_tile == j
                cr = jnp.dot(qd_t, state.astype(BF16), preferred_element_type=F32)
                acc = jnp.where(mine, cr, acc)
                kdm = jnp.where(mine, kd_t, 0.0).astype(BF16)
                sn_ref[b, h] = sdec * state + lax.dot_general(
                    kdm, v_t, (((0,), (0,)), ((), ())), preferred_element_type=F32)
            cross_tiles.append(acc)
        o = inner + jnp.concatenate(cross_tiles, axis=0)
        of = o * lax.rsqrt(jnp.mean(o * o, axis=-1, keepdims=True) + EPS)
        g = g_ref[:, sl].astype(F32)
        o_ref[:, sl] = (of * (g * jax.nn.sigmoid(g))).astype(BF16)


def _retention_decode(qr, kr, vr, gr, state, ls, rows):
    n = qr.shape[0]
    nseq = rows // ls
    row = pl.BlockSpec((rows, RET_W), lambda i: (i, 0))
    st = pl.BlockSpec((nseq, RET_HEADS, RET_DK, RET_DV), lambda i: (i, 0, 0, 0))
    return pl.pallas_call(
        functools.partial(_ret_decode_kernel, ls=ls),
        grid=(n // rows,),
        in_specs=[row, row, row, row, st],
        out_specs=(row, st),
        out_shape=(jax.ShapeDtypeStruct((n, RET_W), BF16),
                   jax.ShapeDtypeStruct(state.shape, F32)),
        compiler_params=pltpu.CompilerParams(dimension_semantics=("arbitrary",),
                                             vmem_limit_bytes=VMEM_LIMIT),
        name="ret_decode",
    )(qr, kr, vr, gr, state)


def _mla_prompt_kernel(q_ref, k_ref, km_ref, o_ref, vt_ref, vmt_ref, m_ref, l_ref, acc_ref, *, tq, tk):
    assert tk % tq == 0 and tk <= 2 * tq
    i = pl.program_id(1)
    nkm = km_ref.shape[0]
    nq = MLA_HEADS * tq

    @pl.when(i == 0)
    def _():
        vmt_ref[...] = km_ref[:, 0:KV_LORA].T
        for c in range(k_ref.shape[0] // tq):
            vt_ref[:, c * tq:(c + 1) * tq] = k_ref[c * tq:(c + 1) * tq, 0:KV_LORA].T

    q_all = q_ref[...].reshape(nq, QCAT_W)
    groups = [slice(g * GROUP_LANES, (g + 1) * GROUP_LANES) for g in range(nq // GROUP_LANES)]

    def scores(kc, grp):
        return lax.dot_general(kc, q_all[grp], (((1,), (1,)), ((), ())),
                               preferred_element_type=F32)

    s = scores(km_ref[...], slice(0, nq))
    s = jnp.where(lax.broadcasted_iota(jnp.int32, (nkm, nq), 0) < N_META, s, NEG_INF)
    m = jnp.max(s, axis=0, keepdims=True)
    p = jnp.exp(s - m)
    m_ref[...] = m
    l_ref[...] = jnp.sum(p, axis=0, keepdims=True)
    acc_ref[...] = jnp.dot(vmt_ref[...], p.astype(BF16), preferred_element_type=F32)

    def update(start, size, mask):
        kc = k_ref[pl.ds(start, size), :]
        vt = vt_ref[:, pl.ds(start, size)]
        ss = [scores(kc, grp) for grp in groups]
        for grp, s in zip(groups, ss):
            if mask is not None:
                s = jnp.where(mask[:, grp], s, NEG_INF)
            m_prev = m_ref[:, grp]
            m_new = jnp.maximum(m_prev, jnp.max(s, axis=0, keepdims=True))
            alpha = jnp.exp(m_prev - m_new)
            p = jnp.exp(s - m_new)
            m_ref[:, grp] = m_new
            l_ref[:, grp] = alpha * l_ref[:, grp] + jnp.sum(p, axis=0, keepdims=True)
            acc_ref[:, grp] = alpha * acc_ref[:, grp] + jnp.dot(
                vt, p.astype(BF16), preferred_element_type=F32)

    n_vis = i * tq

    def body(j, carry):
        update(pl.multiple_of(j * tk, tk), tk, None)
        return carry

    lax.fori_loop(0, n_vis // tk, body, 0)
    if tk > tq:
        @pl.when(n_vis % tk != 0)
        def _():
            update(pl.multiple_of(n_vis - tq, tq), tq, None)

    key = lax.broadcasted_iota(jnp.int32, (tq, nq), 0)
    qry = lax.broadcasted_iota(jnp.int32, (tq, nq), 1) % tq
    update(pl.multiple_of(n_vis, tq), tq, key <= qry)
    out = acc_ref[...] / l_ref[...]
    for h in range(MLA_HEADS):
        o_ref[:, h * KV_LORA:(h + 1) * KV_LORA] = out[:, h * tq:(h + 1) * tq].T.astype(BF16)


def _mla_prompt(qcat, kvcat, kmeta, nb, tq, tk):
    n = kvcat.shape[0]
    t = n // nb
    nq = t // tq
    return pl.pallas_call(
        functools.partial(_mla_prompt_kernel, tq=tq, tk=tk),
        grid=(nb, nq),
        in_specs=[pl.BlockSpec((MLA_HEADS, tq, QCAT_W), lambda b, i: (0, b * nq + i, 0)),
                  pl.BlockSpec((t, QCAT_W), lambda b, i: (b, 0)),
                  _const_spec(kmeta.shape)],
        out_specs=pl.BlockSpec((tq, MLA_HEADS * KV_LORA), lambda b, i: (b * nq + i, 0)),
        out_shape=jax.ShapeDtypeStruct((n, MLA_HEADS * KV_LORA), BF16),
        scratch_shapes=[pltpu.VMEM((KV_LORA, t), BF16), pltpu.VMEM((KV_LORA, kmeta.shape[0]), BF16),
                        pltpu.VMEM((1, MLA_HEADS * tq), F32), pltpu.VMEM((1, MLA_HEADS * tq), F32),
                        pltpu.VMEM((KV_LORA, MLA_HEADS * tq), F32)],
        compiler_params=pltpu.CompilerParams(dimension_semantics=("arbitrary", "arbitrary"),
                                             vmem_limit_bytes=VMEM_LIMIT),
        name="mla_prompt",
    )(qcat, kvcat, kmeta)


def _mla_meta_kernel(q_ref, km_ref, o_ref):
    km = km_ref[...]
    r = q_ref.shape[1]
    row = lax.broadcasted_iota(jnp.int32, (r, km.shape[0]), 0)
    col = lax.broadcasted_iota(jnp.int32, (r, km.shape[0]), 1)
    mask = (col <= row) & (col < N_META)
    for h in range(MLA_HEADS):
        s = lax.dot_general(q_ref[h], km, (((1,), (1,)), ((), ())), preferred_element_type=F32)
        s = jnp.where(mask, s, NEG_INF)
        p = jnp.exp(s - jnp.max(s, axis=-1, keepdims=True))
        acc = jnp.dot(p.astype(BF16), km[:, 0:KV_LORA], preferred_element_type=F32)
        o_ref[:, h * KV_LORA:(h + 1) * KV_LORA] = (
            acc / jnp.sum(p, axis=-1, keepdims=True)).astype(BF16)


def _mla_meta(qcat, kmeta):
    r = qcat.shape[1]
    return pl.pallas_call(
        _mla_meta_kernel,
        out_shape=jax.ShapeDtypeStruct((r, MLA_HEADS * KV_LORA), BF16),
        compiler_params=pltpu.CompilerParams(vmem_limit_bytes=VMEM_LIMIT),
        name="mla_meta",
    )(qcat, kmeta)


def _mla_decode_kernel(pt_ref, q_ref, kself_ref, peself_ref, kv_hbm, pe_hbm, *rest,
                       n_pages, page, ls, chunk, side_lv):
    if side_lv is None:
        o_ref, kvbuf, pebuf, kb16, pb16, s_ref, sem = rest
    else:
        (rq_ref, rk_ref, rv_ref, rg_ref, s0_ref, o_ref, ro_ref, sf_ref,
         kvbuf, pebuf, kb16, pb16, s_ref, sem, st_ref) = rest
    b = pl.program_id(0)
    nb = pl.num_programs(0)
    n_slots = kvbuf.shape[0]
    slot = b % n_slots
    n_keys = n_pages * page

    def page_copies(seq, sl):
        out = []
        for j in range(n_pages):
            pid = pt_ref[seq, j]
            out.append(pltpu.make_async_copy(kv_hbm.at[pid], kvbuf.at[sl, pl.ds(j * page, page)],
                                             sem.at[0, sl]))
            out.append(pltpu.make_async_copy(pe_hbm.at[pid], pebuf.at[sl, j], sem.at[1, sl]))
        return out

    ahead = n_slots - 1

    @pl.when(b == 0)
    def _():
        for first in range(ahead):
            @pl.when(first < nb)
            def _():
                for c in page_copies(first, first):
                    c.start()

    @pl.when(b + ahead < nb)
    def _():
        for c in page_copies(b + ahead, (b + ahead) % n_slots):
            c.start()

    for c in page_copies(b, slot):
        c.wait()

    q = q_ref[0]
    rows = q.shape[0]
    ql = q[:, 0:KV_LORA]
    qp = q[:, KV_LORA:KV_LORA + QK_ROPE]
    ppc = chunk // page
    n_chunks = n_keys // chunk
    side = iter(())
    if side_lv is not None:
        n_seq = st_ref.shape[0]
        seq = b % n_seq

        @pl.when(b < n_seq)
        def _():
            st_ref[seq] = s0_ref[0]

        side = _ret_chunk_steps(rq_ref.at[0], rk_ref.at[0], rv_ref.at[0], rg_ref.at[0],
                                ro_ref.at[0], st_ref.at[seq], side_lv)
    for c in range(n_chunks):
        next(side, None)
        ksl = slice(c * chunk, (c + 1) * chunk)
        kb16[ksl, :] = kvbuf[slot, ksl, :].astype(BF16)
        for j in range(ppc):
            jj = c * ppc + j
            pb16[:, jj * page:(jj + 1) * page] = pebuf[slot, jj].astype(BF16)
        s_ref[:, ksl] = (
            lax.dot_general(ql, kb16[ksl, :], (((1,), (1,)), ((), ())), preferred_element_type=F32)
            + jnp.dot(qp, pb16[:, ksl], preferred_element_type=F32))

    parts = []
    n_split = DECODE_SOFTMAX_PARTS if n_chunks % DECODE_SOFTMAX_PARTS == 0 else 1
    for h in range(n_split):
        hsl = slice(h * n_keys // n_split, (h + 1) * n_keys // n_split)
        s_h = s_ref[:, hsl]
        m_h = jnp.max(s_h, axis=-1, keepdims=True)
        p_h = jnp.exp(s_h - m_h)
        parts.append((m_h, jnp.sum(p_h, axis=-1, keepdims=True),
                      jnp.dot(p_h.astype(BF16), kb16[hsl, :], preferred_element_type=F32)))

    qf = q.astype(F32)
    kvs = kself_ref[0]
    pes = peself_ref[0]
    t_row = lax.broadcasted_iota(jnp.int32, (rows, 1), 0) // MLA_HEADS
    ss = []
    for t2 in range(ls):
        s_t = (jnp.sum(qf[:, 0:KV_LORA] * kvs[t2:t2 + 1, :], axis=-1, keepdims=True)
               + jnp.sum(qf[:, KV_LORA:KV_LORA + QK_ROPE] * pes[t2:t2 + 1, :], axis=-1, keepdims=True))
        ss.append(jnp.where(t_row >= t2, s_t, NEG_INF))
    m_s = ss[0]
    for s_t in ss[1:]:
        m_s = jnp.maximum(m_s, s_t)
    l_s = jnp.zeros((rows, 1), F32)
    acc_s = jnp.zeros((rows, KV_LORA), F32)
    for t2 in range(ls):
        p_t = jnp.exp(ss[t2] - m_s)
        l_s = l_s + p_t
        acc_s = acc_s + p_t * kvs[t2:t2 + 1, :]
    parts.append((m_s, l_s, acc_s))

    m = parts[0][0]
    for m_h, _, _ in parts[1:]:
        m = jnp.maximum(m, m_h)
    l = jnp.zeros((rows, 1), F32)
    acc = jnp.zeros((rows, KV_LORA), F32)
    for m_h, l_h, acc_h in parts:
        w_h = jnp.exp(m_h - m)
        l = l + w_h * l_h
        acc = acc + w_h * acc_h
    o_ref[0] = (acc / l).astype(BF16)

    for _ in side:
        pass
    if side_lv is not None:
        @pl.when(b == nb - 1)
        def _():
            sf_ref[...] = st_ref[...]


def _mla_decode(page_table, q, kself, peself, cache_kv, cache_pe_t, chunk, side=None):
    db, rows, _ = q.shape
    ls = kself.shape[1]
    n_pages = page_table.shape[1]
    page = cache_kv.shape[1]
    n_keys = n_pages * page
    in_specs = [pl.BlockSpec((1, rows, QCAT_W), lambda b, pt: (b, 0, 0)),
                pl.BlockSpec((1, ls, KV_LORA), lambda b, pt: (b, 0, 0)),
                pl.BlockSpec((1, ls, QK_ROPE), lambda b, pt: (b, 0, 0)),
                pl.BlockSpec(memory_space=pl.ANY),
                pl.BlockSpec(memory_space=pl.ANY)]
    out_specs = [pl.BlockSpec((1, rows, KV_LORA), lambda b, pt: (b, 0, 0))]
    out_shape = [jax.ShapeDtypeStruct((db, rows, KV_LORA), BF16)]
    scratch = [pltpu.VMEM((DECODE_SLOTS, n_keys, KV_LORA), F32),
               pltpu.VMEM((DECODE_SLOTS, n_pages, QK_ROPE, page), F32),
               pltpu.VMEM((n_keys, KV_LORA), BF16),
               pltpu.VMEM((QK_ROPE, n_keys), BF16),
               pltpu.VMEM((rows, n_keys), F32),
               pltpu.SemaphoreType.DMA((2, DECODE_SLOTS))]
    operands = [page_table, q, kself, peself, cache_kv, cache_pe_t]
    side_lv = None
    if side is not None:
        qr, kr, vr, gr, s0, L = side
        n_seq, t, _ = qr.shape
        assert n_seq * (t // L) == db, "one retention chunk per decode step"
        blk = pl.BlockSpec((1, L, RET_W), lambda b, pt: (b % n_seq, b // n_seq, 0))
        st_shape = (n_seq, RET_HEADS, RET_DK, RET_DV)
        in_specs += [blk, blk, blk, blk, pl.BlockSpec(s0.shape, lambda b, pt: (0, 0, 0, 0))]
        out_specs += [blk, pl.BlockSpec(st_shape, lambda b, pt: (0, 0, 0, 0))]
        out_shape += [jax.ShapeDtypeStruct((n_seq, t, RET_W), BF16),
                      jax.ShapeDtypeStruct(st_shape, F32)]
        scratch += [pltpu.VMEM(st_shape, F32)]
        operands += [qr, kr, vr, gr, s0]
        side_lv = float(L)
    grid_spec = pltpu.PrefetchScalarGridSpec(
        num_scalar_prefetch=1, grid=(db,), in_specs=in_specs, out_specs=out_specs,
        scratch_shapes=scratch)
    out = pl.pallas_call(
        functools.partial(_mla_decode_kernel, n_pages=n_pages, page=page, ls=ls, chunk=chunk,
                          side_lv=side_lv),
        grid_spec=grid_spec,
        out_shape=out_shape,
        compiler_params=pltpu.CompilerParams(dimension_semantics=("arbitrary",),
                                             vmem_limit_bytes=VMEM_LIMIT),
        name="mla_decode",
    )(*operands)
    return out[0] if side is None else tuple(out)


def _mix_ffn_kernel(x_ref, ret_ref, olat_ref, cin_ref, wuv_ref, wout_ref, gffn_ref, wup_ref,
                    cw_ref, cb_ref, wdown_ref, gfin_ref, y_ref, cout_ref,
                    carry_ref, stage_ref, hmid_ref, *, stride, carry_end, cw):
    t = pl.program_id(1)
    tm = x_ref.shape[0]
    hal = carry_ref.shape[0]
    d_ff = wdown_ref.shape[0]

    @pl.when(t == 0)
    def _():
        carry_ref[...] = cin_ref[...]

    mla = jnp.concatenate(
        [jnp.dot(olat_ref[:, h * KV_LORA:(h + 1) * KV_LORA], wuv_ref[h],
                 preferred_element_type=F32).astype(BF16) for h in range(MLA_HEADS)], axis=1)
    mixed = jnp.concatenate([ret_ref[...], mla], axis=1)
    h1 = x_ref[...] + jnp.dot(mixed, wout_ref[...], preferred_element_type=F32)
    a2 = _rms(h1, gffn_ref[...]).astype(BF16)

    def conv_half(c0, slot):
        u = jnp.dot(a2, wup_ref[:, c0:c0 + cw], preferred_element_type=F32)
        stage_ref[slot, 0:hal, :] = carry_ref[:, c0:c0 + cw]
        stage_ref[slot, hal:hal + tm, :] = u
        carry_ref[:, c0:c0 + cw] = u[carry_end - hal:carry_end, :]
        um1 = stage_ref[slot, hal - stride:hal - stride + tm, :]
        um2 = stage_ref[slot, hal - 2 * stride:hal - 2 * stride + tm, :]
        return (cb_ref[:, c0:c0 + cw] + cw_ref[0:1, c0:c0 + cw] * um2
                + cw_ref[1:2, c0:c0 + cw] * um1 + cw_ref[2:3, c0:c0 + cw] * u)

    n_slots = stage_ref.shape[0]
    for jc in range(d_ff // cw):
        ca = conv_half(jc * cw, (2 * jc) % n_slots)
        cg = conv_half(d_ff + jc * cw, (2 * jc + 1) % n_slots)
        hmid_ref[:, jc * cw:(jc + 1) * cw] = (cg * jax.nn.sigmoid(cg) * ca).astype(BF16)
    h2 = h1 + jnp.dot(hmid_ref[...], wdown_ref[...], preferred_element_type=F32)
    y_ref[...] = _rms(h2, gfin_ref[...])

    @pl.when(t == pl.num_programs(1) - 1)
    def _():
        cout_ref[0] = carry_ref[...]


def _mix_ffn(x, ret, olat, carry_in, wts, nseq, tm, stride, carry_end, cw):
    n, d = x.shape
    hal = carry_in.shape[0]
    nt = n // (nseq * tm)
    wuv, wout, gffn, wup, convw, convb, wdown, gfin = wts
    row = lambda w: pl.BlockSpec((tm, w), lambda b, t: (b * nt + t, 0))
    return pl.pallas_call(
        functools.partial(_mix_ffn_kernel, stride=stride, carry_end=carry_end, cw=cw),
        grid=(nseq, nt),
        in_specs=[row(d), row(RET_W), row(MLA_HEADS * KV_LORA), _const_spec(carry_in.shape),
                  _const_spec(wuv.shape), _const_spec(wout.shape), _const_spec(gffn.shape),
                  _const_spec(wup.shape), _const_spec(convw.shape), _const_spec(convb.shape),
                  _const_spec(wdown.shape), _const_spec(gfin.shape)],
        out_specs=(row(d), pl.BlockSpec((1, hal, wup.shape[1]), lambda b, t: (b, 0, 0))),
        out_shape=(jax.ShapeDtypeStruct((n, d), F32),
                   jax.ShapeDtypeStruct((nseq, hal, wup.shape[1]), F32)),
        scratch_shapes=[pltpu.VMEM((hal, wup.shape[1]), F32),
                        pltpu.VMEM((4, hal + tm, cw), F32),
                        pltpu.VMEM((tm, wdown.shape[0]), BF16)],
        compiler_params=pltpu.CompilerParams(dimension_semantics=("arbitrary", "arbitrary"),
                                             vmem_limit_bytes=VMEM_LIMIT),
        name="mix_ffn",
    )(x, ret, olat, carry_in, wuv, wout, gffn, wup, convw, convb, wdown, gfin)


def _rope_tables(pos):
    pos = np.asarray(pos, np.float64)[:, None]

    def cs(dim):
        inv = ROPE_THETA ** (-np.arange(0, dim, 2, dtype=np.float64) / dim)
        ang = pos * inv[None, :]
        return np.cos(ang), np.sin(ang)

    c, s = cs(RET_DK)
    c128 = np.concatenate([c, c], axis=-1)
    s128 = np.concatenate([-s, s], axis=-1)
    c, s = cs(QK_ROPE)
    z32 = np.zeros_like(s)
    z64 = np.zeros((pos.shape[0], LANES - QK_ROPE))
    c64 = np.concatenate([c, c, z64], axis=-1)
    s64a = np.concatenate([-s, z32, z64], axis=-1)
    s64b = np.concatenate([z32, s, z64], axis=-1)
    return tuple(np.asarray(t, np.float32) for t in (c128, s128, c64, s64a, s64b))


def kernel(x_prompt, x_sample, cache_kv_latent, cache_k_rope, state_retention, state_ffn_conv,
           page_table, meta_tokens, g_mix, w_in, g_q, w_uq, g_kv, w_uk, w_uv, w_out,
           g_ffn, w_up, conv_w, conv_b, w_down, g_final):
    nb, seq, d = x_prompt.shape
    db, ls, _ = x_sample.shape
    depth = w_in.shape[0]
    assert depth == 1, "single-layer step"
    n_pages = page_table.shape[1]
    page = cache_kv_latent.shape[2]
    past_len = n_pages * page
    d_ff = w_down.shape[1]
    l = 0

    w = w_in[l]
    win = jnp.concatenate([w, jnp.zeros((d, LANES - QK_ROPE), w.dtype)], axis=1).astype(BF16)
    wq = w_uq[l].reshape(Q_LORA, MLA_HEADS, QK_NOPE + QK_ROPE)
    wuq = jnp.concatenate([wq, jnp.zeros((Q_LORA, MLA_HEADS, LANES - QK_ROPE), wq.dtype)],
                          axis=-1).reshape(Q_LORA, MLA_HEADS * 2 * LANES).astype(BF16)
    wuk = jnp.transpose(w_uk[l], (1, 2, 0)).astype(BF16)
    wuv = jnp.transpose(w_uv[l], (1, 0, 2)).astype(BF16)
    proj_w = (g_mix[l][None, :], win, g_q[l][None, :], wuq, g_kv[l][None, :], wuk)
    ffn_w = (wuv, w_out[l].astype(BF16), g_ffn[l][None, :], w_up[l].astype(BF16),
             conv_w[l], conv_b[l][None, :], w_down[l].astype(BF16), g_final[None, :])

    tile = META_ROWS
    xm = jnp.concatenate([meta_tokens.astype(F32), jnp.zeros((tile - N_META, d), F32)], axis=0)
    tabs_m = _rope_tables(np.arange(tile))
    qr, kr, vr, gr, qcat_m, kvcat_m, ckv_m, kpe_m = _project(xm, tabs_m, proj_w, tile)
    zero_state = jnp.zeros((1, RET_HEADS, RET_DK, RET_DV), F32)
    ret_m, state_m = _retention_chunks(qr, kr, vr, gr, zero_state, 1, tile, N_META)
    ret_m = ret_m.reshape(tile, RET_W)
    row_valid = (jnp.arange(tile) < N_META)[:, None]
    kmeta = jnp.where(row_valid, kvcat_m, jnp.zeros_like(kvcat_m))
    olat_m = _mla_meta(qcat_m, kmeta)
    _, carry_m = _mix_ffn(xm, ret_m, olat_m, jnp.zeros((8, 2 * d_ff), F32), ffn_w,
                          1, tile, 1, N_META, MXU_TILE)

    tm = min(PROJ_ROWS, seq)
    xp = x_prompt.reshape(nb * seq, d)
    tabs_p = _rope_tables(N_META + np.arange(seq))
    qr_p, kr_p, vr_p, gr_p, qcat, kvcat, kv_p, pe_p = _project(
        xp, tabs_p, proj_w, tm, prefix=(ckv_m[:N_META], kpe_m[:N_META]))
    ns = db * ls
    xs = x_sample.reshape(ns, d)
    pos_s = past_len + np.arange(ls)
    tabs_s = tuple(np.tile(t, (db, 1)) for t in _rope_tables(pos_s))
    qr, kr, vr, gr, qcat_s, _, ckv_s, kpe_s = _project(xs, tabs_s, proj_w, min(ns, PROJ_ROWS // 2))

    ret_s, state_s = _retention_decode(qr, kr, vr, gr, state_retention[l], ls, min(ns, DECODE_RET_ROWS))
    q_s = jnp.transpose(qcat_s.reshape(MLA_HEADS, db, ls, QCAT_W), (1, 2, 0, 3)).reshape(
        db, ls * MLA_HEADS, QCAT_W)
    cache_pe_t = jnp.swapaxes(cache_k_rope[l], 1, 2)
    decode_args = (page_table, q_s, ckv_s.reshape(db, ls, KV_LORA), kpe_s.reshape(db, ls, QK_ROPE),
                   cache_kv_latent[l], cache_pe_t, min(DECODE_KEY_CHUNK, past_len))
    if nb * (seq // RET_CHUNK) == db:
        as3 = lambda a: a.reshape(nb, seq, RET_W)
        olat_s, ret_p, state_p = _mla_decode(
            *decode_args, side=(as3(qr_p), as3(kr_p), as3(vr_p), as3(gr_p), state_m, RET_CHUNK))
    else:
        olat_s = _mla_decode(*decode_args)
        ret_p, state_p = _retention_chunks(qr_p, kr_p, vr_p, gr_p, state_m, nb, RET_CHUNK, RET_CHUNK)
    ret_p = ret_p.reshape(nb * seq, RET_W)

    tq = min(ATTN_Q_ROWS, seq)
    olat_p = _mla_prompt(qcat, kvcat, kmeta, nb, tq, min(ATTN_K_ROWS, 2 * tq))
    tf = min(FFN_ROWS, seq)
    y_p, carry_p = _mix_ffn(xp, ret_p, olat_p, carry_m[0], ffn_w, nb, tf, 1, tf, MXU_TILE)

    olat_s = olat_s.reshape(ns, MLA_HEADS * KV_LORA)
    tmaj = lambda a: jnp.transpose(a.reshape(db, ls, a.shape[-1]), (1, 0, 2)).reshape(ns, a.shape[-1])
    carry_s_in = jnp.transpose(state_ffn_conv[l], (1, 0, 2)).reshape((CONV_W - 1) * db, 2 * d_ff)
    y_s, carry_s = _mix_ffn(tmaj(xs), tmaj(ret_s), tmaj(olat_s), carry_s_in, ffn_w,
                            1, ns, db, ns, MXU_TILE)
    y_s = jnp.transpose(y_s.reshape(ls, db, d), (1, 0, 2))
    conv_s = jnp.transpose(carry_s.reshape(CONV_W - 1, db, 2 * d_ff), (1, 0, 2))

    return (y_p.reshape(nb, seq, d), y_s,
            kv_p[None], pe_p[None],
            state_p[None],
            carry_p[:, 8 - (CONV_W - 1):, :][None],
            ckv_s.reshape(db, ls, KV_LORA)[None], kpe_s.reshape(db, ls, QK_ROPE)[None],
            state_s[None], conv_s[None])
```

```python
import functools
import math

import jax
import jax.numpy as jnp
import numpy as np
from jax import lax
from jax.experimental import pallas as pl
from jax.experimental.pallas import tpu as pltpu

F32 = jnp.float32
BF16 = jnp.bfloat16

N_META = 16
RET_HEADS = 4
RET_DK = 128
RET_DV = 128
MLA_HEADS = 4
Q_LORA = 384
KV_LORA = 256
QK_NOPE = 128
QK_ROPE = 64
CONV_W = 3
ROPE_THETA = 10000.0
EPS = 1e-6
SOFTMAX_SCALE = (QK_NOPE + QK_ROPE) ** -0.5
RET_W = RET_HEADS * RET_DV
LANES = 128
ROW_TILE = 16
GROUP_LANES = 512
RET_CHUNK = 128
MXU_TILE = 256
META_ROWS = 128
PROJ_ROWS = 512
FFN_ROWS = 512
ATTN_Q_ROWS = 512
ATTN_K_ROWS = 1024
DECODE_KEY_CHUNK = 1024
DECODE_SOFTMAX_PARTS = 4
DECODE_RET_ROWS = 64
DECODE_SLOTS = 3
QCAT_W = KV_LORA + LANES
LOG_GAMMA = tuple(math.log1p(-(2.0 ** (-5.0 - h))) for h in range(RET_HEADS))
VMEM_LIMIT = 56 * 1024 * 1024
NEG_INF = float("-inf")


def _const_spec(shape):
    nd = len(shape)
    return pl.BlockSpec(shape, lambda *_: (0,) * nd, pipeline_mode=pl.Buffered(1))


def _rms(x, g):
    return x * lax.rsqrt(jnp.mean(x * x, axis=-1, keepdims=True) + EPS) * g


def _proj_kernel(x_ref, c128_ref, s128_ref, c64_ref, s64a_ref, s64b_ref,
                 gmix_ref, win_ref, gq_ref, wuq_ref, gkv_ref, wuk_ref, *rest, tiles_per_seq):
    if tiles_per_seq:
        pre_ckv_ref, pre_kpe_ref = rest[:2]
        rest = rest[2:]
    qr_ref, kr_ref, vr_ref, gr_ref, qcat_ref, kvcat_ref, ckv_ref, kpe_ref = rest
    a = _rms(x_ref[...], gmix_ref[...]).astype(BF16)
    o = 4 * RET_W
    z_lat = jnp.dot(a, win_ref[:, o:], preferred_element_type=F32)
    z = jnp.dot(a, win_ref[:, 0:o], preferred_element_type=F32)
    c128 = c128_ref[...]
    s128 = s128_ref[...]
    c64 = c64_ref[...]
    s64a = s64a_ref[...]
    s64b = s64b_ref[...]

    def rope128(v):
        return v * c128 + pltpu.roll(v, 64, 1) * s128

    def rope64(v):
        return v * c64 + pltpu.roll(v, 96, 1) * s64a + pltpu.roll(v, 32, 1) * s64b

    cqn = _rms(z_lat[:, 0:Q_LORA], gq_ref[...])
    q2 = jnp.dot(cqn.astype(BF16), wuq_ref[...], preferred_element_type=F32)
    for h in range(MLA_HEADS):
        b0 = h * 2 * LANES
        nope = q2[:, b0:b0 + QK_NOPE]
        pe = rope64(q2[:, b0 + LANES:b0 + 2 * LANES])
        qlat = jnp.dot(nope.astype(BF16), wuk_ref[h], preferred_element_type=F32)
        qcat_ref[h, :, 0:KV_LORA] = (qlat * SOFTMAX_SCALE).astype(BF16)
        qcat_ref[h, :, KV_LORA:QCAT_W] = (pe * SOFTMAX_SCALE).astype(BF16)

    ckvn = _rms(z_lat[:, Q_LORA:Q_LORA + KV_LORA], gkv_ref[...])
    kp = rope64(z_lat[:, Q_LORA + KV_LORA:Q_LORA + KV_LORA + LANES])
    kvcat_ref[:, 0:KV_LORA] = ckvn.astype(BF16)
    kvcat_ref[:, KV_LORA:QCAT_W] = kp.astype(BF16)

    for h in range(RET_HEADS):
        sl = slice(h * RET_DK, (h + 1) * RET_DK)
        qr_ref[:, sl] = rope128(z[:, sl]).astype(BF16)
        ksl = slice(RET_W + h * RET_DK, RET_W + (h + 1) * RET_DK)
        kr_ref[:, sl] = (rope128(z[:, ksl]) * (RET_DK ** -0.5)).astype(BF16)
    vr_ref[...] = z[:, 2 * RET_W:3 * RET_W].astype(BF16)
    gr_ref[...] = z[:, 3 * RET_W:4 * RET_W].astype(BF16)

    if tiles_per_seq:
        tm = x_ref.shape[0]
        n_pre = pre_ckv_ref.shape[0]
        t = pl.program_id(0) % tiles_per_seq

        @pl.when(t == 0)
        def _():
            ckv_ref[0, 0:n_pre, :] = pre_ckv_ref[...]
            kpe_ref[0, 0:n_pre, :] = pre_kpe_ref[...]

        rows = pl.ds(pl.multiple_of(n_pre + t * tm, 8), tm)
        ckv_ref[0, rows, :] = ckvn
        kpe_ref[0, rows, :] = kp[:, 0:QK_ROPE]
    else:
        ckv_ref[...] = ckvn
        kpe_ref[...] = kp[:, 0:QK_ROPE]


def _project(x, tabs, wts, tm, prefix=None):
    n, d = x.shape
    nt = tabs[0].shape[0] // tm
    gmix, win, gq, wuq, gkv, wuk = wts
    row = lambda w: pl.BlockSpec((tm, w), lambda i: (i, 0))
    tab = pl.BlockSpec((tm, LANES), lambda i: (i % nt, 0))
    if prefix is None:
        key_shapes = (jax.ShapeDtypeStruct((n, KV_LORA), F32), jax.ShapeDtypeStruct((n, QK_ROPE), F32))
        key_specs = (row(KV_LORA), row(QK_ROPE))
        extra, extra_specs = (), []
    else:
        t_all = prefix[0].shape[0] + nt * tm
        nseq = n // (nt * tm)
        seq_blk = lambda w: pl.BlockSpec((1, t_all, w), lambda i: (i // nt, 0, 0))
        key_shapes = (jax.ShapeDtypeStruct((nseq, t_all, KV_LORA), F32),
                      jax.ShapeDtypeStruct((nseq, t_all, QK_ROPE), F32))
        key_specs = (seq_blk(KV_LORA), seq_blk(QK_ROPE))
        extra, extra_specs = tuple(prefix), [_const_spec(p.shape) for p in prefix]
    out_shapes = (
        jax.ShapeDtypeStruct((n, RET_W), BF16), jax.ShapeDtypeStruct((n, RET_W), BF16),
        jax.ShapeDtypeStruct((n, RET_W), BF16), jax.ShapeDtypeStruct((n, RET_W), BF16),
        jax.ShapeDtypeStruct((MLA_HEADS, n, QCAT_W), BF16),
        jax.ShapeDtypeStruct((n, QCAT_W), BF16)) + key_shapes
    out_specs = (row(RET_W), row(RET_W), row(RET_W), row(RET_W),
                 pl.BlockSpec((MLA_HEADS, tm, QCAT_W), lambda i: (0, i, 0)),
                 row(QCAT_W)) + key_specs
    return pl.pallas_call(
        functools.partial(_proj_kernel, tiles_per_seq=0 if prefix is None else nt),
        grid=(n // tm,),
        in_specs=[row(d), tab, tab, tab, tab, tab,
                  _const_spec(gmix.shape), _const_spec(win.shape), _const_spec(gq.shape),
                  _const_spec(wuq.shape), _const_spec(gkv.shape), _const_spec(wuk.shape)]
        + extra_specs,
        out_specs=out_specs,
        out_shape=out_shapes,
        compiler_params=pltpu.CompilerParams(dimension_semantics=("arbitrary",),
                                             vmem_limit_bytes=VMEM_LIMIT),
        name="proj",
    )(x, *tabs, gmix, win, gq, wuq, gkv, wuk, *extra)


def _ret_chunk_steps(q_ref, k_ref, v_ref, g_ref, o_ref, st_ref, lv):
    L = q_ref.shape[0]
    li = lax.broadcasted_iota(jnp.int32, (L, L), 0)
    mi = lax.broadcasted_iota(jnp.int32, (L, L), 1)
    diff = (li - mi).astype(F32)
    n = lax.broadcasted_iota(jnp.int32, (L, 1), 0).astype(F32)
    for h in range(RET_HEADS):
        lg = LOG_GAMMA[h]
        sl = slice(h * RET_DK, (h + 1) * RET_DK)
        decay = jnp.where(diff >= 0, jnp.exp(lg * jnp.maximum(diff, 0.0)), 0.0)
        qdec = jnp.exp(lg * (n + 1.0))
        kdec = jnp.where(n < lv, jnp.exp(lg * jnp.maximum(lv - 1.0 - n, 0.0)), 0.0)
        q = q_ref[:, sl]
        k = k_ref[:, sl]
        v = v_ref[:, sl]
        s = lax.dot_general(q, k, (((1,), (1,)), ((), ())), preferred_element_type=F32) * decay
        state = st_ref[h]
        qd = (q.astype(F32) * qdec).astype(BF16)
        cross = jnp.dot(qd, state.astype(BF16), preferred_element_type=F32)
        yield
        inner = jnp.dot(s.astype(BF16), v, preferred_element_type=F32)
        o = inner + cross
        kd = (k.astype(F32) * kdec).astype(BF16)
        st_ref[h] = math.exp(lg * lv) * state + lax.dot_general(
            kd, v, (((0,), (0,)), ((), ())), preferred_element_type=F32)
        of = o * lax.rsqrt(jnp.mean(o * o, axis=-1, keepdims=True) + EPS)
        g = g_ref[:, sl].astype(F32)
        o_ref[:, sl] = (of * (g * jax.nn.sigmoid(g))).astype(BF16)
        yield


def _ret_chunk(*args):
    for _ in _ret_chunk_steps(*args):
        pass


def _ret_chunk_kernel(q_ref, k_ref, v_ref, g_ref, s0_ref, o_ref, sf_ref, st_ref, *, lv):
    c = pl.program_id(0)
    nb = q_ref.shape[0]

    @pl.when(c == 0)
    def _():
        for b in range(nb):
            st_ref[b] = s0_ref[0]

    for b in range(nb):
        _ret_chunk(q_ref.at[b], k_ref.at[b], v_ref.at[b], g_ref.at[b], o_ref.at[b],
                   st_ref.at[b], lv)

    @pl.when(c == pl.num_programs(0) - 1)
    def _():
        sf_ref[...] = st_ref[...]


def _retention_chunks(qr, kr, vr, gr, s0, nb, L, lv):
    n = qr.shape[0]
    t = n // nb
    as3 = lambda a: a.reshape(nb, t, RET_W)
    blk = pl.BlockSpec((nb, L, RET_W), lambda c: (0, c, 0))
    return pl.pallas_call(
        functools.partial(_ret_chunk_kernel, lv=float(lv)),
        grid=(t // L,),
        in_specs=[blk, blk, blk, blk, _const_spec(s0.shape)],
        out_specs=(blk, pl.BlockSpec((nb, RET_HEADS, RET_DK, RET_DV), lambda c: (0, 0, 0, 0))),
        out_shape=(jax.ShapeDtypeStruct((nb, t, RET_W), BF16),
                   jax.ShapeDtypeStruct((nb, RET_HEADS, RET_DK, RET_DV), F32)),
        scratch_shapes=[pltpu.VMEM((nb, RET_HEADS, RET_DK, RET_DV), F32)],
        compiler_params=pltpu.CompilerParams(dimension_semantics=("arbitrary",),
                                             vmem_limit_bytes=VMEM_LIMIT),
        name="ret_chunks",
    )(as3(qr), as3(kr), as3(vr), as3(gr), s0)


def _ret_decode_kernel(q_ref, k_ref, v_ref, g_ref, s_ref, o_ref, sn_ref, *, ls):
    R = q_ref.shape[0]
    per_tile = ROW_TILE // ls
    li = lax.broadcasted_iota(jnp.int32, (R, R), 0)
    mi = lax.broadcasted_iota(jnp.int32, (R, R), 1)
    same = (li // ls) == (mi // ls)
    diff = ((li % ls) - (mi % ls)).astype(F32)
    t_col = (lax.broadcasted_iota(jnp.int32, (R, 1), 0) % ls).astype(F32)
    seq_in_tile = lax.broadcasted_iota(jnp.int32, (ROW_TILE, 1), 0) // ls
    for h in range(RET_HEADS):
        lg = LOG_GAMMA[h]
        sl = slice(h * RET_DK, (h + 1) * RET_DK)
        q = q_ref[:, sl]
        k = k_ref[:, sl]
        v = v_ref[:, sl]
        decay = jnp.where(same & (diff >= 0), jnp.exp(lg * jnp.maximum(diff, 0.0)), 0.0)
        s = lax.dot_general(q, k, (((1,), (1,)), ((), ())), preferred_element_type=F32) * decay
        inner = jnp.dot(s.astype(BF16), v, preferred_element_type=F32)
        qd = q.astype(F32) * jnp.exp(lg * (t_col + 1.0))
        kd = k.astype(F32) * jnp.exp(lg * (ls - 1.0 - t_col))
        sdec = math.exp(lg * ls)
        cross_tiles = []
        for tt in range(R // ROW_TILE):
            rows = slice(tt * ROW_TILE, (tt + 1) * ROW_TILE)
            qd_t = qd[rows].astype(BF16)
            kd_t = kd[rows]
            v_t = v[rows]
            acc = jnp.zeros((ROW_TILE, RET_DV), F32)
            for j in range(per_tile):
                b = tt * per_tile + j
                state = s_ref[b, h]
                mine = seq_in_tile == j
                cr = jnp.dot(qd_t, state.astype(BF16), preferred_element_type=F32)
                acc = jnp.where(mine, cr, acc)
                kdm = jnp.where(mine, kd_t, 0.0).astype(BF16)
                sn_ref[b, h] = sdec * state + lax.dot_general(
                    kdm, v_t, (((0,), (0,)), ((), ())), preferred_element_type=F32)
            cross_tiles.append(acc)
        o = inner + jnp.concatenate(cross_tiles, axis=0)
        of = o * lax.rsqrt(jnp.mean(o * o, axis=-1, keepdims=True) + EPS)
        g = g_ref[:, sl].astype(F32)
        o_ref[:, sl] = (of * (g * jax.nn.sigmoid(g))).astype(BF16)


def _retention_decode(qr, kr, vr, gr, state, ls, rows):
    n = qr.shape[0]
    nseq = rows // ls
    row = pl.BlockSpec((rows, RET_W), lambda i: (i, 0))
    st = pl.BlockSpec((nseq, RET_HEADS, RET_DK, RET_DV), lambda i: (i, 0, 0, 0))
    return pl.pallas_call(
        functools.partial(_ret_decode_kernel, ls=ls),
        grid=(n // rows,),
        in_specs=[row, row, row, row, st],
        out_specs=(row, st),
        out_shape=(jax.ShapeDtypeStruct((n, RET_W), BF16),
                   jax.ShapeDtypeStruct(state.shape, F32)),
        compiler_params=pltpu.CompilerParams(dimension_semantics=("arbitrary",),
                                             vmem_limit_bytes=VMEM_LIMIT),
        name="ret_decode",
    )(qr, kr, vr, gr, state)


def _mla_prompt_kernel(q_ref, k_ref, km_ref, o_ref, vt_ref, vmt_ref, m_ref, l_ref, acc_ref, *, tq, tk):
    assert tk % tq == 0 and tk <= 2 * tq
    i = pl.program_id(1)
    nkm = km_ref.shape[0]
    nq = MLA_HEADS * tq

    @pl.when(i == 0)
    def _():
        vmt_ref[...] = km_ref[:, 0:KV_LORA].T
        for c in range(k_ref.shape[0] // tq):
            vt_ref[:, c * tq:(c + 1) * tq] = k_ref[c * tq:(c + 1) * tq, 0:KV_LORA].T

    q_all = q_ref[...].reshape(nq, QCAT_W)
    groups = [slice(g * GROUP_LANES, (g + 1) * GROUP_LANES) for g in range(nq // GROUP_LANES)]

    def scores(kc, grp):
        return lax.dot_general(kc, q_all[grp], (((1,), (1,)), ((), ())),
                               preferred_element_type=F32)

    s = scores(km_ref[...], slice(0, nq))
    s = jnp.where(lax.broadcasted_iota(jnp.int32, (nkm, nq), 0) < N_META, s, NEG_INF)
    m = jnp.max(s, axis=0, keepdims=True)
    p = jnp.exp(s - m)
    m_ref[...] = m
    l_ref[...] = jnp.sum(p, axis=0, keepdims=True)
    acc_ref[...] = jnp.dot(vmt_ref[...], p.astype(BF16), preferred_element_type=F32)

    def update(start, size, mask):
        kc = k_ref[pl.ds(start, size), :]
        vt = vt_ref[:, pl.ds(start, size)]
        ss = [scores(kc, grp) for grp in groups]
        for grp, s in zip(groups, ss):
            if mask is not None:
                s = jnp.where(mask[:, grp], s, NEG_INF)
            m_prev = m_ref[:, grp]
            m_new = jnp.maximum(m_prev, jnp.max(s, axis=0, keepdims=True))
            alpha = jnp.exp(m_prev - m_new)
            p = jnp.exp(s - m_new)
            m_ref[:, grp] = m_new
            l_ref[:, grp] = alpha * l_ref[:, grp] + jnp.sum(p, axis=0, keepdims=True)
            acc_ref[:, grp] = alpha * acc_ref[:, grp] + jnp.dot(
                vt, p.astype(BF16), preferred_element_type=F32)

    n_vis = i * tq

    def body(j, carry):
        update(pl.multiple_of(j * tk, tk), tk, None)
        return carry

    lax.fori_loop(0, n_vis // tk, body, 0)
    if tk > tq:
        @pl.when(n_vis % tk != 0)
        def _():
            update(pl.multiple_of(n_vis - tq, tq), tq, None)

    key = lax.broadcasted_iota(jnp.int32, (tq, nq), 0)
    qry = lax.broadcasted_iota(jnp.int32, (tq, nq), 1) % tq
    update(pl.multiple_of(n_vis, tq), tq, key <= qry)
    out = acc_ref[...] / l_ref[...]
    for h in range(MLA_HEADS):
        o_ref[:, h * KV_LORA:(h + 1) * KV_LORA] = out[:, h * tq:(h + 1) * tq].T.astype(BF16)


def _mla_prompt(qcat, kvcat, kmeta, nb, tq, tk):
    n = kvcat.shape[0]
    t = n // nb
    nq = t // tq
    return pl.pallas_call(
        functools.partial(_mla_prompt_kernel, tq=tq, tk=tk),
        grid=(nb, nq),
        in_specs=[pl.BlockSpec((MLA_HEADS, tq, QCAT_W), lambda b, i: (0, b * nq + i, 0)),
                  pl.BlockSpec((t, QCAT_W), lambda b, i: (b, 0)),
                  _const_spec(kmeta.shape)],
        out_specs=pl.BlockSpec((tq, MLA_HEADS * KV_LORA), lambda b, i: (b * nq + i, 0)),
        out_shape=jax.ShapeDtypeStruct((n, MLA_HEADS * KV_LORA), BF16),
        scratch_shapes=[pltpu.VMEM((KV_LORA, t), BF16), pltpu.VMEM((KV_LORA, kmeta.shape[0]), BF16),
                        pltpu.VMEM((1, MLA_HEADS * tq), F32), pltpu.VMEM((1, MLA_HEADS * tq), F32),
                        pltpu.VMEM((KV_LORA, MLA_HEADS * tq), F32)],
        compiler_params=pltpu.CompilerParams(dimension_semantics=("arbitrary", "arbitrary"),
                                             vmem_limit_bytes=VMEM_LIMIT),
        name="mla_prompt",
    )(qcat, kvcat, kmeta)


def _mla_meta_kernel(q_ref, km_ref, o_ref):
    km = km_ref[...]
    r = q_ref.shape[1]
    row = lax.broadcasted_iota(jnp.int32, (r, km.shape[0]), 0)
    col = lax.broadcasted_iota(jnp.int32, (r, km.shape[0]), 1)
    mask = (col <= row) & (col < N_META)
    for h in range(MLA_HEADS):
        s = lax.dot_general(q_ref[h], km, (((1,), (1,)), ((), ())), preferred_element_type=F32)
        s = jnp.where(mask, s, NEG_INF)
        p = jnp.exp(s - jnp.max(s, axis=-1, keepdims=True))
        acc = jnp.dot(p.astype(BF16), km[:, 0:KV_LORA], preferred_element_type=F32)
        o_ref[:, h * KV_LORA:(h + 1) * KV_LORA] = (
            acc / jnp.sum(p, axis=-1, keepdims=True)).astype(BF16)


def _mla_meta(qcat, kmeta):
    r = qcat.shape[1]
    return pl.pallas_call(
        _mla_meta_kernel,
        out_shape=jax.ShapeDtypeStruct((r, MLA_HEADS * KV_LORA), BF16),
        compiler_params=pltpu.CompilerParams(vmem_limit_bytes=VMEM_LIMIT),
        name="mla_meta",
    )(qcat, kmeta)


def _mla_decode_kernel(pt_ref, q_ref, kself_ref, peself_ref, kv_hbm, pe_hbm, *rest,
                       n_pages, page, ls, chunk, side_lv):
    if side_lv is None:
        o_ref, kvbuf, pebuf, kb16, pb16, s_ref, sem = rest
    else:
        (rq_ref, rk_ref, rv_ref, rg_ref, s0_ref, o_ref, ro_ref, sf_ref,
         kvbuf, pebuf, kb16, pb16, s_ref, sem, st_ref) = rest
    b = pl.program_id(0)
    nb = pl.num_programs(0)
    n_slots = kvbuf.shape[0]
    slot = b % n_slots
    n_keys = n_pages * page

    def page_copies(seq, sl):
        out = []
        for j in range(n_pages):
            pid = pt_ref[seq, j]
            out.append(pltpu.make_async_copy(kv_hbm.at[pid], kvbuf.at[sl, pl.ds(j * page, page)],
                                             sem.at[0, sl]))
            out.append(pltpu.make_async_copy(pe_hbm.at[pid], pebuf.at[sl, j], sem.at[1, sl]))
        return out

    ahead = n_slots - 1

    @pl.when(b == 0)
    def _():
        for first in range(ahead):
            @pl.when(first < nb)
            def _():
                for c in page_copies(first, first):
                    c.start()

    @pl.when(b + ahead < nb)
    def _():
        for c in page_copies(b + ahead, (b + ahead) % n_slots):
            c.start()

    pltpu.make_async_copy(kvbuf.at[slot], kvbuf.at[slot], sem.at[0, slot]).wait()
    pltpu.make_async_copy(pebuf.at[slot], pebuf.at[slot], sem.at[1, slot]).wait()

    q = q_ref[0]
    rows = q.shape[0]
    ql = q[:, 0:KV_LORA]
    qp = q[:, KV_LORA:KV_LORA + QK_ROPE]
    ppc = chunk // page
    n_chunks = n_keys // chunk
    side = iter(())
    if side_lv is not None:
        n_seq = st_ref.shape[0]
        seq = b % n_seq

        @pl.when(b < n_seq)
        def _():
            st_ref[seq] = s0_ref[0]

        side = _ret_chunk_steps(rq_ref.at[0], rk_ref.at[0], rv_ref.at[0], rg_ref.at[0],
                                ro_ref.at[0], st_ref.at[seq], side_lv)
    for c in range(n_chunks):
        next(side, None)
        ksl = slice(c * chunk, (c + 1) * chunk)
        kb16[ksl, :] = kvbuf[slot, ksl, :].astype(BF16)
        for j in range(ppc):
            jj = c * ppc + j
            pb16[:, jj * page:(jj + 1) * page] = pebuf[slot, jj].astype(BF16)
        s_ref[:, ksl] = (
            lax.dot_general(ql, kb16[ksl, :], (((1,), (1,)), ((), ())), preferred_element_type=F32)
            + jnp.dot(qp, pb16[:, ksl], preferred_element_type=F32))

    parts = []
    n_split = DECODE_SOFTMAX_PARTS if n_chunks % DECODE_SOFTMAX_PARTS == 0 else 1
    for h in range(n_split):
        hsl = slice(h * n_keys // n_split, (h + 1) * n_keys // n_split)
        s_h = s_ref[:, hsl]
        m_h = jnp.max(s_h, axis=-1, keepdims=True)
        p_h = jnp.exp(s_h - m_h)
        parts.append((m_h, jnp.sum(p_h, axis=-1, keepdims=True),
                      jnp.dot(p_h.astype(BF16), kb16[hsl, :], preferred_element_type=F32)))

    qf = q.astype(F32)
    kvs = kself_ref[0]
    pes = peself_ref[0]
    t_row = lax.broadcasted_iota(jnp.int32, (rows, 1), 0) // MLA_HEADS
    ss = []
    for t2 in range(ls):
        s_t = (jnp.sum(qf[:, 0:KV_LORA] * kvs[t2:t2 + 1, :], axis=-1, keepdims=True)
               + jnp.sum(qf[:, KV_LORA:KV_LORA + QK_ROPE] * pes[t2:t2 + 1, :], axis=-1, keepdims=True))
        ss.append(jnp.where(t_row >= t2, s_t, NEG_INF))
    m_s = ss[0]
    for s_t in ss[1:]:
        m_s = jnp.maximum(m_s, s_t)
    l_s = jnp.zeros((rows, 1), F32)
    acc_s = jnp.zeros((rows, KV_LORA), F32)
    for t2 in range(ls):
        p_t = jnp.exp(ss[t2] - m_s)
        l_s = l_s + p_t
        acc_s = acc_s + p_t * kvs[t2:t2 + 1, :]
    parts.append((m_s, l_s, acc_s))

    m = parts[0][0]
    for m_h, _, _ in parts[1:]:
        m = jnp.maximum(m, m_h)
    l = jnp.zeros((rows, 1), F32)
    acc = jnp.zeros((rows, KV_LORA), F32)
    for m_h, l_h, acc_h in parts:
        w_h = jnp.exp(m_h - m)
        l = l + w_h * l_h
        acc = acc + w_h * acc_h
    o_ref[0] = (acc / l).astype(BF16)

    for _ in side:
        pass
    if side_lv is not None:
        @pl.when(b == nb - 1)
        def _():
            sf_ref[...] = st_ref[...]


def _mla_decode(page_table, q, kself, peself, cache_kv, cache_pe_t, chunk, side=None):
    db, rows, _ = q.shape
    ls = kself.shape[1]
    n_pages = page_table.shape[1]
    page = cache_kv.shape[1]
    n_keys = n_pages * page
    in_specs = [pl.BlockSpec((1, rows, QCAT_W), lambda b, pt: (b, 0, 0)),
                pl.BlockSpec((1, ls, KV_LORA), lambda b, pt: (b, 0, 0)),
                pl.BlockSpec((1, ls, QK_ROPE), lambda b, pt: (b, 0, 0)),
                pl.BlockSpec(memory_space=pl.ANY),
                pl.BlockSpec(memory_space=pl.ANY)]
    out_specs = [pl.BlockSpec((1, rows, KV_LORA), lambda b, pt: (b, 0, 0))]
    out_shape = [jax.ShapeDtypeStruct((db, rows, KV_LORA), BF16)]
    scratch = [pltpu.VMEM((DECODE_SLOTS, n_keys, KV_LORA), F32),
               pltpu.VMEM((DECODE_SLOTS, n_pages, QK_ROPE, page), F32),
               pltpu.VMEM((n_keys, KV_LORA), BF16),
               pltpu.VMEM((QK_ROPE, n_keys), BF16),
               pltpu.VMEM((rows, n_keys), F32),
               pltpu.SemaphoreType.DMA((2, DECODE_SLOTS))]
    operands = [page_table, q, kself, peself, cache_kv, cache_pe_t]
    side_lv = None
    if side is not None:
        qr, kr, vr, gr, s0, L = side
        n_seq, t, _ = qr.shape
        assert n_seq * (t // L) == db, "one retention chunk per decode step"
        blk = pl.BlockSpec((1, L, RET_W), lambda b, pt: (b % n_seq, b // n_seq, 0))
        st_shape = (n_seq, RET_HEADS, RET_DK, RET_DV)
        in_specs += [blk, blk, blk, blk, pl.BlockSpec(s0.shape, lambda b, pt: (0, 0, 0, 0))]
        out_specs += [blk, pl.BlockSpec(st_shape, lambda b, pt: (0, 0, 0, 0))]
        out_shape += [jax.ShapeDtypeStruct((n_seq, t, RET_W), BF16),
                      jax.ShapeDtypeStruct(st_shape, F32)]
        scratch += [pltpu.VMEM(st_shape, F32)]
        operands += [qr, kr, vr, gr, s0]
        side_lv = float(L)
    grid_spec = pltpu.PrefetchScalarGridSpec(
        num_scalar_prefetch=1, grid=(db,), in_specs=in_specs, out_specs=out_specs,
        scratch_shapes=scratch)
    out = pl.pallas_call(
        functools.partial(_mla_decode_kernel, n_pages=n_pages, page=page, ls=ls, chunk=chunk,
                          side_lv=side_lv),
        grid_spec=grid_spec,
        out_shape=out_shape,
        compiler_params=pltpu.CompilerParams(dimension_semantics=("arbitrary",),
                                             vmem_limit_bytes=VMEM_LIMIT),
        name="mla_decode",
    )(*operands)
    return out[0] if side is None else tuple(out)


def _mix_ffn_kernel(x_ref, ret_ref, olat_ref, cin_ref, wuv_ref, wout_ref, gffn_ref, wup_ref,
                    cw_ref, cb_ref, wdown_ref, gfin_ref, y_ref, cout_ref,
                    carry_ref, stage_ref, hmid_ref, *, stride, carry_end, cw):
    t = pl.program_id(1)
    tm = x_ref.shape[0]
    hal = carry_ref.shape[0]
    d_ff = wdown_ref.shape[0]

    @pl.when(t == 0)
    def _():
        carry_ref[...] = cin_ref[...]

    mla = jnp.concatenate(
        [jnp.dot(olat_ref[:, h * KV_LORA:(h + 1) * KV_LORA], wuv_ref[h],
                 preferred_element_type=F32).astype(BF16) for h in range(MLA_HEADS)], axis=1)
    mixed = jnp.concatenate([ret_ref[...], mla], axis=1)
    h1 = x_ref[...] + jnp.dot(mixed, wout_ref[...], preferred_element_type=F32)
    a2 = _rms(h1, gffn_ref[...]).astype(BF16)

    def conv_half(c0, slot):
        u = jnp.dot(a2, wup_ref[:, c0:c0 + cw], preferred_element_type=F32)
        stage_ref[slot, 0:hal, :] = carry_ref[:, c0:c0 + cw]
        stage_ref[slot, hal:hal + tm, :] = u
        carry_ref[:, c0:c0 + cw] = u[carry_end - hal:carry_end, :]
        um1 = stage_ref[slot, hal - stride:hal - stride + tm, :]
        um2 = stage_ref[slot, hal - 2 * stride:hal - 2 * stride + tm, :]
        return (cb_ref[:, c0:c0 + cw] + cw_ref[0:1, c0:c0 + cw] * um2
                + cw_ref[1:2, c0:c0 + cw] * um1 + cw_ref[2:3, c0:c0 + cw] * u)

    n_slots = stage_ref.shape[0]
    for jc in range(d_ff // cw):
        ca = conv_half(jc * cw, (2 * jc) % n_slots)
        cg = conv_half(d_ff + jc * cw, (2 * jc + 1) % n_slots)
        hmid_ref[:, jc * cw:(jc + 1) * cw] = (cg * jax.nn.sigmoid(cg) * ca).astype(BF16)
    h2 = h1 + jnp.dot(hmid_ref[...], wdown_ref[...], preferred_element_type=F32)
    y_ref[...] = _rms(h2, gfin_ref[...])

    @pl.when(t == pl.num_programs(1) - 1)
    def _():
        cout_ref[0] = carry_ref[...]


def _mix_ffn(x, ret, olat, carry_in, wts, nseq, tm, stride, carry_end, cw):
    n, d = x.shape
    hal = carry_in.shape[0]
    nt = n // (nseq * tm)
    wuv, wout, gffn, wup, convw, convb, wdown, gfin = wts
    row = lambda w: pl.BlockSpec((tm, w), lambda b, t: (b * nt + t, 0))
    return pl.pallas_call(
        functools.partial(_mix_ffn_kernel, stride=stride, carry_end=carry_end, cw=cw),
        grid=(nseq, nt),
        in_specs=[row(d), row(RET_W), row(MLA_HEADS * KV_LORA), _const_spec(carry_in.shape),
                  _const_spec(wuv.shape), _const_spec(wout.shape), _const_spec(gffn.shape),
                  _const_spec(wup.shape), _const_spec(convw.shape), _const_spec(convb.shape),
                  _const_spec(wdown.shape), _const_spec(gfin.shape)],
        out_specs=(row(d), pl.BlockSpec((1, hal, wup.shape[1]), lambda b, t: (b, 0, 0))),
        out_shape=(jax.ShapeDtypeStruct((n, d), F32),
                   jax.ShapeDtypeStruct((nseq, hal, wup.shape[1]), F32)),
        scratch_shapes=[pltpu.VMEM((hal, wup.shape[1]), F32),
                        pltpu.VMEM((4, hal + tm, cw), F32),
                        pltpu.VMEM((tm, wdown.shape[0]), BF16)],
        compiler_params=pltpu.CompilerParams(dimension_semantics=("arbitrary", "arbitrary"),
                                             vmem_limit_bytes=VMEM_LIMIT),
        name="mix_ffn",
    )(x, ret, olat, carry_in, wuv, wout, gffn, wup, convw, convb, wdown, gfin)


def _rope_tables(pos):
    pos = np.asarray(pos, np.float64)[:, None]

    def cs(dim):
        inv = ROPE_THETA ** (-np.arange(0, dim, 2, dtype=np.float64) / dim)
        ang = pos * inv[None, :]
        return np.cos(ang), np.sin(ang)

    c, s = cs(RET_DK)
    c128 = np.concatenate([c, c], axis=-1)
    s128 = np.concatenate([-s, s], axis=-1)
    c, s = cs(QK_ROPE)
    z32 = np.zeros_like(s)
    z64 = np.zeros((pos.shape[0], LANES - QK_ROPE))
    c64 = np.concatenate([c, c, z64], axis=-1)
    s64a = np.concatenate([-s, z32, z64], axis=-1)
    s64b = np.concatenate([z32, s, z64], axis=-1)
    return tuple(np.asarray(t, np.float32) for t in (c128, s128, c64, s64a, s64b))


def kernel(x_prompt, x_sample, cache_kv_latent, cache_k_rope, state_retention, state_ffn_conv,
           page_table, meta_tokens, g_mix, w_in, g_q, w_uq, g_kv, w_uk, w_uv, w_out,
           g_ffn, w_up, conv_w, conv_b, w_down, g_final):
    nb, seq, d = x_prompt.shape
    db, ls, _ = x_sample.shape
    depth = w_in.shape[0]
    assert depth == 1, "single-layer step"
    n_pages = page_table.shape[1]
    page = cache_kv_latent.shape[2]
    past_len = n_pages * page
    d_ff = w_down.shape[1]
    l = 0

    w = w_in[l]
    win = jnp.concatenate([w, jnp.zeros((d, LANES - QK_ROPE), w.dtype)], axis=1).astype(BF16)
    wq = w_uq[l].reshape(Q_LORA, MLA_HEADS, QK_NOPE + QK_ROPE)
    wuq = jnp.concatenate([wq, jnp.zeros((Q_LORA, MLA_HEADS, LANES - QK_ROPE), wq.dtype)],
                          axis=-1).reshape(Q_LORA, MLA_HEADS * 2 * LANES).astype(BF16)
    wuk = jnp.transpose(w_uk[l], (1, 2, 0)).astype(BF16)
    wuv = jnp.transpose(w_uv[l], (1, 0, 2)).astype(BF16)
    proj_w = (g_mix[l][None, :], win, g_q[l][None, :], wuq, g_kv[l][None, :], wuk)
    ffn_w = (wuv, w_out[l].astype(BF16), g_ffn[l][None, :], w_up[l].astype(BF16),
             conv_w[l], conv_b[l][None, :], w_down[l].astype(BF16), g_final[None, :])

    tile = META_ROWS
    xm = jnp.concatenate([meta_tokens.astype(F32), jnp.zeros((tile - N_META, d), F32)], axis=0)
    tabs_m = _rope_tables(np.arange(tile))
    qr, kr, vr, gr, qcat_m, kvcat_m, ckv_m, kpe_m = _project(xm, tabs_m, proj_w, tile)
    zero_state = jnp.zeros((1, RET_HEADS, RET_DK, RET_DV), F32)
    ret_m, state_m = _retention_chunks(qr, kr, vr, gr, zero_state, 1, tile, N_META)
    ret_m = ret_m.reshape(tile, RET_W)
    row_valid = (jnp.arange(tile) < N_META)[:, None]
    kmeta = jnp.where(row_valid, kvcat_m, jnp.zeros_like(kvcat_m))
    olat_m = _mla_meta(qcat_m, kmeta)
    _, carry_m = _mix_ffn(xm, ret_m, olat_m, jnp.zeros((8, 2 * d_ff), F32), ffn_w,
                          1, tile, 1, N_META, MXU_TILE)

    tm = min(PROJ_ROWS, seq)
    xp = x_prompt.reshape(nb * seq, d)
    tabs_p = _rope_tables(N_META + np.arange(seq))
    qr_p, kr_p, vr_p, gr_p, qcat, kvcat, kv_p, pe_p = _project(
        xp, tabs_p, proj_w, tm, prefix=(ckv_m[:N_META], kpe_m[:N_META]))
    ns = db * ls
    xs = x_sample.reshape(ns, d)
    pos_s = past_len + np.arange(ls)
    tabs_s = tuple(np.tile(t, (db, 1)) for t in _rope_tables(pos_s))
    qr, kr, vr, gr, qcat_s, _, ckv_s, kpe_s = _project(xs, tabs_s, proj_w, min(ns, PROJ_ROWS // 2))

    ret_s, state_s = _retention_decode(qr, kr, vr, gr, state_retention[l], ls, min(ns, DECODE_RET_ROWS))
    q_s = jnp.transpose(qcat_s.reshape(MLA_HEADS, db, ls, QCAT_W), (1, 2, 0, 3)).reshape(
        db, ls * MLA_HEADS, QCAT_W)
    cache_pe_t = jnp.swapaxes(cache_k_rope[l], 1, 2)
    decode_args = (page_table, q_s, ckv_s.reshape(db, ls, KV_LORA), kpe_s.reshape(db, ls, QK_ROPE),
                   cache_kv_latent[l], cache_pe_t, min(DECODE_KEY_CHUNK, past_len))
    if nb * (seq // RET_CHUNK) == db:
        as3 = lambda a: a.reshape(nb, seq, RET_W)
        olat_s, ret_p, state_p = _mla_decode(
            *decode_args, side=(as3(qr_p), as3(kr_p), as3(vr_p), as3(gr_p), state_m, RET_CHUNK))
    else:
        olat_s = _mla_decode(*decode_args)
        ret_p, state_p = _retention_chunks(qr_p, kr_p, vr_p, gr_p, state_m, nb, RET_CHUNK, RET_CHUNK)
    ret_p = ret_p.reshape(nb * seq, RET_W)

    tq = min(ATTN_Q_ROWS, seq)
    olat_p = _mla_prompt(qcat, kvcat, kmeta, nb, tq, min(ATTN_K_ROWS, 2 * tq))
    tf = min(FFN_ROWS, seq)
    y_p, carry_p = _mix_ffn(xp, ret_p, olat_p, carry_m[0], ffn_w, nb, tf, 1, tf, MXU_TILE)

    olat_s = olat_s.reshape(ns, MLA_HEADS * KV_LORA)
    tmaj = lambda a: jnp.transpose(a.reshape(db, ls, a.shape[-1]), (1, 0, 2)).reshape(ns, a.shape[-1])
    carry_s_in = jnp.transpose(state_ffn_conv[l], (1, 0, 2)).reshape((CONV_W - 1) * db, 2 * d_ff)
    y_s, carry_s = _mix_ffn(tmaj(xs), tmaj(ret_s), tmaj(olat_s), carry_s_in, ffn_w,
                            1, ns, db, ns, MXU_TILE)
    y_s = jnp.transpose(y_s.reshape(ls, db, d), (1, 0, 2))
    conv_s = jnp.transpose(carry_s.reshape(CONV_W - 1, db, 2 * d_ff), (1, 0, 2))

    return (y_p.reshape(nb, seq, d), y_s,
            kv_p[None], pe_p[None],
            state_p[None],
            carry_p[:, 8 - (CONV_W - 1):, :][None],
            ckv_s.reshape(db, ls, KV_LORA)[None], kpe_s.reshape(db, ls, QK_ROPE)[None],
            state_s[None], conv_s[None])
```
